```python
import math
import jax, jax.numpy as jnp
from jax import lax
import numpy as np

D_MODEL = 4096
BATCH = 4
SEQ = 2048
DEPTH = 2

N_MIXERS = 2
N_ATTN_LAYERS = (DEPTH + N_MIXERS - 1) // N_MIXERS
N_SSM_LAYERS = DEPTH // N_MIXERS
HEAD_DIM = 128
N_HEADS = D_MODEL // HEAD_DIM
N_KV_HEADS = N_HEADS // 4
GQA_GROUP = N_HEADS // N_KV_HEADS
WINDOW = 128
BLOCK = 128
NUM_BUCKETS = 32
MAX_DISTANCE = WINDOW
SSM_GROUP_CH = 16
SSM_GROUPS = D_MODEL // SSM_GROUP_CH
SSM_STATE = 64
D_FF = ((8 * D_MODEL // 3 + 255) // 256) * 256
RMS_EPS = 1e-6
NEG_INF = -1e30

kernel_name = 'hybrid_swa_s5_convffn_encoder'


def rms_norm(x, g):
    xf = x.astype(jnp.float32)
    y = xf * lax.rsqrt(jnp.mean(xf * xf, axis=-1, keepdims=True) + RMS_EPS)
    return (y * g.astype(jnp.float32)).astype(x.dtype)


def t5_bucket(rel):
    half = NUM_BUCKETS // 2
    max_exact = half // 2
    base = jnp.where(rel > 0, half, 0)
    n = jnp.abs(rel)
    nf = jnp.maximum(n, 1).astype(jnp.float32)
    large = max_exact + (jnp.log(nf / max_exact) / math.log(MAX_DISTANCE / max_exact)
                         * (half - max_exact)).astype(jnp.int32)
    large = jnp.minimum(large, half - 1)
    return base + jnp.where(n < max_exact, n, large)


def windowed_gqa(h, wqkv, sink, wo, rel_bias):
    B, L, _ = h.shape
    nblk = L // BLOCK
    qkv = h @ wqkv
    q_w = N_HEADS * HEAD_DIM
    kv_w = N_KV_HEADS * HEAD_DIM
    q = qkv[..., :q_w].reshape(B, nblk, BLOCK, N_KV_HEADS, GQA_GROUP, HEAD_DIM)
    k = qkv[..., q_w:q_w + kv_w].reshape(B, L, N_KV_HEADS, HEAD_DIM)
    v = qkv[..., q_w + kv_w:].reshape(B, L, N_KV_HEADS, HEAD_DIM)

    def band(t):
        tp = jnp.pad(t, ((0, 0), (BLOCK, BLOCK), (0, 0), (0, 0)))
        tp = tp.reshape(B, nblk + 2, BLOCK, N_KV_HEADS, HEAD_DIM)
        return jnp.concatenate([tp[:, :-2], tp[:, 1:-1], tp[:, 2:]], axis=2)

    kb, vb = band(k), band(v)
    s = jnp.einsum('bnqkgd,bnskd->bnkgqs', q, kb).astype(jnp.float32) * (HEAD_DIM ** -0.5)

    q_idx = jnp.arange(BLOCK)[:, None]
    k_idx = jnp.arange(3 * BLOCK)[None, :]
    rel = k_idx - BLOCK - q_idx
    bias = rel_bias.astype(jnp.float32)[t5_bucket(rel)]
    bias = jnp.transpose(bias, (2, 0, 1)).reshape(N_KV_HEADS, GQA_GROUP, BLOCK, 3 * BLOCK)
    key_pos = (jnp.arange(nblk)[:, None] - 1) * BLOCK + jnp.arange(3 * BLOCK)[None, :]
    valid = (jnp.abs(rel) <= WINDOW)[None] & ((key_pos >= 0) & (key_pos < L))[:, None, :]
    s = jnp.where(valid[None, :, None, None], s + bias, NEG_INF)

    sink_l = sink.astype(jnp.float32).reshape(N_KV_HEADS, GQA_GROUP)[None, None, :, :, None, None]
    m = jnp.maximum(jnp.max(s, axis=-1, keepdims=True), sink_l)
    p = jnp.exp(s - m)
    w = p / (jnp.sum(p, axis=-1, keepdims=True) + jnp.exp(sink_l - m))
    o = jnp.einsum('bnkgqs,bnskd->bnqkgd', w.astype(vb.dtype), vb)
    return o.reshape(B, L, N_HEADS * HEAD_DIM) @ wo


def _recurrence_combine(left, right):
    a_l, b_l = left
    a_r, b_r = right
    return a_r * a_l, a_r * b_l + b_r


def s5_bidirectional(h, w_in, lam_re, lam_im, log_dt, b_re, b_im, c_re, c_im, d_skip, w_glu):
    B, L, _ = h.shape
    u = (h @ w_in).astype(jnp.float32)
    ug = u.reshape(B, L, SSM_GROUPS, SSM_GROUP_CH)
    y = u * d_skip.astype(jnp.float32)
    for dirn, reverse in ((0, False), (1, True)):
        lam = lax.complex(lam_re[dirn].astype(jnp.float32), lam_im[dirn].astype(jnp.float32))
        dt = jnp.exp(log_dt[dirn].astype(jnp.float32))[:, None]
        lam_bar = jnp.exp(lam * dt)
        b_mat = lax.complex(b_re[dirn].astype(jnp.float32), b_im[dirn].astype(jnp.float32))
        b_bar = ((lam_bar - 1.0) / lam)[..., None] * b_mat
        c_mat = lax.complex(c_re[dirn].astype(jnp.float32), c_im[dirn].astype(jnp.float32))
        bu = jnp.einsum('blgh,gph->blgp', ug, b_bar)
        a = jnp.broadcast_to(lam_bar, (1, L, SSM_GROUPS, SSM_STATE))
        _, states = lax.associative_scan(_recurrence_combine, (a, bu), axis=1, reverse=reverse)
        y = y + jnp.real(jnp.einsum('blgp,ghp->blgh', states, c_mat)).reshape(B, L, D_MODEL)
    z = jax.nn.gelu(y).astype(h.dtype) @ w_glu
    return z[..., :D_MODEL] * jax.nn.sigmoid(z[..., D_MODEL:])


def conv_ffn(h, w_gate, w_up, conv_w, conv_b, w_down):
    g = h @ w_gate
    gp = jnp.pad(g, ((0, 0), (1, 1), (0, 0)))
    gc = conv_w[0] * gp[:, :-2] + conv_w[1] * gp[:, 1:-1] + conv_w[2] * gp[:, 2:] + conv_b
    return (jax.nn.silu(gc) * (h @ w_up)) @ w_down


def setup_inputs(seed: int = 0) -> dict:
    key = jax.random.key(seed)
    ks = jax.random.split(key, 24)
    f32 = jnp.float32

    def nrm(k, shape, scale):
        return jax.random.normal(k, shape, f32) * scale

    qkv_out = (N_HEADS + 2 * N_KV_HEADS) * HEAD_DIM
    ssm_state_shape = (N_SSM_LAYERS, 2, SSM_GROUPS, SSM_STATE)
    return {
        'x': nrm(ks[0], (BATCH, SEQ, D_MODEL), 1.0),
        'rel_bias': nrm(ks[1], (NUM_BUCKETS, N_HEADS), 0.5),
        'pre_mix_norm': 1.0 + nrm(ks[2], (DEPTH, D_MODEL), 0.05),
        'post_mix_norm': 1.0 + nrm(ks[3], (DEPTH, D_MODEL), 0.05),
        'pre_ffn_norm': 1.0 + nrm(ks[4], (DEPTH, D_MODEL), 0.05),
        'post_ffn_norm': 1.0 + nrm(ks[5], (DEPTH, D_MODEL), 0.05),
        'attn_wqkv': nrm(ks[6], (N_ATTN_LAYERS, D_MODEL, qkv_out), D_MODEL ** -0.5),
        'attn_sink': nrm(ks[7], (N_ATTN_LAYERS, N_HEADS), 1.0),
        'attn_wo': nrm(ks[8], (N_ATTN_LAYERS, N_HEADS * HEAD_DIM, D_MODEL), (N_HEADS * HEAD_DIM) ** -0.5),
        'ssm_w_in': nrm(ks[9], (N_SSM_LAYERS, D_MODEL, D_MODEL), D_MODEL ** -0.5),
        'ssm_lambda_re': -0.5 + nrm(ks[10], ssm_state_shape, 0.01),
        'ssm_lambda_im': math.pi * jnp.arange(SSM_STATE, dtype=f32) + nrm(ks[11], ssm_state_shape, 0.01),
        'ssm_log_dt': jax.random.uniform(ks[12], (N_SSM_LAYERS, 2, SSM_GROUPS), f32,
                                         math.log(1e-3), math.log(1e-1)),
        'ssm_b_re': nrm(ks[13], (N_SSM_LAYERS, 2, SSM_GROUPS, SSM_STATE, SSM_GROUP_CH), (2 * SSM_GROUP_CH) ** -0.5),
        'ssm_b_im': nrm(ks[14], (N_SSM_LAYERS, 2, SSM_GROUPS, SSM_STATE, SSM_GROUP_CH), (2 * SSM_GROUP_CH) ** -0.5),
        'ssm_c_re': nrm(ks[15], (N_SSM_LAYERS, 2, SSM_GROUPS, SSM_GROUP_CH, SSM_STATE), (2 * SSM_STATE) ** -0.5),
        'ssm_c_im': nrm(ks[16], (N_SSM_LAYERS, 2, SSM_GROUPS, SSM_GROUP_CH, SSM_STATE), (2 * SSM_STATE) ** -0.5),
        'ssm_d': nrm(ks[17], (N_SSM_LAYERS, D_MODEL), 1.0),
        'ssm_w_glu': nrm(ks[18], (N_SSM_LAYERS, D_MODEL, 2 * D_MODEL), D_MODEL ** -0.5),
        'ffn_w_gate': nrm(ks[19], (DEPTH, D_MODEL, D_FF), D_MODEL ** -0.5),
        'ffn_w_up': nrm(ks[20], (DEPTH, D_MODEL, D_FF), D_MODEL ** -0.5),
        'ffn_conv_w': nrm(ks[21], (DEPTH, 3, D_FF), 3 ** -0.5),
        'ffn_conv_b': nrm(ks[22], (DEPTH, D_FF), 0.02),
        'ffn_w_down': nrm(ks[23], (DEPTH, D_FF, D_MODEL), D_FF ** -0.5),
    }


def reference(x, rel_bias, pre_mix_norm, post_mix_norm, pre_ffn_norm, post_ffn_norm,
              attn_wqkv, attn_sink, attn_wo, ssm_w_in, ssm_lambda_re, ssm_lambda_im,
              ssm_log_dt, ssm_b_re, ssm_b_im, ssm_c_re, ssm_c_im, ssm_d, ssm_w_glu,
              ffn_w_gate, ffn_w_up, ffn_conv_w, ffn_conv_b, ffn_w_down):
    for i in range(DEPTH):
        j = i // N_MIXERS
        h = rms_norm(x, pre_mix_norm[i])
        if i % N_MIXERS == 0:
            m = windowed_gqa(h, attn_wqkv[j], attn_sink[j], attn_wo[j], rel_bias)
        else:
            m = s5_bidirectional(h, ssm_w_in[j], ssm_lambda_re[j], ssm_lambda_im[j], ssm_log_dt[j],
                                 ssm_b_re[j], ssm_b_im[j], ssm_c_re[j], ssm_c_im[j], ssm_d[j], ssm_w_glu[j])
        x = x + rms_norm(m, post_mix_norm[i])
        h = rms_norm(x, pre_ffn_norm[i])
        f = conv_ffn(h, ffn_w_gate[i], ffn_w_up[i], ffn_conv_w[i], ffn_conv_b[i], ffn_w_down[i])
        x = x + rms_norm(f, post_ffn_norm[i])
    return x
```

```python
import functools
import math

import jax
import jax.numpy as jnp
from jax import lax
from jax.experimental import pallas as pl
from jax.experimental.pallas import tpu as pltpu

F32 = jnp.float32
BF16 = jnp.bfloat16

HEAD_DIM = 128
ATTN_BLOCK = 128
NUM_BUCKETS = 32
SSM_GROUP_CH = 16
SSM_STATE = 64
SSM_CHUNK = 16
RMS_EPS = 1e-6
NEG_INF = -1e30

MIB = 1024 * 1024


def _cparams(n_grid_dims, vmem_mib):
    return pltpu.CompilerParams(
        dimension_semantics=("arbitrary",) * n_grid_dims,
        vmem_limit_bytes=vmem_mib * MIB,
    )


def _pick(n, prefs):
    for p in prefs:
        if p <= n and n % p == 0:
            return p
    return n


def _rms(x, g):
    return x * lax.rsqrt(jnp.mean(x * x, axis=-1, keepdims=True) + RMS_EPS) * g


def _norm_kernel(x_ref, g_ref, h_ref):
    h_ref[...] = _rms(x_ref[...], g_ref[...]).astype(h_ref.dtype)


def _norm_cast(x, g):
    m, d = x.shape
    tm = _pick(m, (256, 128, 64, 32, 16, 8))
    return pl.pallas_call(
        _norm_kernel,
        grid=(m // tm,),
        in_specs=[pl.BlockSpec((tm, d), lambda i: (i, 0)),
                  pl.BlockSpec((1, d), lambda i: (0, 0))],
        out_specs=pl.BlockSpec((tm, d), lambda i: (i, 0)),
        out_shape=jax.ShapeDtypeStruct((m, d), BF16),
        compiler_params=_cparams(1, 32),
        name="norm_cast",
    )(x, g.reshape(1, d))


def _resid_norm_kernel(x_ref, m_ref, g1_ref, g2_ref, xo_ref, ho_ref):
    xn = x_ref[...] + _rms(m_ref[...], g1_ref[...])
    xo_ref[...] = xn
    ho_ref[...] = _rms(xn, g2_ref[...]).astype(ho_ref.dtype)


def _resid_kernel(x_ref, m_ref, g1_ref, xo_ref):
    xo_ref[...] = x_ref[...] + _rms(m_ref[...], g1_ref[...])


def _resid_norm(x, mix, g_post, g_next):
    m, d = x.shape
    tm = _pick(m, (128, 64, 32, 16, 8))
    row = pl.BlockSpec((tm, d), lambda i: (i, 0))
    vec = pl.BlockSpec((1, d), lambda i: (0, 0))
    if g_next is None:
        return pl.pallas_call(
            _resid_kernel, grid=(m // tm,),
            in_specs=[row, row, vec], out_specs=row,
            out_shape=jax.ShapeDtypeStruct((m, d), F32),
            compiler_params=_cparams(1, 32), name="resid",
        )(x, mix, g_post.reshape(1, d)), None
    return pl.pallas_call(
        _resid_norm_kernel, grid=(m // tm,),
        in_specs=[row, row, vec, vec], out_specs=[row, row],
        out_shape=[jax.ShapeDtypeStruct((m, d), F32), jax.ShapeDtypeStruct((m, d), BF16)],
        compiler_params=_cparams(1, 32), name="resid_norm",
    )(x, mix, g_post.reshape(1, d), g_next.reshape(1, d))


def _fetch_row_panel(a_hbm, a_ref, sem):
    @pl.when(pl.program_id(1) == 0)
    def _():
        tm = a_ref.shape[0]
        cp = pltpu.make_async_copy(
            a_hbm.at[pl.ds(pl.multiple_of(pl.program_id(0) * tm, tm), tm), :], a_ref, sem)
        cp.start()
        cp.wait()


def _panel_scratch(tm, k):
    return [pltpu.VMEM((tm, k), BF16), pltpu.SemaphoreType.DMA(())]


def _mm_kernel(a_hbm, w_ref, o_ref, a_ref, sem, *, kc):
    _fetch_row_panel(a_hbm, a_ref, sem)
    k = a_ref.shape[1]
    acc = None
    for k0 in range(0, k, kc):
        part = jnp.dot(a_ref[:, k0:k0 + kc], w_ref[k0:k0 + kc, :].astype(BF16),
                       preferred_element_type=F32)
        acc = part if acc is None else acc + part
    o_ref[...] = acc.astype(o_ref.dtype)


def _mm(a, w, out_dtype, tm, tn, kc=None, vmem_mib=56):
    m, k = a.shape
    n = w.shape[1]
    kc = k if kc is None else kc
    return pl.pallas_call(
        functools.partial(_mm_kernel, kc=kc),
        grid=(m // tm, n // tn),
        in_specs=[pl.BlockSpec(memory_space=pl.ANY),
                  pl.BlockSpec((k, tn), lambda i, j: (0, j))],
        out_specs=pl.BlockSpec((tm, tn), lambda i, j: (i, j)),
        out_shape=jax.ShapeDtypeStruct((m, n), out_dtype),
        scratch_shapes=_panel_scratch(tm, k),
        compiler_params=_cparams(2, vmem_mib),
        name="mm",
    )(a, w)


def _glu_kernel(a_hbm, wa_ref, wb_ref, o_ref, a_ref, sem):
    _fetch_row_panel(a_hbm, a_ref, sem)
    a = a_ref[...]
    ya = jnp.dot(a, wa_ref[...].astype(BF16), preferred_element_type=F32)
    yb = jnp.dot(a, wb_ref[...].astype(BF16), preferred_element_type=F32)
    o_ref[...] = (ya * jax.nn.sigmoid(yb)).astype(o_ref.dtype)


def _glu(a, w, tm, tn):
    m, k = a.shape
    n = w.shape[1] // 2
    nj = n // tn
    return pl.pallas_call(
        _glu_kernel,
        grid=(m // tm, nj),
        in_specs=[pl.BlockSpec(memory_space=pl.ANY),
                  pl.BlockSpec((k, tn), lambda i, j: (0, j)),
                  pl.BlockSpec((k, tn), lambda i, j: (0, j + nj))],
        out_specs=pl.BlockSpec((tm, tn), lambda i, j: (i, j)),
        out_shape=jax.ShapeDtypeStruct((m, n), F32),
        scratch_shapes=_panel_scratch(tm, k),
        compiler_params=_cparams(2, 56),
        name="glu",
    )(a, w, w)


def _ffn_in_kernel(a_hbm, wg_ref, wu_ref, cw_ref, cb_ref, o_ref, a_ref, sem):
    _fetch_row_panel(a_hbm, a_ref, sem)
    a = a_ref[...]
    g = jnp.dot(a, wg_ref[...].astype(BF16), preferred_element_type=F32)
    u = jnp.dot(a, wu_ref[...].astype(BF16), preferred_element_type=F32)
    rows = g.shape[0]
    row = lax.broadcasted_iota(jnp.int32, (rows, 1), 0)
    g_prev = jnp.where(row == 0, 0.0, pltpu.roll(g, 1, 0))
    g_next = jnp.where(row == rows - 1, 0.0, pltpu.roll(g, rows - 1, 0))
    gc = cw_ref[0:1, :] * g_prev + cw_ref[1:2, :] * g + cw_ref[2:3, :] * g_next + cb_ref[...]
    o_ref[...] = (gc * jax.nn.sigmoid(gc) * u).astype(o_ref.dtype)


def _ffn_in(h, w_gate, w_up, conv_w, conv_b, seq, tn):
    m, k = h.shape
    f = w_gate.shape[1]
    return pl.pallas_call(
        _ffn_in_kernel,
        grid=(m // seq, f // tn),
        in_specs=[pl.BlockSpec(memory_space=pl.ANY),
                  pl.BlockSpec((k, tn), lambda i, j: (0, j)),
                  pl.BlockSpec((k, tn), lambda i, j: (0, j)),
                  pl.BlockSpec((3, tn), lambda i, j: (0, j)),
                  pl.BlockSpec((1, tn), lambda i, j: (0, j))],
        out_specs=pl.BlockSpec((seq, tn), lambda i, j: (i, j)),
        out_shape=jax.ShapeDtypeStruct((m, f), BF16),
        scratch_shapes=_panel_scratch(seq, k),
        compiler_params=_cparams(2, 56),
        name="ffn_in",
    )(h, w_gate, w_up, conv_w, conv_b.reshape(1, f))


def _t5_bucket(rel):
    half = NUM_BUCKETS // 2
    max_exact = half // 2
    base = jnp.where(rel > 0, half, 0)
    n = jnp.abs(rel)
    nf = jnp.maximum(n, 1).astype(F32)
    large = max_exact + (jnp.log(nf / max_exact) / math.log(ATTN_BLOCK / max_exact)
                         * (half - max_exact)).astype(jnp.int32)
    large = jnp.minimum(large, half - 1)
    return base + jnp.where(n < max_exact, n, large)


def _bias_kernel(bucket_ref, inwin_ref, rbt_ref, o_ref):
    nb = rbt_ref.shape[1]
    lanes = bucket_ref.shape[1]
    onehot = (lax.broadcasted_iota(jnp.int32, (nb, lanes), 0) == bucket_ref[...]).astype(F32)
    bias = jnp.dot(rbt_ref[...], onehot, preferred_element_type=F32,
                   precision=lax.Precision.HIGHEST)
    o_ref[...] = jnp.where(inwin_ref[...] > 0, bias, NEG_INF)


def _attn_bias(rel_bias):
    nb, heads = rel_bias.shape
    blk = ATTN_BLOCK
    q_idx = jnp.arange(blk)[:, None]
    k_idx = jnp.arange(3 * blk)[None, :]
    rel = k_idx - blk - q_idx
    bucket = _t5_bucket(rel).reshape(1, 3 * blk * blk).astype(jnp.int32)
    inwin = (jnp.abs(rel) <= blk).astype(jnp.int32).reshape(1, 3 * blk * blk)
    tl = 4096
    out = pl.pallas_call(
        _bias_kernel,
        grid=(3 * blk * blk // tl,),
        in_specs=[pl.BlockSpec((1, tl), lambda i: (0, i)),
                  pl.BlockSpec((1, tl), lambda i: (0, i)),
                  pl.BlockSpec((heads, nb), lambda i: (0, 0))],
        out_specs=pl.BlockSpec((heads, tl), lambda i: (0, i)),
        out_shape=jax.ShapeDtypeStruct((heads, 3 * blk * blk), F32),
        compiler_params=_cparams(1, 32),
        name="attn_bias",
    )(bucket, inwin, rel_bias.T)
    return out.reshape(heads, blk, 3 * blk)


def _attn_kernel(sink_ref, q_ref, kp_ref, ko_ref, kn_ref, vp_ref, vo_ref, vn_ref, bias_ref,
                 o_ref, *, nblk, kvh, grp):
    blk, hd = ATTN_BLOCK, HEAD_DIM
    n = pl.program_id(0) % nblk
    col = lax.broadcasted_iota(jnp.int32, (1, 3 * blk), 1)
    key_pos = (n - 1) * blk + col
    edge = jnp.where((key_pos >= 0) & (key_pos < nblk * blk), 0.0, NEG_INF)
    scale = hd ** -0.5
    for kh in range(kvh):
        ks = slice(kh * hd, (kh + 1) * hd)
        k = jnp.concatenate([kp_ref[:, ks], ko_ref[:, ks], kn_ref[:, ks]], axis=0)
        v = jnp.concatenate([vp_ref[:, ks], vo_ref[:, ks], vn_ref[:, ks]], axis=0)
        heads = [kh * grp + g for g in range(grp)]
        q = jnp.concatenate([q_ref[:, h * hd:(h + 1) * hd] for h in heads], axis=0)
        s = lax.dot_general(q, k, (((1,), (1,)), ((), ())), preferred_element_type=F32) * scale
        s = s + bias_ref[kh * grp:(kh + 1) * grp].reshape(grp * blk, 3 * blk) + edge
        sink = jnp.concatenate([jnp.full((blk, 1), sink_ref[h], F32) for h in heads], axis=0)
        mx = jnp.maximum(jnp.max(s, axis=-1, keepdims=True), sink)
        p = jnp.exp(s - mx)
        denom = jnp.sum(p, axis=-1, keepdims=True) + jnp.exp(sink - mx)
        o = jnp.dot(p.astype(BF16), v, preferred_element_type=F32) / denom
        for g, h in enumerate(heads):
            o_ref[:, h * hd:(h + 1) * hd] = o[g * blk:(g + 1) * blk].astype(o_ref.dtype)


def _attention(qkv, bias, sink, seq, heads, kvh):
    m = qkv.shape[0]
    blk, hd = ATTN_BLOCK, HEAD_DIM
    nblk = seq // blk
    grp = heads // kvh
    qw, kw = heads * hd, kvh * hd
    kcol, vcol = qw // kw, qw // kw + 1

    def prev(i):
        return jnp.where(i % nblk == 0, i, i - 1)

    def nxt(i):
        return jnp.where(i % nblk == nblk - 1, i, i + 1)

    kv = lambda rowf, colb: pl.BlockSpec((blk, kw), lambda i: (rowf(i), colb))
    same = lambda i: i
    return pl.pallas_call(
        functools.partial(_attn_kernel, nblk=nblk, kvh=kvh, grp=grp),
        grid=(m // blk,),
        in_specs=[pl.BlockSpec(memory_space=pltpu.SMEM),
                  pl.BlockSpec((blk, qw), lambda i: (i, 0)),
                  kv(prev, kcol), kv(same, kcol), kv(nxt, kcol),
                  kv(prev, vcol), kv(same, vcol), kv(nxt, vcol),
                  pl.BlockSpec((heads, blk, 3 * blk), lambda i: (0, 0, 0))],
        out_specs=pl.BlockSpec((blk, qw), lambda i: (i, 0)),
        out_shape=jax.ShapeDtypeStruct((m, qw), BF16),
        compiler_params=_cparams(1, 48),
        name="attention",
    )(sink, qkv, qkv, qkv, qkv, qkv, qkv, qkv, bias)


def _cis_pow(zr, zi, e):
    mag = jnp.exp(zr * e)
    return mag * jnp.cos(zi * e), mag * jnp.sin(zi * e)


def _ssm_prep_group(g, lre_r, lim_r, ldt_r, lre_c, lim_c, ldt_c, btr, bti, ctr, cti, dcol):
    t, hch, p = SSM_CHUNK, SSM_GROUP_CH, SSM_STATE
    th, p2 = t * hch, 2 * p
    hi_prec = lax.Precision.HIGHEST
    ar, ai = lre_r[g], lim_r[g]
    dt = jnp.exp(ldt_r[g])
    zr, zi = ar * dt, ai * dt
    lbr, lbi = _cis_pow(zr, zi, 1.0)
    nr = lbr - 1.0
    den = ar * ar + ai * ai
    cr = (nr * ar + lbi * ai) / den
    ci = (lbi * ar - nr * ai) / den
    b_r, b_i = btr[g], bti[g]
    bbr = cr * b_r - ci * b_i
    bbi = cr * b_i + ci * b_r
    row = lax.broadcasted_iota(jnp.int32, (th, p2), 0)
    lane = lax.broadcasted_iota(jnp.int32, (th, p2), 1)
    j = row // hch
    e_w = jnp.where(lane < p, t - 1 - j, j).astype(F32)
    pr, pi = _cis_pow(zr, zi, e_w)
    bt_r = jnp.concatenate([bbr] * t, axis=0)
    bt_i = jnp.concatenate([bbi] * t, axis=0)
    w_mat = jnp.concatenate([pr * bt_r - pi * bt_i, pr * bt_i + pi * bt_r], axis=1)
    l_r, l_i = _cis_pow(zr, zi, float(t))
    lam = jnp.concatenate([l_r, l_i], axis=1)
    arc, aic = lre_c[g], lim_c[g]
    dtc = jnp.exp(ldt_c[g])
    zrc, zic = arc * dtc, aic * dtc
    rowc = lax.broadcasted_iota(jnp.int32, (p2, th), 0)
    lanec = lax.broadcasted_iota(jnp.int32, (p2, th), 1)
    nn = lanec // hch
    e_k = jnp.where(rowc < p, nn, t - 1 - nn).astype(F32)
    qr, qi = _cis_pow(zrc, zic, e_k)
    tile = (lax.broadcasted_iota(jnp.int32, (hch, th), 1) % hch
            == lax.broadcasted_iota(jnp.int32, (hch, th), 0)).astype(F32)
    c_r = jnp.dot(ctr[g], tile, preferred_element_type=F32, precision=hi_prec)
    c_i = jnp.dot(cti[g], tile, preferred_element_type=F32, precision=hi_prec)
    e_r = c_r * qr - c_i * qi
    e_i = c_r * qi + c_i * qr
    lbrc, lbic = _cis_pow(zrc, zic, 1.0)
    v_mat = jnp.concatenate([e_r * lbrc - e_i * lbic, -(e_r * lbic + e_i * lbrc)], axis=0)
    fwd_lane = lax.broadcasted_iota(jnp.int32, (hch, p2), 1) < p
    rhs = jnp.concatenate([e_r, e_i], axis=0)
    lhs0 = jnp.concatenate([jnp.where(fwd_lane, bbr, 0.0), -jnp.where(fwd_lane, bbi, 0.0)], axis=1)
    lhs1 = jnp.concatenate([jnp.where(fwd_lane, 0.0, bbr), -jnp.where(fwd_lane, 0.0, bbi)], axis=1)
    k0 = jnp.dot(lhs0, rhs, preferred_element_type=F32, precision=hi_prec)
    k1 = jnp.dot(lhs1, rhs, preferred_element_type=F32, precision=hi_prec)
    lane_m = lax.broadcasted_iota(jnp.int32, (hch, th), 1)
    row_m = lax.broadcasted_iota(jnp.int32, (hch, th), 0)
    d_g = dcol[g]
    blocks = []
    for jj in range(t):
        a = pltpu.roll(k0, hch * jj, 1) if jj else k0
        a = jnp.where(lane_m >= hch * jj, a, 0.0)
        sh = (hch * (jj + 1)) % th
        b = pltpu.roll(k1, sh, 1) if sh else k1
        b = jnp.where(lane_m < hch * (jj + 1), b, 0.0)
        dd = jnp.where(lane_m == hch * jj + row_m, d_g, 0.0)
        blocks.append(a + b + dd)
    m_mat = jnp.concatenate(blocks, axis=0)
    return m_mat, w_mat, v_mat, lam


def _ssm_prep_kernel(lre_r, lim_r, ldt_r, lre_c, lim_c, ldt_c, btr, bti, ctr, cti, dcol,
                     m_ref, w_ref, v_ref, lam_ref, *, pairs, nb):
    ins = (lre_r, lim_r, ldt_r, lre_c, lim_c, ldt_c, btr, bti, ctr, cti, dcol)
    first = lax.broadcasted_iota(jnp.int32, (2 * nb, 4 * SSM_STATE), 0) < nb

    def body(q, carry):
        lams = []
        for s in range(2):
            g = 2 * q + s
            m_mat, w_mat, v_mat, lam = _ssm_prep_group(g, *ins)
            m_ref[g] = m_mat.astype(m_ref.dtype)
            w_ref[g] = w_mat.astype(w_ref.dtype)
            v_ref[g] = v_mat.astype(v_ref.dtype)
            lams.append(jnp.broadcast_to(lam, (2 * nb, 4 * SSM_STATE)))
        lam_ref[q] = jnp.where(first, lams[0], lams[1])
        return carry

    lax.fori_loop(0, pairs, body, 0)


def _ssm_prep(lre, lim, ldt, bre, bim, cre, cim, d, nb):
    _, g, p = lre.shape
    hch, t = SSM_GROUP_CH, SSM_CHUNK
    th = t * hch
    cat = lambda a: jnp.concatenate([a[0], a[1]], axis=-1)
    lre2, lim2 = cat(lre), cat(lim)
    ldt2 = jnp.repeat(ldt.T, p, axis=1)
    bt = lambda a: jnp.transpose(a, (1, 3, 0, 2)).reshape(g, hch, 2 * p)
    ct = lambda a: jnp.transpose(a, (1, 0, 3, 2)).reshape(g, 2 * p, hch)
    gp = _pick(g, (8, 4, 2))
    rowv = pl.BlockSpec((gp, 1, 2 * p), lambda i: (i, 0, 0))
    colv = pl.BlockSpec((gp, 2 * p, 1), lambda i: (i, 0, 0))
    mat = pl.BlockSpec((gp, th, th), lambda i: (i, 0, 0))
    return pl.pallas_call(
        functools.partial(_ssm_prep_kernel, pairs=gp // 2, nb=nb),
        grid=(g // gp,),
        in_specs=[rowv, rowv, rowv, colv, colv, colv,
                  pl.BlockSpec((gp, hch, 2 * p), lambda i: (i, 0, 0)),
                  pl.BlockSpec((gp, hch, 2 * p), lambda i: (i, 0, 0)),
                  pl.BlockSpec((gp, 2 * p, hch), lambda i: (i, 0, 0)),
                  pl.BlockSpec((gp, 2 * p, hch), lambda i: (i, 0, 0)),
                  pl.BlockSpec((gp, hch, 1), lambda i: (i, 0, 0))],
        out_specs=[mat, mat, mat,
                   pl.BlockSpec((gp // 2, 2 * nb, 4 * p), lambda i: (i, 0, 0))],
        out_shape=[jax.ShapeDtypeStruct((g, th, th), BF16)] * 3
        + [jax.ShapeDtypeStruct((g // 2, 2 * nb, 4 * p), F32)],
        compiler_params=_cparams(1, 32),
        name="ssm_prep",
    )(lre2.reshape(g, 1, 2 * p), lim2.reshape(g, 1, 2 * p), ldt2.reshape(g, 1, 2 * p),
      lre2.reshape(g, 2 * p, 1), lim2.reshape(g, 2 * p, 1), ldt2.reshape(g, 2 * p, 1),
      bt(bre), bt(bim), ct(cre), ct(cim), d.reshape(g, hch, 1))


def _ssm_kernel(u_ref, m_ref, w_ref, v_ref, lam_ref, y_ref, yin_s, s_s, xf_s, xb_s,
                *, pb, nchunk, nb):
    slab = 2 * nb
    rows = nchunk * slab
    p2 = 2 * SSM_STATE
    first = (lax.broadcasted_iota(jnp.int32, (rows, 1), 0) // nb) % 2 == 0

    def pair_dot(lhs, mats, q):
        ya = jnp.dot(lhs, mats[2 * q], preferred_element_type=F32)
        yb = jnp.dot(lhs, mats[2 * q + 1], preferred_element_type=F32)
        return jnp.where(first, ya, yb)

    for q in range(pb):
        u = u_ref[q]
        yin_s[q] = pair_dot(u, m_ref, q)
        s_s[q] = pair_dot(u, w_ref, q)

    fwd = lax.broadcasted_iota(jnp.int32, (slab, p2), 1) < SSM_STATE
    lam_r = [lam_ref[q, :, 0:p2] for q in range(pb)]
    lam_i = [lam_ref[q, :, p2:2 * p2] for q in range(pb)]

    def step(k, carry):
        kf = pl.multiple_of(k * slab, slab)
        kb = pl.multiple_of((nchunk - 1 - k) * slab, slab)
        out = []
        for q in range(pb):
            xr, xi = carry[2 * q], carry[2 * q + 1]
            xf_s[q, pl.ds(kf, slab), 0:p2] = xr
            xf_s[q, pl.ds(kf, slab), p2:2 * p2] = xi
            xb_s[q, pl.ds(kb, slab), 0:p2] = xr
            xb_s[q, pl.ds(kb, slab), p2:2 * p2] = xi
            sr = jnp.where(fwd, s_s[q, pl.ds(kf, slab), 0:p2], s_s[q, pl.ds(kb, slab), 0:p2])
            si = jnp.where(fwd, s_s[q, pl.ds(kf, slab), p2:2 * p2],
                           s_s[q, pl.ds(kb, slab), p2:2 * p2])
            out.append(lam_r[q] * xr - lam_i[q] * xi + sr)
            out.append(lam_r[q] * xi + lam_i[q] * xr + si)
        return tuple(out)

    zero = jnp.zeros((slab, p2), F32)
    lax.fori_loop(0, nchunk, step, (zero,) * (2 * pb))

    fwd_all = lax.broadcasted_iota(jnp.int32, (rows, 2 * p2), 1) % p2 < SSM_STATE
    for q in range(pb):
        x = jnp.where(fwd_all, xf_s[q], xb_s[q]).astype(BF16)
        y = yin_s[q] + pair_dot(x, v_ref, q)
        y_ref[q] = jax.nn.gelu(y).astype(y_ref.dtype)


def _ssm_core(u2, m_mat, w_mat, v_mat, lam, nchunk, nb):
    npair, rows, th = u2.shape
    pb = _pick(npair, (4, 2, 1))
    p4 = 4 * SSM_STATE
    big = pl.BlockSpec((pb, rows, th), lambda i: (i, 0, 0))
    mat = pl.BlockSpec((2 * pb, th, th), lambda i: (i, 0, 0))
    return pl.pallas_call(
        functools.partial(_ssm_kernel, pb=pb, nchunk=nchunk, nb=nb),
        grid=(npair // pb,),
        in_specs=[big, mat, mat, mat, pl.BlockSpec((pb, 2 * nb, p4), lambda i: (i, 0, 0))],
        out_specs=big,
        out_shape=jax.ShapeDtypeStruct((npair, rows, th), BF16),
        scratch_shapes=[pltpu.VMEM((pb, rows, th), F32), pltpu.VMEM((pb, rows, p4), F32),
                        pltpu.VMEM((pb, rows, p4), F32), pltpu.VMEM((pb, rows, p4), F32)],
        compiler_params=_cparams(1, 48),
        name="ssm_core",
    )(u2, m_mat, w_mat, v_mat, lam)


def _s5_mixer(h, w_in, lre, lim, ldt, bre, bim, cre, cim, d, w_glu, batch, seq):
    m, dm = h.shape
    hch, t = SSM_GROUP_CH, SSM_CHUNK
    g = dm // hch
    nchunk = seq // t
    u = _mm(h, w_in, F32, tm=_pick(m, (2048, 1024, 512, 256, 128)), tn=_pick(dm, (256, 128)))
    y = _s5_states_mix(u.reshape(batch, seq, dm), lre, lim, ldt, bre, bim, cre, cim, d)
    yg = _gelu_cast(y.reshape(m, dm))
    return _glu(yg, w_glu, tm=_pick(m, (2048, 1024, 512, 256, 128)), tn=_pick(dm, (256, 128)))


def _gelu_kernel(y_ref, o_ref):
    o_ref[...] = jax.nn.gelu(y_ref[...]).astype(o_ref.dtype)


def _gelu_cast(y):
    m, d = y.shape
    tm = _pick(m, (256, 128, 64, 32, 16, 8))
    row = pl.BlockSpec((tm, d), lambda i: (i, 0))
    return pl.pallas_call(
        _gelu_kernel, grid=(m // tm,), in_specs=[row], out_specs=row,
        out_shape=jax.ShapeDtypeStruct((m, d), BF16),
        compiler_params=_cparams(1, 32), name="gelu_cast",
    )(y)


def _recurrence_combine(left, right):
    a_l, b_l = left
    a_r, b_r = right
    return a_r * a_l, a_r * b_l + b_r


def _s5_states_mix(u, lam_re, lam_im, log_dt, b_re, b_im, c_re, c_im, d_skip):
    bsz, seq, dm = u.shape
    g, p = lam_re.shape[1], lam_re.shape[2]
    ug = u.reshape(bsz, seq, g, dm // g)
    y = u * d_skip
    for dirn, reverse in ((0, False), (1, True)):
        lam = lax.complex(lam_re[dirn], lam_im[dirn])
        dt = jnp.exp(log_dt[dirn])[:, None]
        lam_bar = jnp.exp(lam * dt)
        b_bar = ((lam_bar - 1.0) / lam)[..., None] * lax.complex(b_re[dirn], b_im[dirn])
        c_mat = lax.complex(c_re[dirn], c_im[dirn])
        bu = jnp.einsum('blgh,gph->blgp', ug, b_bar)
        a = jnp.broadcast_to(lam_bar, (1, seq, g, p))
        _, states = lax.associative_scan(_recurrence_combine, (a, bu), axis=1, reverse=reverse)
        y = y + jnp.real(jnp.einsum('blgp,ghp->blgh', states, c_mat)).reshape(bsz, seq, dm)
    return y


def kernel(x, rel_bias, pre_mix_norm, post_mix_norm, pre_ffn_norm, post_ffn_norm, attn_wqkv, attn_sink, attn_wo, ssm_w_in, ssm_lambda_re, ssm_lambda_im, ssm_log_dt, ssm_b_re, ssm_b_im, ssm_c_re, ssm_c_im, ssm_d, ssm_w_glu, ffn_w_gate, ffn_w_up, ffn_conv_w, ffn_conv_b, ffn_w_down):
    batch, seq, dm = x.shape
    depth = pre_mix_norm.shape[0]
    m = batch * seq
    heads = dm // HEAD_DIM
    kvh = (attn_wqkv.shape[2] // HEAD_DIM - heads) // 2
    dff = ffn_w_gate.shape[2]
    assert seq % ATTN_BLOCK == 0 and seq % SSM_CHUNK == 0 and (2 * batch) % 8 == 0
    tm_big = _pick(m, (2048, 1024, 512, 256, 128))
    tm_mid = _pick(m, (1024, 512, 256, 128))

    xf = x.reshape(m, dm)
    h = _norm_cast(xf, pre_mix_norm[0])
    bias = _attn_bias(rel_bias)
    for i in range(depth):
        j = i // 2
        if i % 2 == 0:
            qkv = _mm(h, attn_wqkv[j], BF16, tm=tm_big, tn=_pick(attn_wqkv.shape[2], (512, 256, 128)))
            o = _attention(qkv, bias, attn_sink[j], seq, heads, kvh)
            mix = _mm(o, attn_wo[j], F32, tm=tm_big, tn=_pick(dm, (512, 256, 128)))
        else:
            mix = _s5_mixer(h, ssm_w_in[j], ssm_lambda_re[j], ssm_lambda_im[j], ssm_log_dt[j],
                            ssm_b_re[j], ssm_b_im[j], ssm_c_re[j], ssm_c_im[j], ssm_d[j],
                            ssm_w_glu[j], batch, seq)
        xf, h = _resid_norm(xf, mix, post_mix_norm[i], pre_ffn_norm[i])
        hid = _ffn_in(h, ffn_w_gate[i], ffn_w_up[i], ffn_conv_w[i], ffn_conv_b[i], seq,
                      tn=_pick(dff, (256, 128)))
        kc = dff // 2 if (dff // 2) % 128 == 0 else dff
        f = _mm(hid, ffn_w_down[i], F32, tm=tm_mid, tn=_pick(dm, (256, 128)), kc=kc, vmem_mib=60)
        g_next = pre_mix_norm[i + 1] if i + 1 < depth else None
        xf, h = _resid_norm(xf, f, post_ffn_norm[i], g_next)
    return xf.reshape(batch, seq, dm)
```

```python
import functools
import math

import jax
import jax.numpy as jnp
from jax import lax
from jax.experimental import pallas as pl
from jax.experimental.pallas import tpu as pltpu

F32 = jnp.float32
BF16 = jnp.bfloat16

HEAD_DIM = 128
ATTN_BLOCK = 128
NUM_BUCKETS = 32
SSM_GROUP_CH = 16
SSM_STATE = 64
SSM_CHUNK = 16
RMS_EPS = 1e-6
NEG_INF = -1e30

MIB = 1024 * 1024


def _cparams(n_grid_dims, vmem_mib):
    return pltpu.CompilerParams(
        dimension_semantics=("arbitrary",) * n_grid_dims,
        vmem_limit_bytes=vmem_mib * MIB,
    )


def _pick(n, prefs):
    for p in prefs:
        if p <= n and n % p == 0:
            return p
    return n


def _rms(x, g):
    return x * lax.rsqrt(jnp.mean(x * x, axis=-1, keepdims=True) + RMS_EPS) * g


def _norm_kernel(x_ref, g_ref, h_ref):
    h_ref[...] = _rms(x_ref[...], g_ref[...]).astype(h_ref.dtype)


def _norm_cast(x, g):
    m, d = x.shape
    tm = _pick(m, (256, 128, 64, 32, 16, 8))
    return pl.pallas_call(
        _norm_kernel,
        grid=(m // tm,),
        in_specs=[pl.BlockSpec((tm, d), lambda i: (i, 0)),
                  pl.BlockSpec((1, d), lambda i: (0, 0))],
        out_specs=pl.BlockSpec((tm, d), lambda i: (i, 0)),
        out_shape=jax.ShapeDtypeStruct((m, d), BF16),
        compiler_params=_cparams(1, 32),
        name="norm_cast",
    )(x, g.reshape(1, d))


def _resid_norm_kernel(x_ref, m_ref, g1_ref, g2_ref, xo_ref, ho_ref):
    xn = x_ref[...] + _rms(m_ref[...], g1_ref[...])
    xo_ref[...] = xn
    ho_ref[...] = _rms(xn, g2_ref[...]).astype(ho_ref.dtype)


def _resid_kernel(x_ref, m_ref, g1_ref, xo_ref):
    xo_ref[...] = x_ref[...] + _rms(m_ref[...], g1_ref[...])


def _resid_norm(x, mix, g_post, g_next):
    m, d = x.shape
    tm = _pick(m, (128, 64, 32, 16, 8))
    row = pl.BlockSpec((tm, d), lambda i: (i, 0))
    vec = pl.BlockSpec((1, d), lambda i: (0, 0))
    if g_next is None:
        return pl.pallas_call(
            _resid_kernel, grid=(m // tm,),
            in_specs=[row, row, vec], out_specs=row,
            out_shape=jax.ShapeDtypeStruct((m, d), F32),
            compiler_params=_cparams(1, 32), name="resid",
        )(x, mix, g_post.reshape(1, d)), None
    return pl.pallas_call(
        _resid_norm_kernel, grid=(m // tm,),
        in_specs=[row, row, vec, vec], out_specs=[row, row],
        out_shape=[jax.ShapeDtypeStruct((m, d), F32), jax.ShapeDtypeStruct((m, d), BF16)],
        compiler_params=_cparams(1, 32), name="resid_norm",
    )(x, mix, g_post.reshape(1, d), g_next.reshape(1, d))


def _fetch_row_panel(a_hbm, a_ref, sem):
    @pl.when(pl.program_id(1) == 0)
    def _():
        tm = a_ref.shape[0]
        cp = pltpu.make_async_copy(
            a_hbm.at[pl.ds(pl.multiple_of(pl.program_id(0) * tm, tm), tm), :], a_ref, sem)
        cp.start()
        cp.wait()


def _panel_scratch(tm, k):
    return [pltpu.VMEM((tm, k), BF16), pltpu.SemaphoreType.DMA(())]


def _mm_kernel(a_hbm, w_ref, o_ref, a_ref, sem, *, kc):
    _fetch_row_panel(a_hbm, a_ref, sem)
    k = a_ref.shape[1]
    acc = None
    for k0 in range(0, k, kc):
        part = jnp.dot(a_ref[:, k0:k0 + kc], w_ref[k0:k0 + kc, :].astype(BF16),
                       preferred_element_type=F32)
        acc = part if acc is None else acc + part
    o_ref[...] = acc.astype(o_ref.dtype)


def _mm(a, w, out_dtype, tm, tn, kc=None, vmem_mib=56):
    m, k = a.shape
    n = w.shape[1]
    kc = k if kc is None else kc
    return pl.pallas_call(
        functools.partial(_mm_kernel, kc=kc),
        grid=(m // tm, n // tn),
        in_specs=[pl.BlockSpec(memory_space=pl.ANY),
                  pl.BlockSpec((k, tn), lambda i, j: (0, j))],
        out_specs=pl.BlockSpec((tm, tn), lambda i, j: (i, j)),
        out_shape=jax.ShapeDtypeStruct((m, n), out_dtype),
        scratch_shapes=_panel_scratch(tm, k),
        compiler_params=_cparams(2, vmem_mib),
        name="mm",
    )(a, w)


def _glu_kernel(a_hbm, wa_ref, wb_ref, o_ref, a_ref, sem):
    _fetch_row_panel(a_hbm, a_ref, sem)
    a = a_ref[...]
    ya = jnp.dot(a, wa_ref[...].astype(BF16), preferred_element_type=F32)
    yb = jnp.dot(a, wb_ref[...].astype(BF16), preferred_element_type=F32)
    o_ref[...] = (ya * jax.nn.sigmoid(yb)).astype(o_ref.dtype)


def _glu(a, w, tm, tn):
    m, k = a.shape
    n = w.shape[1] // 2
    nj = n // tn
    return pl.pallas_call(
        _glu_kernel,
        grid=(m // tm, nj),
        in_specs=[pl.BlockSpec(memory_space=pl.ANY),
                  pl.BlockSpec((k, tn), lambda i, j: (0, j)),
                  pl.BlockSpec((k, tn), lambda i, j: (0, j + nj))],
        out_specs=pl.BlockSpec((tm, tn), lambda i, j: (i, j)),
        out_shape=jax.ShapeDtypeStruct((m, n), F32),
        scratch_shapes=_panel_scratch(tm, k),
        compiler_params=_cparams(2, 56),
        name="glu",
    )(a, w, w)


def _ffn_in_kernel(a_hbm, wg_ref, wu_ref, cw_ref, cb_ref, o_ref, a_ref, sem):
    _fetch_row_panel(a_hbm, a_ref, sem)
    a = a_ref[...]
    g = jnp.dot(a, wg_ref[...].astype(BF16), preferred_element_type=F32)
    u = jnp.dot(a, wu_ref[...].astype(BF16), preferred_element_type=F32)
    rows = g.shape[0]
    row = lax.broadcasted_iota(jnp.int32, (rows, 1), 0)
    g_prev = jnp.where(row == 0, 0.0, pltpu.roll(g, 1, 0))
    g_next = jnp.where(row == rows - 1, 0.0, pltpu.roll(g, rows - 1, 0))
    gc = cw_ref[0:1, :] * g_prev + cw_ref[1:2, :] * g + cw_ref[2:3, :] * g_next + cb_ref[...]
    o_ref[...] = (gc * jax.nn.sigmoid(gc) * u).astype(o_ref.dtype)


def _ffn_in(h, w_gate, w_up, conv_w, conv_b, seq, tn):
    m, k = h.shape
    f = w_gate.shape[1]
    return pl.pallas_call(
        _ffn_in_kernel,
        grid=(m // seq, f // tn),
        in_specs=[pl.BlockSpec(memory_space=pl.ANY),
                  pl.BlockSpec((k, tn), lambda i, j: (0, j)),
                  pl.BlockSpec((k, tn), lambda i, j: (0, j)),
                  pl.BlockSpec((3, tn), lambda i, j: (0, j)),
                  pl.BlockSpec((1, tn), lambda i, j: (0, j))],
        out_specs=pl.BlockSpec((seq, tn), lambda i, j: (i, j)),
        out_shape=jax.ShapeDtypeStruct((m, f), BF16),
        scratch_shapes=_panel_scratch(seq, k),
        compiler_params=_cparams(2, 56),
        name="ffn_in",
    )(h, w_gate, w_up, conv_w, conv_b.reshape(1, f))


def _t5_bucket(rel):
    half = NUM_BUCKETS // 2
    max_exact = half // 2
    base = jnp.where(rel > 0, half, 0)
    n = jnp.abs(rel)
    nf = jnp.maximum(n, 1).astype(F32)
    large = max_exact + (jnp.log(nf / max_exact) / math.log(ATTN_BLOCK / max_exact)
                         * (half - max_exact)).astype(jnp.int32)
    large = jnp.minimum(large, half - 1)
    return base + jnp.where(n < max_exact, n, large)


def _bias_kernel(bucket_ref, inwin_ref, rbt_ref, o_ref):
    nb = rbt_ref.shape[1]
    lanes = bucket_ref.shape[1]
    onehot = (lax.broadcasted_iota(jnp.int32, (nb, lanes), 0) == bucket_ref[...]).astype(F32)
    bias = jnp.dot(rbt_ref[...], onehot, preferred_element_type=F32,
                   precision=lax.Precision.HIGHEST)
    o_ref[...] = jnp.where(inwin_ref[...] > 0, bias, NEG_INF)


def _attn_bias(rel_bias):
    nb, heads = rel_bias.shape
    blk = ATTN_BLOCK
    q_idx = jnp.arange(blk)[:, None]
    k_idx = jnp.arange(3 * blk)[None, :]
    rel = k_idx - blk - q_idx
    bucket = _t5_bucket(rel).reshape(1, 3 * blk * blk).astype(jnp.int32)
    inwin = (jnp.abs(rel) <= blk).astype(jnp.int32).reshape(1, 3 * blk * blk)
    tl = 4096
    out = pl.pallas_call(
        _bias_kernel,
        grid=(3 * blk * blk // tl,),
        in_specs=[pl.BlockSpec((1, tl), lambda i: (0, i)),
                  pl.BlockSpec((1, tl), lambda i: (0, i)),
                  pl.BlockSpec((heads, nb), lambda i: (0, 0))],
        out_specs=pl.BlockSpec((heads, tl), lambda i: (0, i)),
        out_shape=jax.ShapeDtypeStruct((heads, 3 * blk * blk), F32),
        compiler_params=_cparams(1, 32),
        name="attn_bias",
    )(bucket, inwin, rel_bias.T)
    return out.reshape(heads, blk, 3 * blk)


def _attn_kernel(sink_ref, q_ref, kp_ref, ko_ref, kn_ref, vp_ref, vo_ref, vn_ref, bias_ref,
                 o_ref, *, nblk, kvh, grp):
    blk, hd = ATTN_BLOCK, HEAD_DIM
    n = pl.program_id(0) % nblk
    col = lax.broadcasted_iota(jnp.int32, (1, 3 * blk), 1)
    key_pos = (n - 1) * blk + col
    edge = jnp.where((key_pos >= 0) & (key_pos < nblk * blk), 0.0, NEG_INF)
    scale = hd ** -0.5
    for kh in range(kvh):
        ks = slice(kh * hd, (kh + 1) * hd)
        k = jnp.concatenate([kp_ref[:, ks], ko_ref[:, ks], kn_ref[:, ks]], axis=0)
        v = jnp.concatenate([vp_ref[:, ks], vo_ref[:, ks], vn_ref[:, ks]], axis=0)
        heads = [kh * grp + g for g in range(grp)]
        q = jnp.concatenate([q_ref[:, h * hd:(h + 1) * hd] for h in heads], axis=0)
        s = lax.dot_general(q, k, (((1,), (1,)), ((), ())), preferred_element_type=F32) * scale
        s = s + bias_ref[kh * grp:(kh + 1) * grp].reshape(grp * blk, 3 * blk) + edge
        sink = jnp.concatenate([jnp.full((blk, 1), sink_ref[h], F32) for h in heads], axis=0)
        mx = jnp.maximum(jnp.max(s, axis=-1, keepdims=True), sink)
        p = jnp.exp(s - mx)
        denom = jnp.sum(p, axis=-1, keepdims=True) + jnp.exp(sink - mx)
        o = jnp.dot(p.astype(BF16), v, preferred_element_type=F32) / denom
        for g, h in enumerate(heads):
            o_ref[:, h * hd:(h + 1) * hd] = o[g * blk:(g + 1) * blk].astype(o_ref.dtype)


def _attention(qkv, bias, sink, seq, heads, kvh):
    m = qkv.shape[0]
    blk, hd = ATTN_BLOCK, HEAD_DIM
    nblk = seq // blk
    grp = heads // kvh
    qw, kw = heads * hd, kvh * hd
    kcol, vcol = qw // kw, qw // kw + 1

    def prev(i):
        return jnp.where(i % nblk == 0, i, i - 1)

    def nxt(i):
        return jnp.where(i % nblk == nblk - 1, i, i + 1)

    kv = lambda rowf, colb: pl.BlockSpec((blk, kw), lambda i: (rowf(i), colb))
    same = lambda i: i
    return pl.pallas_call(
        functools.partial(_attn_kernel, nblk=nblk, kvh=kvh, grp=grp),
        grid=(m // blk,),
        in_specs=[pl.BlockSpec(memory_space=pltpu.SMEM),
                  pl.BlockSpec((blk, qw), lambda i: (i, 0)),
                  kv(prev, kcol), kv(same, kcol), kv(nxt, kcol),
                  kv(prev, vcol), kv(same, vcol), kv(nxt, vcol),
                  pl.BlockSpec((heads, blk, 3 * blk), lambda i: (0, 0, 0))],
        out_specs=pl.BlockSpec((blk, qw), lambda i: (i, 0)),
        out_shape=jax.ShapeDtypeStruct((m, qw), BF16),
        compiler_params=_cparams(1, 48),
        name="attention",
    )(sink, qkv, qkv, qkv, qkv, qkv, qkv, qkv, bias)


def _cis_pow(zr, zi, e):
    mag = jnp.exp(zr * e)
    return mag * jnp.cos(zi * e), mag * jnp.sin(zi * e)


def _ssm_prep_group(g, lre_r, lim_r, ldt_r, lre_c, lim_c, ldt_c, btr, bti, ctr, cti, dcol):
    t, hch, p = SSM_CHUNK, SSM_GROUP_CH, SSM_STATE
    th, p2 = t * hch, 2 * p
    hi_prec = lax.Precision.HIGHEST
    ar, ai = lre_r[g], lim_r[g]
    dt = jnp.exp(ldt_r[g])
    zr, zi = ar * dt, ai * dt
    lbr, lbi = _cis_pow(zr, zi, 1.0)
    nr = lbr - 1.0
    den = ar * ar + ai * ai
    cr = (nr * ar + lbi * ai) / den
    ci = (lbi * ar - nr * ai) / den
    b_r, b_i = btr[g], bti[g]
    bbr = cr * b_r - ci * b_i
    bbi = cr * b_i + ci * b_r
    row = lax.broadcasted_iota(jnp.int32, (th, p2), 0)
    lane = lax.broadcasted_iota(jnp.int32, (th, p2), 1)
    j = row // hch
    e_w = jnp.where(lane < p, t - 1 - j, j).astype(F32)
    pr, pi = _cis_pow(zr, zi, e_w)
    bt_r = jnp.concatenate([bbr] * t, axis=0)
    bt_i = jnp.concatenate([bbi] * t, axis=0)
    w_mat = jnp.concatenate([pr * bt_r - pi * bt_i, pr * bt_i + pi * bt_r], axis=1)
    l_r, l_i = _cis_pow(zr, zi, float(t))
    lam = jnp.concatenate([l_r, l_i], axis=1)
    arc, aic = lre_c[g], lim_c[g]
    dtc = jnp.exp(ldt_c[g])
    zrc, zic = arc * dtc, aic * dtc
    rowc = lax.broadcasted_iota(jnp.int32, (p2, th), 0)
    lanec = lax.broadcasted_iota(jnp.int32, (p2, th), 1)
    nn = lanec // hch
    e_k = jnp.where(rowc < p, nn, t - 1 - nn).astype(F32)
    qr, qi = _cis_pow(zrc, zic, e_k)
    tile = (lax.broadcasted_iota(jnp.int32, (hch, th), 1) % hch
            == lax.broadcasted_iota(jnp.int32, (hch, th), 0)).astype(F32)
    c_r = jnp.dot(ctr[g], tile, preferred_element_type=F32, precision=hi_prec)
    c_i = jnp.dot(cti[g], tile, preferred_element_type=F32, precision=hi_prec)
    e_r = c_r * qr - c_i * qi
    e_i = c_r * qi + c_i * qr
    lbrc, lbic = _cis_pow(zrc, zic, 1.0)
    v_mat = jnp.concatenate([e_r * lbrc - e_i * lbic, -(e_r * lbic + e_i * lbrc)], axis=0)
    fwd_lane = lax.broadcasted_iota(jnp.int32, (hch, p2), 1) < p
    rhs = jnp.concatenate([e_r, e_i], axis=0)
    lhs0 = jnp.concatenate([jnp.where(fwd_lane, bbr, 0.0), -jnp.where(fwd_lane, bbi, 0.0)], axis=1)
    lhs1 = jnp.concatenate([jnp.where(fwd_lane, 0.0, bbr), -jnp.where(fwd_lane, 0.0, bbi)], axis=1)
    k0 = jnp.dot(lhs0, rhs, preferred_element_type=F32, precision=hi_prec)
    k1 = jnp.dot(lhs1, rhs, preferred_element_type=F32, precision=hi_prec)
    lane_m = lax.broadcasted_iota(jnp.int32, (hch, th), 1)
    row_m = lax.broadcasted_iota(jnp.int32, (hch, th), 0)
    d_g = dcol[g]
    blocks = []
    for jj in range(t):
        a = pltpu.roll(k0, hch * jj, 1) if jj else k0
        a = jnp.where(lane_m >= hch * jj, a, 0.0)
        sh = (hch * (jj + 1)) % th
        b = pltpu.roll(k1, sh, 1) if sh else k1
        b = jnp.where(lane_m < hch * (jj + 1), b, 0.0)
        dd = jnp.where(lane_m == hch * jj + row_m, d_g, 0.0)
        blocks.append(a + b + dd)
    m_mat = jnp.concatenate(blocks, axis=0)
    return m_mat, w_mat, v_mat, lam


def _ssm_prep_kernel(lre_r, lim_r, ldt_r, lre_c, lim_c, ldt_c, btr, bti, ctr, cti, dcol,
                     m_ref, w_ref, v_ref, lam_ref, *, pairs, nb):
    ins = (lre_r, lim_r, ldt_r, lre_c, lim_c, ldt_c, btr, bti, ctr, cti, dcol)
    first = lax.broadcasted_iota(jnp.int32, (2 * nb, 4 * SSM_STATE), 0) < nb

    def body(q, carry):
        lams = []
        for s in range(2):
            g = 2 * q + s
            m_mat, w_mat, v_mat, lam = _ssm_prep_group(g, *ins)
            m_ref[g] = m_mat.astype(m_ref.dtype)
            w_ref[g] = w_mat.astype(w_ref.dtype)
            v_ref[g] = v_mat.astype(v_ref.dtype)
            lams.append(jnp.broadcast_to(lam, (2 * nb, 4 * SSM_STATE)))
        lam_ref[q] = jnp.where(first, lams[0], lams[1])
        return carry

    lax.fori_loop(0, pairs, body, 0)


def _ssm_prep(lre, lim, ldt, bre, bim, cre, cim, d, nb):
    _, g, p = lre.shape
    hch, t = SSM_GROUP_CH, SSM_CHUNK
    th = t * hch
    cat = lambda a: jnp.concatenate([a[0], a[1]], axis=-1)
    lre2, lim2 = cat(lre), cat(lim)
    ldt2 = jnp.repeat(ldt.T, p, axis=1)
    bt = lambda a: jnp.transpose(a, (1, 3, 0, 2)).reshape(g, hch, 2 * p)
    ct = lambda a: jnp.transpose(a, (1, 0, 3, 2)).reshape(g, 2 * p, hch)
    gp = _pick(g, (8, 4, 2))
    rowv = pl.BlockSpec((gp, 1, 2 * p), lambda i: (i, 0, 0))
    colv = pl.BlockSpec((gp, 2 * p, 1), lambda i: (i, 0, 0))
    mat = pl.BlockSpec((gp, th, th), lambda i: (i, 0, 0))
    return pl.pallas_call(
        functools.partial(_ssm_prep_kernel, pairs=gp // 2, nb=nb),
        grid=(g // gp,),
        in_specs=[rowv, rowv, rowv, colv, colv, colv,
                  pl.BlockSpec((gp, hch, 2 * p), lambda i: (i, 0, 0)),
                  pl.BlockSpec((gp, hch, 2 * p), lambda i: (i, 0, 0)),
                  pl.BlockSpec((gp, 2 * p, hch), lambda i: (i, 0, 0)),
                  pl.BlockSpec((gp, 2 * p, hch), lambda i: (i, 0, 0)),
                  pl.BlockSpec((gp, hch, 1), lambda i: (i, 0, 0))],
        out_specs=[mat, mat, mat,
                   pl.BlockSpec((gp // 2, 2 * nb, 4 * p), lambda i: (i, 0, 0))],
        out_shape=[jax.ShapeDtypeStruct((g, th, th), BF16)] * 3
        + [jax.ShapeDtypeStruct((g // 2, 2 * nb, 4 * p), F32)],
        compiler_params=_cparams(1, 32),
        name="ssm_prep",
    )(lre2.reshape(g, 1, 2 * p), lim2.reshape(g, 1, 2 * p), ldt2.reshape(g, 1, 2 * p),
      lre2.reshape(g, 2 * p, 1), lim2.reshape(g, 2 * p, 1), ldt2.reshape(g, 2 * p, 1),
      bt(bre), bt(bim), ct(cre), ct(cim), d.reshape(g, hch, 1))


def _ssm_kernel(u_ref, m_ref, w_ref, v_ref, lam_ref, yext_ref, y_ref, yin_s, s_s, xf_s, xb_s,
                *, pb, nchunk, nb):
    slab = 2 * nb
    rows = nchunk * slab
    p2 = 2 * SSM_STATE
    first = (lax.broadcasted_iota(jnp.int32, (rows, 1), 0) // nb) % 2 == 0

    def pair_dot(lhs, mats, q):
        ya = jnp.dot(lhs, mats[2 * q], preferred_element_type=F32)
        yb = jnp.dot(lhs, mats[2 * q + 1], preferred_element_type=F32)
        return jnp.where(first, ya, yb)

    for q in range(pb):
        u = u_ref[q]
        yin_s[q] = pair_dot(u, m_ref, q)
        s_s[q] = pair_dot(u, w_ref, q)

    fwd = lax.broadcasted_iota(jnp.int32, (slab, p2), 1) < SSM_STATE
    lam_r = [lam_ref[q, :, 0:p2] for q in range(pb)]
    lam_i = [lam_ref[q, :, p2:2 * p2] for q in range(pb)]

    def step(k, carry):
        kf = pl.multiple_of(k * slab, slab)
        kb = pl.multiple_of((nchunk - 1 - k) * slab, slab)
        out = []
        for q in range(pb):
            xr, xi = carry[2 * q], carry[2 * q + 1]
            xf_s[q, pl.ds(kf, slab), 0:p2] = xr
            xf_s[q, pl.ds(kf, slab), p2:2 * p2] = xi
            xb_s[q, pl.ds(kb, slab), 0:p2] = xr
            xb_s[q, pl.ds(kb, slab), p2:2 * p2] = xi
            sr = jnp.where(fwd, s_s[q, pl.ds(kf, slab), 0:p2], s_s[q, pl.ds(kb, slab), 0:p2])
            si = jnp.where(fwd, s_s[q, pl.ds(kf, slab), p2:2 * p2],
                           s_s[q, pl.ds(kb, slab), p2:2 * p2])
            out.append(lam_r[q] * xr - lam_i[q] * xi + sr)
            out.append(lam_r[q] * xi + lam_i[q] * xr + si)
        return tuple(out)

    zero = jnp.zeros((slab, p2), F32)
    lax.fori_loop(0, nchunk, step, (zero,) * (2 * pb))

    fwd_all = lax.broadcasted_iota(jnp.int32, (rows, 2 * p2), 1) % p2 < SSM_STATE
    for q in range(pb):
        x = jnp.where(fwd_all, xf_s[q], xb_s[q]).astype(BF16)
        y = yin_s[q] + pair_dot(x, v_ref, q) + yext_ref[q]
        y_ref[q] = jax.nn.gelu(y).astype(y_ref.dtype)


def _ssm_core(u2, m_mat, w_mat, v_mat, lam, yext, nchunk, nb):
    npair, rows, th = u2.shape
    pb = _pick(npair, (4, 2, 1))
    p4 = 4 * SSM_STATE
    big = pl.BlockSpec((pb, rows, th), lambda i: (i, 0, 0))
    mat = pl.BlockSpec((2 * pb, th, th), lambda i: (i, 0, 0))
    return pl.pallas_call(
        functools.partial(_ssm_kernel, pb=pb, nchunk=nchunk, nb=nb),
        grid=(npair // pb,),
        in_specs=[big, mat, mat, mat, pl.BlockSpec((pb, 2 * nb, p4), lambda i: (i, 0, 0)), big],
        out_specs=big,
        out_shape=jax.ShapeDtypeStruct((npair, rows, th), BF16),
        scratch_shapes=[pltpu.VMEM((pb, rows, th), F32), pltpu.VMEM((pb, rows, p4), F32),
                        pltpu.VMEM((pb, rows, p4), F32), pltpu.VMEM((pb, rows, p4), F32)],
        compiler_params=_cparams(1, 56),
        name="ssm_core",
    )(u2, m_mat, w_mat, v_mat, lam, yext)


def _s5_mixer(h, w_in, lre, lim, ldt, bre, bim, cre, cim, d, w_glu, batch, seq):
    m, dm = h.shape
    hch, t = SSM_GROUP_CH, SSM_CHUNK
    g = dm // hch
    nchunk = seq // t
    u = _mm(h, w_in, F32, tm=_pick(m, (2048, 1024, 512, 256, 128)), tn=_pick(dm, (256, 128)))
    yb = _s5_direction(u.reshape(batch, seq, dm), lre[1], lim[1], ldt[1], bre[1], bim[1],
                       cre[1], cim[1], reverse=True)
    to_pairs = lambda a: (a.reshape(batch, nchunk, t, g // 2, 2, hch)
                          .transpose(3, 1, 4, 0, 2, 5)
                          .reshape(g // 2, nchunk * 2 * batch, t * hch))
    m_mat, w_mat, v_mat, lam = _ssm_prep(lre, lim, ldt, bre.at[1].set(0.0), bim.at[1].set(0.0),
                                         cre, cim, d, batch)
    y2 = _ssm_core(to_pairs(u.astype(BF16)), m_mat, w_mat, v_mat, lam, to_pairs(yb), nchunk, batch)
    yg = (y2.reshape(g // 2, nchunk, 2, batch, t, hch)
          .transpose(3, 1, 4, 0, 2, 5)
          .reshape(m, dm))
    return _glu(yg, w_glu, tm=_pick(m, (2048, 1024, 512, 256, 128)), tn=_pick(dm, (256, 128)))


def _gelu_kernel(y_ref, o_ref):
    o_ref[...] = jax.nn.gelu(y_ref[...]).astype(o_ref.dtype)


def _gelu_cast(y):
    m, d = y.shape
    tm = _pick(m, (256, 128, 64, 32, 16, 8))
    row = pl.BlockSpec((tm, d), lambda i: (i, 0))
    return pl.pallas_call(
        _gelu_kernel, grid=(m // tm,), in_specs=[row], out_specs=row,
        out_shape=jax.ShapeDtypeStruct((m, d), BF16),
        compiler_params=_cparams(1, 32), name="gelu_cast",
    )(y)


def _recurrence_combine(left, right):
    a_l, b_l = left
    a_r, b_r = right
    return a_r * a_l, a_r * b_l + b_r


def _s5_direction(u, lam_re, lam_im, log_dt, b_re, b_im, c_re, c_im, reverse):
    bsz, seq, dm = u.shape
    g, p = lam_re.shape
    ug = u.reshape(bsz, seq, g, dm // g)
    lam = lax.complex(lam_re, lam_im)
    dt = jnp.exp(log_dt)[:, None]
    lam_bar = jnp.exp(lam * dt)
    b_bar = ((lam_bar - 1.0) / lam)[..., None] * lax.complex(b_re, b_im)
    c_mat = lax.complex(c_re, c_im)
    bu = jnp.einsum('blgh,gph->blgp', ug, b_bar)
    a = jnp.broadcast_to(lam_bar, (1, seq, g, p))
    _, states = lax.associative_scan(_recurrence_combine, (a, bu), axis=1, reverse=reverse)
    return jnp.real(jnp.einsum('blgp,ghp->blgh', states, c_mat)).reshape(bsz, seq, dm)


def kernel(x, rel_bias, pre_mix_norm, post_mix_norm, pre_ffn_norm, post_ffn_norm, attn_wqkv, attn_sink, attn_wo, ssm_w_in, ssm_lambda_re, ssm_lambda_im, ssm_log_dt, ssm_b_re, ssm_b_im, ssm_c_re, ssm_c_im, ssm_d, ssm_w_glu, ffn_w_gate, ffn_w_up, ffn_conv_w, ffn_conv_b, ffn_w_down):
    batch, seq, dm = x.shape
    depth = pre_mix_norm.shape[0]
    m = batch * seq
    heads = dm // HEAD_DIM
    kvh = (attn_wqkv.shape[2] // HEAD_DIM - heads) // 2
    dff = ffn_w_gate.shape[2]
    assert seq % ATTN_BLOCK == 0 and seq % SSM_CHUNK == 0 and (2 * batch) % 8 == 0
    tm_big = _pick(m, (2048, 1024, 512, 256, 128))
    tm_mid = _pick(m, (1024, 512, 256, 128))

    xf = x.reshape(m, dm)
    h = _norm_cast(xf, pre_mix_norm[0])
    bias = _attn_bias(rel_bias)
    for i in range(depth):
        j = i // 2
        if i % 2 == 0:
            qkv = _mm(h, attn_wqkv[j], BF16, tm=tm_big, tn=_pick(attn_wqkv.shape[2], (512, 256, 128)))
            o = _attention(qkv, bias, attn_sink[j], seq, heads, kvh)
            mix = _mm(o, attn_wo[j], F32, tm=tm_big, tn=_pick(dm, (512, 256, 128)))
        else:
            mix = _s5_mixer(h, ssm_w_in[j], ssm_lambda_re[j], ssm_lambda_im[j], ssm_log_dt[j],
                            ssm_b_re[j], ssm_b_im[j], ssm_c_re[j], ssm_c_im[j], ssm_d[j],
                            ssm_w_glu[j], batch, seq)
        xf, h = _resid_norm(xf, mix, post_mix_norm[i], pre_ffn_norm[i])
        hid = _ffn_in(h, ffn_w_gate[i], ffn_w_up[i], ffn_conv_w[i], ffn_conv_b[i], seq,
                      tn=_pick(dff, (256, 128)))
        kc = dff // 2 if (dff // 2) % 128 == 0 else dff
        f = _mm(hid, ffn_w_down[i], F32, tm=tm_mid, tn=_pick(dm, (256, 128)), kc=kc, vmem_mib=60)
        g_next = pre_mix_norm[i + 1] if i + 1 < depth else None
        xf, h = _resid_norm(xf, f, post_ffn_norm[i], g_next)
    return xf.reshape(batch, seq, dm)
```

```python
import functools
import math

import jax
import jax.numpy as jnp
from jax import lax
from jax.experimental import pallas as pl
from jax.experimental.pallas import tpu as pltpu

F32 = jnp.float32
BF16 = jnp.bfloat16

HEAD_DIM = 128
ATTN_BLOCK = 128
NUM_BUCKETS = 32
SSM_GROUP_CH = 16
SSM_STATE = 64
SSM_CHUNK = 16
RMS_EPS = 1e-6
NEG_INF = -1e30

MIB = 1024 * 1024


def _cparams(n_grid_dims, vmem_mib):
    return pltpu.CompilerParams(
        dimension_semantics=("arbitrary",) * n_grid_dims,
        vmem_limit_bytes=vmem_mib * MIB,
    )


def _pick(n, prefs):
    for p in prefs:
        if p <= n and n % p == 0:
            return p
    return n


def _rms(x, g):
    return x * lax.rsqrt(jnp.mean(x * x, axis=-1, keepdims=True) + RMS_EPS) * g


def _norm_kernel(x_ref, g_ref, h_ref):
    h_ref[...] = _rms(x_ref[...], g_ref[...]).astype(h_ref.dtype)


def _norm_cast(x, g):
    m, d = x.shape
    tm = _pick(m, (256, 128, 64, 32, 16, 8))
    return pl.pallas_call(
        _norm_kernel,
        grid=(m // tm,),
        in_specs=[pl.BlockSpec((tm, d), lambda i: (i, 0)),
                  pl.BlockSpec((1, d), lambda i: (0, 0))],
        out_specs=pl.BlockSpec((tm, d), lambda i: (i, 0)),
        out_shape=jax.ShapeDtypeStruct((m, d), BF16),
        compiler_params=_cparams(1, 32),
        name="norm_cast",
    )(x, g.reshape(1, d))


def _resid_norm_kernel(x_ref, m_ref, g1_ref, g2_ref, xo_ref, ho_ref):
    xn = x_ref[...] + _rms(m_ref[...], g1_ref[...])
    xo_ref[...] = xn
    ho_ref[...] = _rms(xn, g2_ref[...]).astype(ho_ref.dtype)


def _resid_kernel(x_ref, m_ref, g1_ref, xo_ref):
    xo_ref[...] = x_ref[...] + _rms(m_ref[...], g1_ref[...])


def _resid_norm(x, mix, g_post, g_next):
    m, d = x.shape
    tm = _pick(m, (128, 64, 32, 16, 8))
    row = pl.BlockSpec((tm, d), lambda i: (i, 0))
    vec = pl.BlockSpec((1, d), lambda i: (0, 0))
    if g_next is None:
        return pl.pallas_call(
            _resid_kernel, grid=(m // tm,),
            in_specs=[row, row, vec], out_specs=row,
            out_shape=jax.ShapeDtypeStruct((m, d), F32),
            compiler_params=_cparams(1, 32), name="resid",
        )(x, mix, g_post.reshape(1, d)), None
    return pl.pallas_call(
        _resid_norm_kernel, grid=(m // tm,),
        in_specs=[row, row, vec, vec], out_specs=[row, row],
        out_shape=[jax.ShapeDtypeStruct((m, d), F32), jax.ShapeDtypeStruct((m, d), BF16)],
        compiler_params=_cparams(1, 32), name="resid_norm",
    )(x, mix, g_post.reshape(1, d), g_next.reshape(1, d))


def _fetch_row_panel(a_hbm, a_ref, sem):
    @pl.when(pl.program_id(1) == 0)
    def _():
        tm = a_ref.shape[0]
        cp = pltpu.make_async_copy(
            a_hbm.at[pl.ds(pl.multiple_of(pl.program_id(0) * tm, tm), tm), :], a_ref, sem)
        cp.start()
        cp.wait()


def _panel_scratch(tm, k):
    return [pltpu.VMEM((tm, k), BF16), pltpu.SemaphoreType.DMA(())]


def _mm_kernel(a_hbm, w_ref, o_ref, a_ref, sem, *, kc):
    _fetch_row_panel(a_hbm, a_ref, sem)
    k = a_ref.shape[1]
    acc = None
    for k0 in range(0, k, kc):
        part = jnp.dot(a_ref[:, k0:k0 + kc], w_ref[k0:k0 + kc, :].astype(BF16),
                       preferred_element_type=F32)
        acc = part if acc is None else acc + part
    o_ref[...] = acc.astype(o_ref.dtype)


def _mm(a, w, out_dtype, tm, tn, kc=None, vmem_mib=56):
    m, k = a.shape
    n = w.shape[1]
    kc = k if kc is None else kc
    return pl.pallas_call(
        functools.partial(_mm_kernel, kc=kc),
        grid=(m // tm, n // tn),
        in_specs=[pl.BlockSpec(memory_space=pl.ANY),
                  pl.BlockSpec((k, tn), lambda i, j: (0, j))],
        out_specs=pl.BlockSpec((tm, tn), lambda i, j: (i, j)),
        out_shape=jax.ShapeDtypeStruct((m, n), out_dtype),
        scratch_shapes=_panel_scratch(tm, k),
        compiler_params=_cparams(2, vmem_mib),
        name="mm",
    )(a, w)


def _glu_kernel(a_hbm, wa_ref, wb_ref, o_ref, a_ref, sem):
    _fetch_row_panel(a_hbm, a_ref, sem)
    a = a_ref[...]
    ya = jnp.dot(a, wa_ref[...].astype(BF16), preferred_element_type=F32)
    yb = jnp.dot(a, wb_ref[...].astype(BF16), preferred_element_type=F32)
    o_ref[...] = (ya * jax.nn.sigmoid(yb)).astype(o_ref.dtype)


def _glu(a, w, tm, tn):
    m, k = a.shape
    n = w.shape[1] // 2
    nj = n // tn
    return pl.pallas_call(
        _glu_kernel,
        grid=(m // tm, nj),
        in_specs=[pl.BlockSpec(memory_space=pl.ANY),
                  pl.BlockSpec((k, tn), lambda i, j: (0, j)),
                  pl.BlockSpec((k, tn), lambda i, j: (0, j + nj))],
        out_specs=pl.BlockSpec((tm, tn), lambda i, j: (i, j)),
        out_shape=jax.ShapeDtypeStruct((m, n), F32),
        scratch_shapes=_panel_scratch(tm, k),
        compiler_params=_cparams(2, 56),
        name="glu",
    )(a, w, w)


def _ffn_in_kernel(a_hbm, wg_ref, wu_ref, cw_ref, cb_ref, o_ref, a_ref, sem):
    _fetch_row_panel(a_hbm, a_ref, sem)
    a = a_ref[...]
    g = jnp.dot(a, wg_ref[...].astype(BF16), preferred_element_type=F32)
    u = jnp.dot(a, wu_ref[...].astype(BF16), preferred_element_type=F32)
    rows = g.shape[0]
    row = lax.broadcasted_iota(jnp.int32, (rows, 1), 0)
    g_prev = jnp.where(row == 0, 0.0, pltpu.roll(g, 1, 0))
    g_next = jnp.where(row == rows - 1, 0.0, pltpu.roll(g, rows - 1, 0))
    gc = cw_ref[0:1, :] * g_prev + cw_ref[1:2, :] * g + cw_ref[2:3, :] * g_next + cb_ref[...]
    o_ref[...] = (gc * jax.nn.sigmoid(gc) * u).astype(o_ref.dtype)


def _ffn_in(h, w_gate, w_up, conv_w, conv_b, seq, tn):
    m, k = h.shape
    f = w_gate.shape[1]
    return pl.pallas_call(
        _ffn_in_kernel,
        grid=(m // seq, f // tn),
        in_specs=[pl.BlockSpec(memory_space=pl.ANY),
                  pl.BlockSpec((k, tn), lambda i, j: (0, j)),
                  pl.BlockSpec((k, tn), lambda i, j: (0, j)),
                  pl.BlockSpec((3, tn), lambda i, j: (0, j)),
                  pl.BlockSpec((1, tn), lambda i, j: (0, j))],
        out_specs=pl.BlockSpec((seq, tn), lambda i, j: (i, j)),
        out_shape=jax.ShapeDtypeStruct((m, f), BF16),
        scratch_shapes=_panel_scratch(seq, k),
        compiler_params=_cparams(2, 56),
        name="ffn_in",
    )(h, w_gate, w_up, conv_w, conv_b.reshape(1, f))


def _t5_bucket(rel):
    half = NUM_BUCKETS // 2
    max_exact = half // 2
    base = jnp.where(rel > 0, half, 0)
    n = jnp.abs(rel)
    nf = jnp.maximum(n, 1).astype(F32)
    large = max_exact + (jnp.log(nf / max_exact) / math.log(ATTN_BLOCK / max_exact)
                         * (half - max_exact)).astype(jnp.int32)
    large = jnp.minimum(large, half - 1)
    return base + jnp.where(n < max_exact, n, large)


def _bias_kernel(bucket_ref, inwin_ref, rbt_ref, o_ref):
    nb = rbt_ref.shape[1]
    lanes = bucket_ref.shape[1]
    onehot = (lax.broadcasted_iota(jnp.int32, (nb, lanes), 0) == bucket_ref[...]).astype(F32)
    bias = jnp.dot(rbt_ref[...], onehot, preferred_element_type=F32,
                   precision=lax.Precision.HIGHEST)
    o_ref[...] = jnp.where(inwin_ref[...] > 0, bias, NEG_INF)


def _attn_bias(rel_bias):
    nb, heads = rel_bias.shape
    blk = ATTN_BLOCK
    q_idx = jnp.arange(blk)[:, None]
    k_idx = jnp.arange(3 * blk)[None, :]
    rel = k_idx - blk - q_idx
    bucket = _t5_bucket(rel).reshape(1, 3 * blk * blk).astype(jnp.int32)
    inwin = (jnp.abs(rel) <= blk).astype(jnp.int32).reshape(1, 3 * blk * blk)
    tl = 4096
    out = pl.pallas_call(
        _bias_kernel,
        grid=(3 * blk * blk // tl,),
        in_specs=[pl.BlockSpec((1, tl), lambda i: (0, i)),
                  pl.BlockSpec((1, tl), lambda i: (0, i)),
                  pl.BlockSpec((heads, nb), lambda i: (0, 0))],
        out_specs=pl.BlockSpec((heads, tl), lambda i: (0, i)),
        out_shape=jax.ShapeDtypeStruct((heads, 3 * blk * blk), F32),
        compiler_params=_cparams(1, 32),
        name="attn_bias",
    )(bucket, inwin, rel_bias.T)
    return out.reshape(heads, blk, 3 * blk)


def _attn_kernel(sink_ref, q_ref, kp_ref, ko_ref, kn_ref, vp_ref, vo_ref, vn_ref, bias_ref,
                 o_ref, *, nblk, kvh, grp):
    blk, hd = ATTN_BLOCK, HEAD_DIM
    n = pl.program_id(0) % nblk
    col = lax.broadcasted_iota(jnp.int32, (1, 3 * blk), 1)
    key_pos = (n - 1) * blk + col
    edge = jnp.where((key_pos >= 0) & (key_pos < nblk * blk), 0.0, NEG_INF)
    scale = hd ** -0.5
    for kh in range(kvh):
        ks = slice(kh * hd, (kh + 1) * hd)
        k = jnp.concatenate([kp_ref[:, ks], ko_ref[:, ks], kn_ref[:, ks]], axis=0)
        v = jnp.concatenate([vp_ref[:, ks], vo_ref[:, ks], vn_ref[:, ks]], axis=0)
        heads = [kh * grp + g for g in range(grp)]
        q = jnp.concatenate([q_ref[:, h * hd:(h + 1) * hd] for h in heads], axis=0)
        s = lax.dot_general(q, k, (((1,), (1,)), ((), ())), preferred_element_type=F32) * scale
        s = s + bias_ref[kh * grp:(kh + 1) * grp].reshape(grp * blk, 3 * blk) + edge
        sink = jnp.concatenate([jnp.full((blk, 1), sink_ref[h], F32) for h in heads], axis=0)
        mx = jnp.maximum(jnp.max(s, axis=-1, keepdims=True), sink)
        p = jnp.exp(s - mx)
        denom = jnp.sum(p, axis=-1, keepdims=True) + jnp.exp(sink - mx)
        o = jnp.dot(p.astype(BF16), v, preferred_element_type=F32) / denom
        for g, h in enumerate(heads):
            o_ref[:, h * hd:(h + 1) * hd] = o[g * blk:(g + 1) * blk].astype(o_ref.dtype)


def _attention(qkv, bias, sink, seq, heads, kvh):
    m = qkv.shape[0]
    blk, hd = ATTN_BLOCK, HEAD_DIM
    nblk = seq // blk
    grp = heads // kvh
    qw, kw = heads * hd, kvh * hd
    kcol, vcol = qw // kw, qw // kw + 1

    def prev(i):
        return jnp.where(i % nblk == 0, i, i - 1)

    def nxt(i):
        return jnp.where(i % nblk == nblk - 1, i, i + 1)

    kv = lambda rowf, colb: pl.BlockSpec((blk, kw), lambda i: (rowf(i), colb))
    same = lambda i: i
    return pl.pallas_call(
        functools.partial(_attn_kernel, nblk=nblk, kvh=kvh, grp=grp),
        grid=(m // blk,),
        in_specs=[pl.BlockSpec(memory_space=pltpu.SMEM),
                  pl.BlockSpec((blk, qw), lambda i: (i, 0)),
                  kv(prev, kcol), kv(same, kcol), kv(nxt, kcol),
                  kv(prev, vcol), kv(same, vcol), kv(nxt, vcol),
                  pl.BlockSpec((heads, blk, 3 * blk), lambda i: (0, 0, 0))],
        out_specs=pl.BlockSpec((blk, qw), lambda i: (i, 0)),
        out_shape=jax.ShapeDtypeStruct((m, qw), BF16),
        compiler_params=_cparams(1, 48),
        name="attention",
    )(sink, qkv, qkv, qkv, qkv, qkv, qkv, qkv, bias)


def _cis_pow(zr, zi, e):
    mag = jnp.exp(zr * e)
    return mag * jnp.cos(zi * e), mag * jnp.sin(zi * e)


def _ssm_prep_group(g, lre_r, lim_r, ldt_r, lre_c, lim_c, ldt_c, btr, bti, ctr, cti, dcol):
    t, hch, p = SSM_CHUNK, SSM_GROUP_CH, SSM_STATE
    th, p2 = t * hch, 2 * p
    hi_prec = lax.Precision.HIGHEST
    ar, ai = lre_r[g], lim_r[g]
    dt = jnp.exp(ldt_r[g])
    zr, zi = ar * dt, ai * dt
    lbr, lbi = _cis_pow(zr, zi, 1.0)
    nr = lbr - 1.0
    den = ar * ar + ai * ai
    cr = (nr * ar + lbi * ai) / den
    ci = (lbi * ar - nr * ai) / den
    b_r, b_i = btr[g], bti[g]
    bbr = cr * b_r - ci * b_i
    bbi = cr * b_i + ci * b_r
    row = lax.broadcasted_iota(jnp.int32, (th, p2), 0)
    lane = lax.broadcasted_iota(jnp.int32, (th, p2), 1)
    j = row // hch
    e_w = jnp.where(lane < p, t - 1 - j, j).astype(F32)
    pr, pi = _cis_pow(zr, zi, e_w)
    bt_r = jnp.concatenate([bbr] * t, axis=0)
    bt_i = jnp.concatenate([bbi] * t, axis=0)
    w_mat = jnp.concatenate([pr * bt_r - pi * bt_i, pr * bt_i + pi * bt_r], axis=1)
    l_r, l_i = _cis_pow(zr, zi, float(t))
    lam = jnp.concatenate([l_r, l_i], axis=1)
    arc, aic = lre_c[g], lim_c[g]
    dtc = jnp.exp(ldt_c[g])
    zrc, zic = arc * dtc, aic * dtc
    rowc = lax.broadcasted_iota(jnp.int32, (p2, th), 0)
    lanec = lax.broadcasted_iota(jnp.int32, (p2, th), 1)
    nn = lanec // hch
    e_k = jnp.where(rowc < p, nn, t - 1 - nn).astype(F32)
    qr, qi = _cis_pow(zrc, zic, e_k)
    tile = (lax.broadcasted_iota(jnp.int32, (hch, th), 1) % hch
            == lax.broadcasted_iota(jnp.int32, (hch, th), 0)).astype(F32)
    c_r = jnp.dot(ctr[g], tile, preferred_element_type=F32, precision=hi_prec)
    c_i = jnp.dot(cti[g], tile, preferred_element_type=F32, precision=hi_prec)
    e_r = c_r * qr - c_i * qi
    e_i = c_r * qi + c_i * qr
    lbrc, lbic = _cis_pow(zrc, zic, 1.0)
    v_mat = jnp.concatenate([e_r * lbrc - e_i * lbic, -(e_r * lbic + e_i * lbrc)], axis=0)
    fwd_lane = lax.broadcasted_iota(jnp.int32, (hch, p2), 1) < p
    rhs = jnp.concatenate([e_r, e_i], axis=0)
    lhs0 = jnp.concatenate([jnp.where(fwd_lane, bbr, 0.0), -jnp.where(fwd_lane, bbi, 0.0)], axis=1)
    lhs1 = jnp.concatenate([jnp.where(fwd_lane, 0.0, bbr), -jnp.where(fwd_lane, 0.0, bbi)], axis=1)
    k0 = jnp.dot(lhs0, rhs, preferred_element_type=F32, precision=hi_prec)
    k1 = jnp.dot(lhs1, rhs, preferred_element_type=F32, precision=hi_prec)
    lane_m = lax.broadcasted_iota(jnp.int32, (hch, th), 1)
    row_m = lax.broadcasted_iota(jnp.int32, (hch, th), 0)
    d_g = dcol[g]
    blocks = []
    for jj in range(t):
        a = pltpu.roll(k0, hch * jj, 1) if jj else k0
        a = jnp.where(lane_m >= hch * jj, a, 0.0)
        sh = (hch * (jj + 1)) % th
        b = pltpu.roll(k1, sh, 1) if sh else k1
        b = jnp.where(lane_m < hch * (jj + 1), b, 0.0)
        dd = jnp.where(lane_m == hch * jj + row_m, d_g, 0.0)
        blocks.append(a + b + dd)
    m_mat = jnp.concatenate(blocks, axis=0)
    return m_mat, w_mat, v_mat, lam


def _ssm_prep_kernel(lre_r, lim_r, ldt_r, lre_c, lim_c, ldt_c, btr, bti, ctr, cti, dcol,
                     m_ref, w_ref, v_ref, lam_ref, *, pairs, nb):
    ins = (lre_r, lim_r, ldt_r, lre_c, lim_c, ldt_c, btr, bti, ctr, cti, dcol)
    first = lax.broadcasted_iota(jnp.int32, (2 * nb, 4 * SSM_STATE), 0) < nb

    def body(q, carry):
        lams = []
        for s in range(2):
            g = 2 * q + s
            m_mat, w_mat, v_mat, lam = _ssm_prep_group(g, *ins)
            m_ref[g] = m_mat.astype(m_ref.dtype)
            w_ref[g] = w_mat.astype(w_ref.dtype)
            v_ref[g] = v_mat.astype(v_ref.dtype)
            lams.append(jnp.broadcast_to(lam, (2 * nb, 4 * SSM_STATE)))
        lam_ref[q] = jnp.where(first, lams[0], lams[1])
        return carry

    lax.fori_loop(0, pairs, body, 0)


def _ssm_prep(lre, lim, ldt, bre, bim, cre, cim, d, nb):
    _, g, p = lre.shape
    hch, t = SSM_GROUP_CH, SSM_CHUNK
    th = t * hch
    cat = lambda a: jnp.concatenate([a[0], a[1]], axis=-1)
    lre2, lim2 = cat(lre), cat(lim)
    ldt2 = jnp.repeat(ldt.T, p, axis=1)
    bt = lambda a: jnp.transpose(a, (1, 3, 0, 2)).reshape(g, hch, 2 * p)
    ct = lambda a: jnp.transpose(a, (1, 0, 3, 2)).reshape(g, 2 * p, hch)
    gp = _pick(g, (8, 4, 2))
    rowv = pl.BlockSpec((gp, 1, 2 * p), lambda i: (i, 0, 0))
    colv = pl.BlockSpec((gp, 2 * p, 1), lambda i: (i, 0, 0))
    mat = pl.BlockSpec((gp, th, th), lambda i: (i, 0, 0))
    return pl.pallas_call(
        functools.partial(_ssm_prep_kernel, pairs=gp // 2, nb=nb),
        grid=(g // gp,),
        in_specs=[rowv, rowv, rowv, colv, colv, colv,
                  pl.BlockSpec((gp, hch, 2 * p), lambda i: (i, 0, 0)),
                  pl.BlockSpec((gp, hch, 2 * p), lambda i: (i, 0, 0)),
                  pl.BlockSpec((gp, 2 * p, hch), lambda i: (i, 0, 0)),
                  pl.BlockSpec((gp, 2 * p, hch), lambda i: (i, 0, 0)),
                  pl.BlockSpec((gp, hch, 1), lambda i: (i, 0, 0))],
        out_specs=[mat, mat, mat,
                   pl.BlockSpec((gp // 2, 2 * nb, 4 * p), lambda i: (i, 0, 0))],
        out_shape=[jax.ShapeDtypeStruct((g, th, th), BF16)] * 3
        + [jax.ShapeDtypeStruct((g // 2, 2 * nb, 4 * p), F32)],
        compiler_params=_cparams(1, 32),
        name="ssm_prep",
    )(lre2.reshape(g, 1, 2 * p), lim2.reshape(g, 1, 2 * p), ldt2.reshape(g, 1, 2 * p),
      lre2.reshape(g, 2 * p, 1), lim2.reshape(g, 2 * p, 1), ldt2.reshape(g, 2 * p, 1),
      bt(bre), bt(bim), ct(cre), ct(cim), d.reshape(g, hch, 1))


def _ssm_kernel(u_ref, m_ref, w_ref, v_ref, lam_ref, yext_ref, y_ref, yin_s, s_s, xf_s, xb_s,
                *, pb, nchunk, nb):
    slab = 2 * nb
    rows = nchunk * slab
    p2 = 2 * SSM_STATE
    first = (lax.broadcasted_iota(jnp.int32, (rows, 1), 0) // nb) % 2 == 0

    def pair_dot(lhs, mats, q):
        ya = jnp.dot(lhs, mats[2 * q], preferred_element_type=F32)
        yb = jnp.dot(lhs, mats[2 * q + 1], preferred_element_type=F32)
        return jnp.where(first, ya, yb)

    for q in range(pb):
        u = u_ref[q]
        yin_s[q] = pair_dot(u, m_ref, q)
        s_s[q] = pair_dot(u, w_ref, q)

    fwd = lax.broadcasted_iota(jnp.int32, (slab, p2), 1) < SSM_STATE
    lam_r = [lam_ref[q, :, 0:p2] for q in range(pb)]
    lam_i = [lam_ref[q, :, p2:2 * p2] for q in range(pb)]

    def step(k, carry):
        kf = pl.multiple_of(k * slab, slab)
        kb = pl.multiple_of((nchunk - 1 - k) * slab, slab)
        out = []
        for q in range(pb):
            xr, xi = carry[2 * q], carry[2 * q + 1]
            xf_s[q, pl.ds(kf, slab), 0:p2] = xr
            xf_s[q, pl.ds(kf, slab), p2:2 * p2] = xi
            xb_s[q, pl.ds(kb, slab), 0:p2] = xr
            xb_s[q, pl.ds(kb, slab), p2:2 * p2] = xi
            sr = jnp.where(fwd, s_s[q, pl.ds(kf, slab), 0:p2], s_s[q, pl.ds(kb, slab), 0:p2])
            si = jnp.where(fwd, s_s[q, pl.ds(kf, slab), p2:2 * p2],
                           s_s[q, pl.ds(kb, slab), p2:2 * p2])
            out.append(lam_r[q] * xr - lam_i[q] * xi + sr)
            out.append(lam_r[q] * xi + lam_i[q] * xr + si)
        return tuple(out)

    zero = jnp.zeros((slab, p2), F32)
    lax.fori_loop(0, nchunk, step, (zero,) * (2 * pb))

    fwd_all = lax.broadcasted_iota(jnp.int32, (rows, 2 * p2), 1) % p2 < SSM_STATE
    for q in range(pb):
        x = jnp.where(fwd_all, xf_s[q], xb_s[q]).astype(BF16)
        y = yin_s[q] + pair_dot(x, v_ref, q) + yext_ref[q]
        y_ref[q] = jax.nn.gelu(y).astype(y_ref.dtype)


def _ssm_core(u2, m_mat, w_mat, v_mat, lam, yext, nchunk, nb):
    npair, rows, th = u2.shape
    pb = _pick(npair, (4, 2, 1))
    p4 = 4 * SSM_STATE
    big = pl.BlockSpec((pb, rows, th), lambda i: (i, 0, 0))
    mat = pl.BlockSpec((2 * pb, th, th), lambda i: (i, 0, 0))
    return pl.pallas_call(
        functools.partial(_ssm_kernel, pb=pb, nchunk=nchunk, nb=nb),
        grid=(npair // pb,),
        in_specs=[big, mat, mat, mat, pl.BlockSpec((pb, 2 * nb, p4), lambda i: (i, 0, 0)), big],
        out_specs=big,
        out_shape=jax.ShapeDtypeStruct((npair, rows, th), F32),
        scratch_shapes=[pltpu.VMEM((pb, rows, th), F32), pltpu.VMEM((pb, rows, p4), F32),
                        pltpu.VMEM((pb, rows, p4), F32), pltpu.VMEM((pb, rows, p4), F32)],
        compiler_params=_cparams(1, 56),
        name="ssm_core",
    )(u2, m_mat, w_mat, v_mat, lam, yext)


def _s5_mixer(h, w_in, lre, lim, ldt, bre, bim, cre, cim, d, w_glu, batch, seq):
    m, dm = h.shape
    hch, t = SSM_GROUP_CH, SSM_CHUNK
    g = dm // hch
    nchunk = seq // t
    u = _mm(h, w_in, F32, tm=_pick(m, (2048, 1024, 512, 256, 128)), tn=_pick(dm, (256, 128)))
    yb = _s5_direction(u.reshape(batch, seq, dm), lre[1], lim[1], ldt[1], bre[1], bim[1],
                       cre[1], cim[1], reverse=True)
    to_pairs = lambda a: (a.reshape(batch, nchunk, t, g // 2, 2, hch)
                          .transpose(3, 1, 4, 0, 2, 5)
                          .reshape(g // 2, nchunk * 2 * batch, t * hch))
    m_mat, w_mat, v_mat, lam = _ssm_prep(lre, lim, ldt, bre.at[1].set(0.0), bim.at[1].set(0.0),
                                         cre, cim, d, batch)
    y2 = _ssm_core(to_pairs(u.astype(BF16)), m_mat, w_mat, v_mat, lam, to_pairs(yb), nchunk, batch)
    yg = (y2.reshape(g // 2, nchunk, 2, batch, t, hch)
          .transpose(3, 1, 4, 0, 2, 5)
          .reshape(m, dm)).astype(BF16)
    return _glu(yg, w_glu, tm=_pick(m, (2048, 1024, 512, 256, 128)), tn=_pick(dm, (256, 128)))


def _gelu_kernel(y_ref, o_ref):
    o_ref[...] = jax.nn.gelu(y_ref[...]).astype(o_ref.dtype)


def _gelu_cast(y):
    m, d = y.shape
    tm = _pick(m, (256, 128, 64, 32, 16, 8))
    row = pl.BlockSpec((tm, d), lambda i: (i, 0))
    return pl.pallas_call(
        _gelu_kernel, grid=(m // tm,), in_specs=[row], out_specs=row,
        out_shape=jax.ShapeDtypeStruct((m, d), BF16),
        compiler_params=_cparams(1, 32), name="gelu_cast",
    )(y)


def _recurrence_combine(left, right):
    a_l, b_l = left
    a_r, b_r = right
    return a_r * a_l, a_r * b_l + b_r


def _s5_direction(u, lam_re, lam_im, log_dt, b_re, b_im, c_re, c_im, reverse):
    bsz, seq, dm = u.shape
    g, p = lam_re.shape
    ug = u.reshape(bsz, seq, g, dm // g)
    lam = lax.complex(lam_re, lam_im)
    dt = jnp.exp(log_dt)[:, None]
    lam_bar = jnp.exp(lam * dt)
    b_bar = ((lam_bar - 1.0) / lam)[..., None] * lax.complex(b_re, b_im)
    c_mat = lax.complex(c_re, c_im)
    bu = jnp.einsum('blgh,gph->blgp', ug, b_bar)
    a = jnp.broadcast_to(lam_bar, (1, seq, g, p))
    _, states = lax.associative_scan(_recurrence_combine, (a, bu), axis=1, reverse=reverse)
    return jnp.real(jnp.einsum('blgp,ghp->blgh', states, c_mat)).reshape(bsz, seq, dm)


def kernel(x, rel_bias, pre_mix_norm, post_mix_norm, pre_ffn_norm, post_ffn_norm, attn_wqkv, attn_sink, attn_wo, ssm_w_in, ssm_lambda_re, ssm_lambda_im, ssm_log_dt, ssm_b_re, ssm_b_im, ssm_c_re, ssm_c_im, ssm_d, ssm_w_glu, ffn_w_gate, ffn_w_up, ffn_conv_w, ffn_conv_b, ffn_w_down):
    batch, seq, dm = x.shape
    depth = pre_mix_norm.shape[0]
    m = batch * seq
    heads = dm // HEAD_DIM
    kvh = (attn_wqkv.shape[2] // HEAD_DIM - heads) // 2
    dff = ffn_w_gate.shape[2]
    assert seq % ATTN_BLOCK == 0 and seq % SSM_CHUNK == 0 and (2 * batch) % 8 == 0
    tm_big = _pick(m, (2048, 1024, 512, 256, 128))
    tm_mid = _pick(m, (1024, 512, 256, 128))

    xf = x.reshape(m, dm)
    h = _norm_cast(xf, pre_mix_norm[0])
    bias = _attn_bias(rel_bias)
    for i in range(depth):
        j = i // 2
        if i % 2 == 0:
            qkv = _mm(h, attn_wqkv[j], BF16, tm=tm_big, tn=_pick(attn_wqkv.shape[2], (512, 256, 128)))
            o = _attention(qkv, bias, attn_sink[j], seq, heads, kvh)
            mix = _mm(o, attn_wo[j], F32, tm=tm_big, tn=_pick(dm, (512, 256, 128)))
        else:
            mix = _s5_mixer(h, ssm_w_in[j], ssm_lambda_re[j], ssm_lambda_im[j], ssm_log_dt[j],
                            ssm_b_re[j], ssm_b_im[j], ssm_c_re[j], ssm_c_im[j], ssm_d[j],
                            ssm_w_glu[j], batch, seq)
        xf, h = _resid_norm(xf, mix, post_mix_norm[i], pre_ffn_norm[i])
        hid = _ffn_in(h, ffn_w_gate[i], ffn_w_up[i], ffn_conv_w[i], ffn_conv_b[i], seq,
                      tn=_pick(dff, (256, 128)))
        kc = dff // 2 if (dff // 2) % 128 == 0 else dff
        f = _mm(hid, ffn_w_down[i], F32, tm=tm_mid, tn=_pick(dm, (256, 128)), kc=kc, vmem_mib=60)
        g_next = pre_mix_norm[i + 1] if i + 1 < depth else None
        xf, h = _resid_norm(xf, f, post_ffn_norm[i], g_next)
    return xf.reshape(batch, seq, dm)
```

```python
import functools
import math

import jax
import jax.numpy as jnp
from jax import lax
from jax.experimental import pallas as pl
from jax.experimental.pallas import tpu as pltpu

F32 = jnp.float32
BF16 = jnp.bfloat16

HEAD_DIM = 128
ATTN_BLOCK = 128
NUM_BUCKETS = 32
SSM_GROUP_CH = 16
SSM_STATE = 64
SSM_CHUNK = 16
RMS_EPS = 1e-6
NEG_INF = -1e30

MIB = 1024 * 1024


def _cparams(n_grid_dims, vmem_mib):
    return pltpu.CompilerParams(
        dimension_semantics=("arbitrary",) * n_grid_dims,
        vmem_limit_bytes=vmem_mib * MIB,
    )


def _pick(n, prefs):
    for p in prefs:
        if p <= n and n % p == 0:
            return p
    return n


def _rms(x, g):
    return x * lax.rsqrt(jnp.mean(x * x, axis=-1, keepdims=True) + RMS_EPS) * g


def _norm_kernel(x_ref, g_ref, h_ref):
    h_ref[...] = _rms(x_ref[...], g_ref[...]).astype(h_ref.dtype)


def _norm_cast(x, g):
    m, d = x.shape
    tm = _pick(m, (256, 128, 64, 32, 16, 8))
    return pl.pallas_call(
        _norm_kernel,
        grid=(m // tm,),
        in_specs=[pl.BlockSpec((tm, d), lambda i: (i, 0)),
                  pl.BlockSpec((1, d), lambda i: (0, 0))],
        out_specs=pl.BlockSpec((tm, d), lambda i: (i, 0)),
        out_shape=jax.ShapeDtypeStruct((m, d), BF16),
        compiler_params=_cparams(1, 32),
        name="norm_cast",
    )(x, g.reshape(1, d))


def _resid_norm_kernel(x_ref, m_ref, g1_ref, g2_ref, xo_ref, ho_ref):
    xn = x_ref[...] + _rms(m_ref[...], g1_ref[...])
    xo_ref[...] = xn
    ho_ref[...] = _rms(xn, g2_ref[...]).astype(ho_ref.dtype)


def _resid_kernel(x_ref, m_ref, g1_ref, xo_ref):
    xo_ref[...] = x_ref[...] + _rms(m_ref[...], g1_ref[...])


def _resid_norm(x, mix, g_post, g_next):
    m, d = x.shape
    tm = _pick(m, (128, 64, 32, 16, 8))
    row = pl.BlockSpec((tm, d), lambda i: (i, 0))
    vec = pl.BlockSpec((1, d), lambda i: (0, 0))
    if g_next is None:
        return pl.pallas_call(
            _resid_kernel, grid=(m // tm,),
            in_specs=[row, row, vec], out_specs=row,
            out_shape=jax.ShapeDtypeStruct((m, d), F32),
            compiler_params=_cparams(1, 32), name="resid",
        )(x, mix, g_post.reshape(1, d)), None
    return pl.pallas_call(
        _resid_norm_kernel, grid=(m // tm,),
        in_specs=[row, row, vec, vec], out_specs=[row, row],
        out_shape=[jax.ShapeDtypeStruct((m, d), F32), jax.ShapeDtypeStruct((m, d), BF16)],
        compiler_params=_cparams(1, 32), name="resid_norm",
    )(x, mix, g_post.reshape(1, d), g_next.reshape(1, d))


def _fetch_row_panel(a_hbm, a_ref, sem):
    @pl.when(pl.program_id(1) == 0)
    def _():
        tm = a_ref.shape[0]
        cp = pltpu.make_async_copy(
            a_hbm.at[pl.ds(pl.multiple_of(pl.program_id(0) * tm, tm), tm), :], a_ref, sem)
        cp.start()
        cp.wait()


def _panel_scratch(tm, k):
    return [pltpu.VMEM((tm, k), BF16), pltpu.SemaphoreType.DMA(())]


def _mm_kernel(a_hbm, w_ref, o_ref, a_ref, sem, *, kc):
    _fetch_row_panel(a_hbm, a_ref, sem)
    k = a_ref.shape[1]
    acc = None
    for k0 in range(0, k, kc):
        part = jnp.dot(a_ref[:, k0:k0 + kc], w_ref[k0:k0 + kc, :].astype(BF16),
                       preferred_element_type=F32)
        acc = part if acc is None else acc + part
    o_ref[...] = acc.astype(o_ref.dtype)


def _mm(a, w, out_dtype, tm, tn, kc=None, vmem_mib=56):
    m, k = a.shape
    n = w.shape[1]
    kc = k if kc is None else kc
    return pl.pallas_call(
        functools.partial(_mm_kernel, kc=kc),
        grid=(m // tm, n // tn),
        in_specs=[pl.BlockSpec(memory_space=pl.ANY),
                  pl.BlockSpec((k, tn), lambda i, j: (0, j))],
        out_specs=pl.BlockSpec((tm, tn), lambda i, j: (i, j)),
        out_shape=jax.ShapeDtypeStruct((m, n), out_dtype),
        scratch_shapes=_panel_scratch(tm, k),
        compiler_params=_cparams(2, vmem_mib),
        name="mm",
    )(a, w)


def _glu_kernel(a_hbm, wa_ref, wb_ref, o_ref, a_ref, sem):
    _fetch_row_panel(a_hbm, a_ref, sem)
    a = a_ref[...]
    ya = jnp.dot(a, wa_ref[...].astype(BF16), preferred_element_type=F32)
    yb = jnp.dot(a, wb_ref[...].astype(BF16), preferred_element_type=F32)
    o_ref[...] = (ya * jax.nn.sigmoid(yb)).astype(o_ref.dtype)


def _glu(a, w, tm, tn):
    m, k = a.shape
    n = w.shape[1] // 2
    nj = n // tn
    return pl.pallas_call(
        _glu_kernel,
        grid=(m // tm, nj),
        in_specs=[pl.BlockSpec(memory_space=pl.ANY),
                  pl.BlockSpec((k, tn), lambda i, j: (0, j)),
                  pl.BlockSpec((k, tn), lambda i, j: (0, j + nj))],
        out_specs=pl.BlockSpec((tm, tn), lambda i, j: (i, j)),
        out_shape=jax.ShapeDtypeStruct((m, n), F32),
        scratch_shapes=_panel_scratch(tm, k),
        compiler_params=_cparams(2, 56),
        name="glu",
    )(a, w, w)


def _ffn_in_kernel(a_hbm, wg_ref, wu_ref, cw_ref, cb_ref, o_ref, a_ref, sem):
    _fetch_row_panel(a_hbm, a_ref, sem)
    a = a_ref[...]
    g = jnp.dot(a, wg_ref[...].astype(BF16), preferred_element_type=F32)
    u = jnp.dot(a, wu_ref[...].astype(BF16), preferred_element_type=F32)
    rows = g.shape[0]
    row = lax.broadcasted_iota(jnp.int32, (rows, 1), 0)
    g_prev = jnp.where(row == 0, 0.0, pltpu.roll(g, 1, 0))
    g_next = jnp.where(row == rows - 1, 0.0, pltpu.roll(g, rows - 1, 0))
    gc = cw_ref[0:1, :] * g_prev + cw_ref[1:2, :] * g + cw_ref[2:3, :] * g_next + cb_ref[...]
    o_ref[...] = (gc * jax.nn.sigmoid(gc) * u).astype(o_ref.dtype)


def _ffn_in(h, w_gate, w_up, conv_w, conv_b, seq, tn):
    m, k = h.shape
    f = w_gate.shape[1]
    return pl.pallas_call(
        _ffn_in_kernel,
        grid=(m // seq, f // tn),
        in_specs=[pl.BlockSpec(memory_space=pl.ANY),
                  pl.BlockSpec((k, tn), lambda i, j: (0, j)),
                  pl.BlockSpec((k, tn), lambda i, j: (0, j)),
                  pl.BlockSpec((3, tn), lambda i, j: (0, j)),
                  pl.BlockSpec((1, tn), lambda i, j: (0, j))],
        out_specs=pl.BlockSpec((seq, tn), lambda i, j: (i, j)),
        out_shape=jax.ShapeDtypeStruct((m, f), BF16),
        scratch_shapes=_panel_scratch(seq, k),
        compiler_params=_cparams(2, 56),
        name="ffn_in",
    )(h, w_gate, w_up, conv_w, conv_b.reshape(1, f))


def _t5_bucket(rel):
    half = NUM_BUCKETS // 2
    max_exact = half // 2
    base = jnp.where(rel > 0, half, 0)
    n = jnp.abs(rel)
    nf = jnp.maximum(n, 1).astype(F32)
    large = max_exact + (jnp.log(nf / max_exact) / math.log(ATTN_BLOCK / max_exact)
                         * (half - max_exact)).astype(jnp.int32)
    large = jnp.minimum(large, half - 1)
    return base + jnp.where(n < max_exact, n, large)


def _bias_kernel(bucket_ref, inwin_ref, rbt_ref, o_ref):
    nb = rbt_ref.shape[1]
    lanes = bucket_ref.shape[1]
    onehot = (lax.broadcasted_iota(jnp.int32, (nb, lanes), 0) == bucket_ref[...]).astype(F32)
    bias = jnp.dot(rbt_ref[...], onehot, preferred_element_type=F32,
                   precision=lax.Precision.HIGHEST)
    o_ref[...] = jnp.where(inwin_ref[...] > 0, bias, NEG_INF)


def _attn_bias(rel_bias):
    nb, heads = rel_bias.shape
    blk = ATTN_BLOCK
    q_idx = jnp.arange(blk)[:, None]
    k_idx = jnp.arange(3 * blk)[None, :]
    rel = k_idx - blk - q_idx
    bucket = _t5_bucket(rel).reshape(1, 3 * blk * blk).astype(jnp.int32)
    inwin = (jnp.abs(rel) <= blk).astype(jnp.int32).reshape(1, 3 * blk * blk)
    tl = 4096
    out = pl.pallas_call(
        _bias_kernel,
        grid=(3 * blk * blk // tl,),
        in_specs=[pl.BlockSpec((1, tl), lambda i: (0, i)),
                  pl.BlockSpec((1, tl), lambda i: (0, i)),
                  pl.BlockSpec((heads, nb), lambda i: (0, 0))],
        out_specs=pl.BlockSpec((heads, tl), lambda i: (0, i)),
        out_shape=jax.ShapeDtypeStruct((heads, 3 * blk * blk), F32),
        compiler_params=_cparams(1, 32),
        name="attn_bias",
    )(bucket, inwin, rel_bias.T)
    return out.reshape(heads, blk, 3 * blk)


def _attn_kernel(sink_ref, q_ref, kp_ref, ko_ref, kn_ref, vp_ref, vo_ref, vn_ref, bias_ref,
                 o_ref, *, nblk, kvh, grp):
    blk, hd = ATTN_BLOCK, HEAD_DIM
    n = pl.program_id(0) % nblk
    col = lax.broadcasted_iota(jnp.int32, (1, 3 * blk), 1)
    key_pos = (n - 1) * blk + col
    edge = jnp.where((key_pos >= 0) & (key_pos < nblk * blk), 0.0, NEG_INF)
    scale = hd ** -0.5
    for kh in range(kvh):
        ks = slice(kh * hd, (kh + 1) * hd)
        k = jnp.concatenate([kp_ref[:, ks], ko_ref[:, ks], kn_ref[:, ks]], axis=0)
        v = jnp.concatenate([vp_ref[:, ks], vo_ref[:, ks], vn_ref[:, ks]], axis=0)
        heads = [kh * grp + g for g in range(grp)]
        q = jnp.concatenate([q_ref[:, h * hd:(h + 1) * hd] for h in heads], axis=0)
        s = lax.dot_general(q, k, (((1,), (1,)), ((), ())), preferred_element_type=F32) * scale
        s = s + bias_ref[kh * grp:(kh + 1) * grp].reshape(grp * blk, 3 * blk) + edge
        sink = jnp.concatenate([jnp.full((blk, 1), sink_ref[h], F32) for h in heads], axis=0)
        mx = jnp.maximum(jnp.max(s, axis=-1, keepdims=True), sink)
        p = jnp.exp(s - mx)
        denom = jnp.sum(p, axis=-1, keepdims=True) + jnp.exp(sink - mx)
        o = jnp.dot(p.astype(BF16), v, preferred_element_type=F32) / denom
        for g, h in enumerate(heads):
            o_ref[:, h * hd:(h + 1) * hd] = o[g * blk:(g + 1) * blk].astype(o_ref.dtype)


def _attention(qkv, bias, sink, seq, heads, kvh):
    m = qkv.shape[0]
    blk, hd = ATTN_BLOCK, HEAD_DIM
    nblk = seq // blk
    grp = heads // kvh
    qw, kw = heads * hd, kvh * hd
    kcol, vcol = qw // kw, qw // kw + 1

    def prev(i):
        return jnp.where(i % nblk == 0, i, i - 1)

    def nxt(i):
        return jnp.where(i % nblk == nblk - 1, i, i + 1)

    kv = lambda rowf, colb: pl.BlockSpec((blk, kw), lambda i: (rowf(i), colb))
    same = lambda i: i
    return pl.pallas_call(
        functools.partial(_attn_kernel, nblk=nblk, kvh=kvh, grp=grp),
        grid=(m // blk,),
        in_specs=[pl.BlockSpec(memory_space=pltpu.SMEM),
                  pl.BlockSpec((blk, qw), lambda i: (i, 0)),
                  kv(prev, kcol), kv(same, kcol), kv(nxt, kcol),
                  kv(prev, vcol), kv(same, vcol), kv(nxt, vcol),
                  pl.BlockSpec((heads, blk, 3 * blk), lambda i: (0, 0, 0))],
        out_specs=pl.BlockSpec((blk, qw), lambda i: (i, 0)),
        out_shape=jax.ShapeDtypeStruct((m, qw), BF16),
        compiler_params=_cparams(1, 48),
        name="attention",
    )(sink, qkv, qkv, qkv, qkv, qkv, qkv, qkv, bias)


def _cis_pow(zr, zi, e):
    mag = jnp.exp(zr * e)
    return mag * jnp.cos(zi * e), mag * jnp.sin(zi * e)


def _ssm_prep_group(g, lre_r, lim_r, ldt_r, lre_c, lim_c, ldt_c, btr, bti, ctr, cti, dcol):
    t, hch, p = SSM_CHUNK, SSM_GROUP_CH, SSM_STATE
    th, p2 = t * hch, 2 * p
    hi_prec = lax.Precision.HIGHEST
    ar, ai = lre_r[g], lim_r[g]
    dt = jnp.exp(ldt_r[g])
    zr, zi = ar * dt, ai * dt
    lbr, lbi = _cis_pow(zr, zi, 1.0)
    nr = lbr - 1.0
    den = ar * ar + ai * ai
    cr = (nr * ar + lbi * ai) / den
    ci = (lbi * ar - nr * ai) / den
    b_r, b_i = btr[g], bti[g]
    bbr = cr * b_r - ci * b_i
    bbi = cr * b_i + ci * b_r
    row = lax.broadcasted_iota(jnp.int32, (th, p2), 0)
    lane = lax.broadcasted_iota(jnp.int32, (th, p2), 1)
    j = row // hch
    e_w = jnp.where(lane < p, t - 1 - j, j).astype(F32)
    pr, pi = _cis_pow(zr, zi, e_w)
    bt_r = jnp.concatenate([bbr] * t, axis=0)
    bt_i = jnp.concatenate([bbi] * t, axis=0)
    w_mat = jnp.concatenate([pr * bt_r - pi * bt_i, pr * bt_i + pi * bt_r], axis=1)
    l_r, l_i = _cis_pow(zr, zi, float(t))
    lam = jnp.concatenate([l_r, l_i], axis=1)
    arc, aic = lre_c[g], lim_c[g]
    dtc = jnp.exp(ldt_c[g])
    zrc, zic = arc * dtc, aic * dtc
    rowc = lax.broadcasted_iota(jnp.int32, (p2, th), 0)
    lanec = lax.broadcasted_iota(jnp.int32, (p2, th), 1)
    nn = lanec // hch
    e_k = jnp.where(rowc < p, nn, t - 1 - nn).astype(F32)
    qr, qi = _cis_pow(zrc, zic, e_k)
    tile = (lax.broadcasted_iota(jnp.int32, (hch, th), 1) % hch
            == lax.broadcasted_iota(jnp.int32, (hch, th), 0)).astype(F32)
    c_r = jnp.dot(ctr[g], tile, preferred_element_type=F32, precision=hi_prec)
    c_i = jnp.dot(cti[g], tile, preferred_element_type=F32, precision=hi_prec)
    e_r = c_r * qr - c_i * qi
    e_i = c_r * qi + c_i * qr
    lbrc, lbic = _cis_pow(zrc, zic, 1.0)
    v_mat = jnp.concatenate([e_r * lbrc - e_i * lbic, -(e_r * lbic + e_i * lbrc)], axis=0)
    fwd_lane = lax.broadcasted_iota(jnp.int32, (hch, p2), 1) < p
    rhs = jnp.concatenate([e_r, e_i], axis=0)
    lhs0 = jnp.concatenate([jnp.where(fwd_lane, bbr, 0.0), -jnp.where(fwd_lane, bbi, 0.0)], axis=1)
    lhs1 = jnp.concatenate([jnp.where(fwd_lane, 0.0, bbr), -jnp.where(fwd_lane, 0.0, bbi)], axis=1)
    k0 = jnp.dot(lhs0, rhs, preferred_element_type=F32, precision=hi_prec)
    k1 = jnp.dot(lhs1, rhs, preferred_element_type=F32, precision=hi_prec)
    lane_m = lax.broadcasted_iota(jnp.int32, (hch, th), 1)
    row_m = lax.broadcasted_iota(jnp.int32, (hch, th), 0)
    d_g = dcol[g]
    blocks = []
    for jj in range(t):
        a = pltpu.roll(k0, hch * jj, 1) if jj else k0
        a = jnp.where(lane_m >= hch * jj, a, 0.0)
        sh = (hch * (jj + 1)) % th
        b = pltpu.roll(k1, sh, 1) if sh else k1
        b = jnp.where(lane_m < hch * (jj + 1), b, 0.0)
        dd = jnp.where(lane_m == hch * jj + row_m, d_g, 0.0)
        blocks.append(a + b + dd)
    m_mat = jnp.concatenate(blocks, axis=0)
    return m_mat, w_mat, v_mat, lam


def _ssm_prep_kernel(lre_r, lim_r, ldt_r, lre_c, lim_c, ldt_c, btr, bti, ctr, cti, dcol,
                     m_ref, w_ref, v_ref, lam_ref, *, pairs, nb):
    ins = (lre_r, lim_r, ldt_r, lre_c, lim_c, ldt_c, btr, bti, ctr, cti, dcol)
    first = lax.broadcasted_iota(jnp.int32, (2 * nb, 4 * SSM_STATE), 0) < nb

    def body(q, carry):
        lams = []
        for s in range(2):
            g = 2 * q + s
            m_mat, w_mat, v_mat, lam = _ssm_prep_group(g, *ins)
            m_ref[g] = m_mat.astype(m_ref.dtype)
            w_ref[g] = w_mat.astype(w_ref.dtype)
            v_ref[g] = v_mat.astype(v_ref.dtype)
            lams.append(jnp.broadcast_to(lam, (2 * nb, 4 * SSM_STATE)))
        lam_ref[q] = jnp.where(first, lams[0], lams[1])
        return carry

    lax.fori_loop(0, pairs, body, 0)


def _ssm_prep(lre, lim, ldt, bre, bim, cre, cim, d, nb):
    _, g, p = lre.shape
    hch, t = SSM_GROUP_CH, SSM_CHUNK
    th = t * hch
    cat = lambda a: jnp.concatenate([a[0], a[1]], axis=-1)
    lre2, lim2 = cat(lre), cat(lim)
    ldt2 = jnp.repeat(ldt.T, p, axis=1)
    bt = lambda a: jnp.transpose(a, (1, 3, 0, 2)).reshape(g, hch, 2 * p)
    ct = lambda a: jnp.transpose(a, (1, 0, 3, 2)).reshape(g, 2 * p, hch)
    gp = _pick(g, (8, 4, 2))
    rowv = pl.BlockSpec((gp, 1, 2 * p), lambda i: (i, 0, 0))
    colv = pl.BlockSpec((gp, 2 * p, 1), lambda i: (i, 0, 0))
    mat = pl.BlockSpec((gp, th, th), lambda i: (i, 0, 0))
    return pl.pallas_call(
        functools.partial(_ssm_prep_kernel, pairs=gp // 2, nb=nb),
        grid=(g // gp,),
        in_specs=[rowv, rowv, rowv, colv, colv, colv,
                  pl.BlockSpec((gp, hch, 2 * p), lambda i: (i, 0, 0)),
                  pl.BlockSpec((gp, hch, 2 * p), lambda i: (i, 0, 0)),
                  pl.BlockSpec((gp, 2 * p, hch), lambda i: (i, 0, 0)),
                  pl.BlockSpec((gp, 2 * p, hch), lambda i: (i, 0, 0)),
                  pl.BlockSpec((gp, hch, 1), lambda i: (i, 0, 0))],
        out_specs=[mat, mat, mat,
                   pl.BlockSpec((gp // 2, 2 * nb, 4 * p), lambda i: (i, 0, 0))],
        out_shape=[jax.ShapeDtypeStruct((g, th, th), BF16)] * 3
        + [jax.ShapeDtypeStruct((g // 2, 2 * nb, 4 * p), F32)],
        compiler_params=_cparams(1, 32),
        name="ssm_prep",
    )(lre2.reshape(g, 1, 2 * p), lim2.reshape(g, 1, 2 * p), ldt2.reshape(g, 1, 2 * p),
      lre2.reshape(g, 2 * p, 1), lim2.reshape(g, 2 * p, 1), ldt2.reshape(g, 2 * p, 1),
      bt(bre), bt(bim), ct(cre), ct(cim), d.reshape(g, hch, 1))


def _ssm_kernel(u_ref, m_ref, w_ref, v_ref, lam_ref, yext_ref, y_ref, yin_s, s_s, xf_s, xb_s,
                tok_s, *, pb, nchunk, nb):
    slab = 2 * nb
    rows = nchunk * slab
    p2 = 2 * SSM_STATE
    half = SSM_CHUNK * SSM_GROUP_CH // 2
    first = (lax.broadcasted_iota(jnp.int32, (rows, 1), 0) // nb) % 2 == 0

    def pair_dot(lhs, mats, q):
        ya = jnp.dot(lhs, mats[2 * q], preferred_element_type=F32)
        yb = jnp.dot(lhs, mats[2 * q + 1], preferred_element_type=F32)
        return jnp.where(first, ya, yb)

    for q in range(pb):
        u = u_ref[q]
        y0 = pair_dot(u, m_ref, q)
        yin_s[q, 0] = y0[:, :half]
        yin_s[q, 1] = y0[:, half:]
        s_s[q] = pair_dot(u, w_ref, q)

    fwd = lax.broadcasted_iota(jnp.int32, (slab, p2), 1) < SSM_STATE
    lam_r = [lam_ref[q, :, 0:p2] for q in range(pb)]
    lam_i = [lam_ref[q, :, p2:2 * p2] for q in range(pb)]

    def step(k, carry):
        kf = pl.multiple_of(k * slab, slab)
        kb = pl.multiple_of((nchunk - 1 - k) * slab, slab)
        out = []
        for q in range(pb):
            xr, xi = carry[2 * q], carry[2 * q + 1]
            xf_s[q, pl.ds(kf, slab), 0:p2] = xr
            xf_s[q, pl.ds(kf, slab), p2:2 * p2] = xi
            xb_s[q, pl.ds(kb, slab), 0:p2] = xr
            xb_s[q, pl.ds(kb, slab), p2:2 * p2] = xi
            sr = jnp.where(fwd, s_s[q, pl.ds(kf, slab), 0:p2], s_s[q, pl.ds(kb, slab), 0:p2])
            si = jnp.where(fwd, s_s[q, pl.ds(kf, slab), p2:2 * p2],
                           s_s[q, pl.ds(kb, slab), p2:2 * p2])
            out.append(lam_r[q] * xr - lam_i[q] * xi + sr)
            out.append(lam_r[q] * xi + lam_i[q] * xr + si)
        return tuple(out)

    zero = jnp.zeros((slab, p2), F32)
    lax.fori_loop(0, nchunk, step, (zero,) * (2 * pb))

    fwd_all = lax.broadcasted_iota(jnp.int32, (rows, 2 * p2), 1) % p2 < SSM_STATE
    for q in range(pb):
        x = jnp.where(fwd_all, xf_s[q], xb_s[q]).astype(BF16)
        yv = pair_dot(x, v_ref, q)
        yin_s[q, 0] = yin_s[q, 0] + yv[:, :half]
        yin_s[q, 1] = yin_s[q, 1] + yv[:, half:]

    hch, tch = SSM_GROUP_CH, SSM_CHUNK
    lanes = 2 * pb * hch
    lane_grp = lax.broadcasted_iota(jnp.int32, (nchunk, lanes), 1) // hch
    for b in range(nb):
        for tq in range(tch):
            col = (tq * hch) // lanes
            acc = None
            for q in range(pb):
                for g2 in range(2):
                    gl = 2 * q + g2
                    piece = yin_s[q, col, pl.ds(g2 * nb + b, nchunk, stride=slab), :]
                    shift = (hch * gl - (tq * hch) % lanes) % lanes
                    if shift:
                        piece = pltpu.roll(piece, shift, 1)
                    acc = piece if acc is None else jnp.where(lane_grp == gl, piece, acc)
            tok_s[pl.ds(b * nchunk * tch + tq, nchunk, stride=tch), :] = acc
    y_ref[...] = jax.nn.gelu(tok_s[...] + yext_ref[...]).astype(y_ref.dtype)


def _ssm_core(u2, m_mat, w_mat, v_mat, lam, yext, nchunk, nb):
    npair, rows, th = u2.shape
    m, dm = yext.shape
    pb = 4
    assert npair % pb == 0 and 2 * pb * SSM_GROUP_CH == 128 and th == 256
    p4 = 4 * SSM_STATE
    big = pl.BlockSpec((pb, rows, th), lambda i: (i, 0, 0))
    mat = pl.BlockSpec((2 * pb, th, th), lambda i: (i, 0, 0))
    tok = pl.BlockSpec((m, 128), lambda i: (0, i))
    return pl.pallas_call(
        functools.partial(_ssm_kernel, pb=pb, nchunk=nchunk, nb=nb),
        grid=(npair // pb,),
        in_specs=[big, mat, mat, mat, pl.BlockSpec((pb, 2 * nb, p4), lambda i: (i, 0, 0)), tok],
        out_specs=tok,
        out_shape=jax.ShapeDtypeStruct((m, dm), BF16),
        scratch_shapes=[pltpu.VMEM((pb, 2, rows, th // 2), F32), pltpu.VMEM((pb, rows, p4), F32),
                        pltpu.VMEM((pb, rows, p4), F32), pltpu.VMEM((pb, rows, p4), F32),
                        pltpu.VMEM((m, 128), F32)],
        compiler_params=_cparams(1, 56),
        name="ssm_core",
    )(u2, m_mat, w_mat, v_mat, lam, yext)


def _s5_mixer(h, w_in, lre, lim, ldt, bre, bim, cre, cim, d, w_glu, batch, seq):
    m, dm = h.shape
    hch, t = SSM_GROUP_CH, SSM_CHUNK
    g = dm // hch
    nchunk = seq // t
    u = _mm(h, w_in, F32, tm=_pick(m, (2048, 1024, 512, 256, 128)), tn=_pick(dm, (256, 128)))
    yb = _s5_direction(u.reshape(batch, seq, dm), lre[1], lim[1], ldt[1], bre[1], bim[1],
                       cre[1], cim[1], reverse=True)
    to_pairs = lambda a: (a.reshape(batch, nchunk, t, g // 2, 2, hch)
                          .transpose(3, 1, 4, 0, 2, 5)
                          .reshape(g // 2, nchunk * 2 * batch, t * hch))
    m_mat, w_mat, v_mat, lam = _ssm_prep(lre, lim, ldt, bre.at[1].set(0.0), bim.at[1].set(0.0),
                                         cre, cim, d, batch)
    yg = _ssm_core(to_pairs(u.astype(BF16)), m_mat, w_mat, v_mat, lam, yb.reshape(m, dm),
                   nchunk, batch)
    return _glu(yg, w_glu, tm=_pick(m, (2048, 1024, 512, 256, 128)), tn=_pick(dm, (256, 128)))


def _gelu_kernel(y_ref, o_ref):
    o_ref[...] = jax.nn.gelu(y_ref[...]).astype(o_ref.dtype)


def _gelu_cast(y):
    m, d = y.shape
    tm = _pick(m, (256, 128, 64, 32, 16, 8))
    row = pl.BlockSpec((tm, d), lambda i: (i, 0))
    return pl.pallas_call(
        _gelu_kernel, grid=(m // tm,), in_specs=[row], out_specs=row,
        out_shape=jax.ShapeDtypeStruct((m, d), BF16),
        compiler_params=_cparams(1, 32), name="gelu_cast",
    )(y)


def _recurrence_combine(left, right):
    a_l, b_l = left
    a_r, b_r = right
    return a_r * a_l, a_r * b_l + b_r


def _s5_direction(u, lam_re, lam_im, log_dt, b_re, b_im, c_re, c_im, reverse):
    bsz, seq, dm = u.shape
    g, p = lam_re.shape
    ug = u.reshape(bsz, seq, g, dm // g)
    lam = lax.complex(lam_re, lam_im)
    dt = jnp.exp(log_dt)[:, None]
    lam_bar = jnp.exp(lam * dt)
    b_bar = ((lam_bar - 1.0) / lam)[..., None] * lax.complex(b_re, b_im)
    c_mat = lax.complex(c_re, c_im)
    bu = jnp.einsum('blgh,gph->blgp', ug, b_bar)
    a = jnp.broadcast_to(lam_bar, (1, seq, g, p))
    _, states = lax.associative_scan(_recurrence_combine, (a, bu), axis=1, reverse=reverse)
    return jnp.real(jnp.einsum('blgp,ghp->blgh', states, c_mat)).reshape(bsz, seq, dm)


def kernel(x, rel_bias, pre_mix_norm, post_mix_norm, pre_ffn_norm, post_ffn_norm, attn_wqkv, attn_sink, attn_wo, ssm_w_in, ssm_lambda_re, ssm_lambda_im, ssm_log_dt, ssm_b_re, ssm_b_im, ssm_c_re, ssm_c_im, ssm_d, ssm_w_glu, ffn_w_gate, ffn_w_up, ffn_conv_w, ffn_conv_b, ffn_w_down):
    batch, seq, dm = x.shape
    depth = pre_mix_norm.shape[0]
    m = batch * seq
    heads = dm // HEAD_DIM
    kvh = (attn_wqkv.shape[2] // HEAD_DIM - heads) // 2
    dff = ffn_w_gate.shape[2]
    assert seq % ATTN_BLOCK == 0 and seq % SSM_CHUNK == 0 and (2 * batch) % 8 == 0
    tm_big = _pick(m, (2048, 1024, 512, 256, 128))
    tm_mid = _pick(m, (1024, 512, 256, 128))

    xf = x.reshape(m, dm)
    h = _norm_cast(xf, pre_mix_norm[0])
    bias = _attn_bias(rel_bias)
    for i in range(depth):
        j = i // 2
        if i % 2 == 0:
            qkv = _mm(h, attn_wqkv[j], BF16, tm=tm_big, tn=_pick(attn_wqkv.shape[2], (512, 256, 128)))
            o = _attention(qkv, bias, attn_sink[j], seq, heads, kvh)
            mix = _mm(o, attn_wo[j], F32, tm=tm_big, tn=_pick(dm, (512, 256, 128)))
        else:
            mix = _s5_mixer(h, ssm_w_in[j], ssm_lambda_re[j], ssm_lambda_im[j], ssm_log_dt[j],
                            ssm_b_re[j], ssm_b_im[j], ssm_c_re[j], ssm_c_im[j], ssm_d[j],
                            ssm_w_glu[j], batch, seq)
        xf, h = _resid_norm(xf, mix, post_mix_norm[i], pre_ffn_norm[i])
        hid = _ffn_in(h, ffn_w_gate[i], ffn_w_up[i], ffn_conv_w[i], ffn_conv_b[i], seq,
                      tn=_pick(dff, (256, 128)))
        kc = dff // 2 if (dff // 2) % 128 == 0 else dff
        f = _mm(hid, ffn_w_down[i], F32, tm=tm_mid, tn=_pick(dm, (256, 128)), kc=kc, vmem_mib=60)
        g_next = pre_mix_norm[i + 1] if i + 1 < depth else None
        xf, h = _resid_norm(xf, f, post_ffn_norm[i], g_next)
    return xf.reshape(batch, seq, dm)
```

```python
import functools
import math

import jax
import jax.numpy as jnp
from jax import lax
from jax.experimental import pallas as pl
from jax.experimental.pallas import tpu as pltpu

F32 = jnp.float32
BF16 = jnp.bfloat16

HEAD_DIM = 128
ATTN_BLOCK = 128
NUM_BUCKETS = 32
SSM_GROUP_CH = 16
SSM_STATE = 64
SSM_CHUNK = 16
RMS_EPS = 1e-6
NEG_INF = -1e30

MIB = 1024 * 1024


def _cparams(n_grid_dims, vmem_mib):
    return pltpu.CompilerParams(
        dimension_semantics=("arbitrary",) * n_grid_dims,
        vmem_limit_bytes=vmem_mib * MIB,
    )


def _pick(n, prefs):
    for p in prefs:
        if p <= n and n % p == 0:
            return p
    return n


def _rms(x, g):
    return x * lax.rsqrt(jnp.mean(x * x, axis=-1, keepdims=True) + RMS_EPS) * g


def _norm_kernel(x_ref, g_ref, h_ref):
    h_ref[...] = _rms(x_ref[...], g_ref[...]).astype(h_ref.dtype)


def _norm_cast(x, g):
    m, d = x.shape
    tm = _pick(m, (256, 128, 64, 32, 16, 8))
    return pl.pallas_call(
        _norm_kernel,
        grid=(m // tm,),
        in_specs=[pl.BlockSpec((tm, d), lambda i: (i, 0)),
                  pl.BlockSpec((1, d), lambda i: (0, 0))],
        out_specs=pl.BlockSpec((tm, d), lambda i: (i, 0)),
        out_shape=jax.ShapeDtypeStruct((m, d), BF16),
        compiler_params=_cparams(1, 32),
        name="norm_cast",
    )(x, g.reshape(1, d))


def _resid_norm_kernel(x_ref, m_ref, g1_ref, g2_ref, xo_ref, ho_ref):
    xn = x_ref[...] + _rms(m_ref[...], g1_ref[...])
    xo_ref[...] = xn
    ho_ref[...] = _rms(xn, g2_ref[...]).astype(ho_ref.dtype)


def _resid_kernel(x_ref, m_ref, g1_ref, xo_ref):
    xo_ref[...] = x_ref[...] + _rms(m_ref[...], g1_ref[...])


def _resid_norm(x, mix, g_post, g_next):
    m, d = x.shape
    tm = _pick(m, (128, 64, 32, 16, 8))
    row = pl.BlockSpec((tm, d), lambda i: (i, 0))
    vec = pl.BlockSpec((1, d), lambda i: (0, 0))
    if g_next is None:
        return pl.pallas_call(
            _resid_kernel, grid=(m // tm,),
            in_specs=[row, row, vec], out_specs=row,
            out_shape=jax.ShapeDtypeStruct((m, d), F32),
            compiler_params=_cparams(1, 32), name="resid",
        )(x, mix, g_post.reshape(1, d)), None
    return pl.pallas_call(
        _resid_norm_kernel, grid=(m // tm,),
        in_specs=[row, row, vec, vec], out_specs=[row, row],
        out_shape=[jax.ShapeDtypeStruct((m, d), F32), jax.ShapeDtypeStruct((m, d), BF16)],
        compiler_params=_cparams(1, 32), name="resid_norm",
    )(x, mix, g_post.reshape(1, d), g_next.reshape(1, d))


def _fetch_row_panel(a_hbm, a_ref, sem):
    @pl.when(pl.program_id(1) == 0)
    def _():
        tm = a_ref.shape[0]
        cp = pltpu.make_async_copy(
            a_hbm.at[pl.ds(pl.multiple_of(pl.program_id(0) * tm, tm), tm), :], a_ref, sem)
        cp.start()
        cp.wait()


def _panel_scratch(tm, k):
    return [pltpu.VMEM((tm, k), BF16), pltpu.SemaphoreType.DMA(())]


def _mm_kernel(a_hbm, w_ref, o_ref, a_ref, sem, *, kc):
    _fetch_row_panel(a_hbm, a_ref, sem)
    k = a_ref.shape[1]
    acc = None
    for k0 in range(0, k, kc):
        part = jnp.dot(a_ref[:, k0:k0 + kc], w_ref[k0:k0 + kc, :].astype(BF16),
                       preferred_element_type=F32)
        acc = part if acc is None else acc + part
    o_ref[...] = acc.astype(o_ref.dtype)


def _mm(a, w, out_dtype, tm, tn, kc=None, vmem_mib=56):
    m, k = a.shape
    n = w.shape[1]
    kc = k if kc is None else kc
    return pl.pallas_call(
        functools.partial(_mm_kernel, kc=kc),
        grid=(m // tm, n // tn),
        in_specs=[pl.BlockSpec(memory_space=pl.ANY),
                  pl.BlockSpec((k, tn), lambda i, j: (0, j))],
        out_specs=pl.BlockSpec((tm, tn), lambda i, j: (i, j)),
        out_shape=jax.ShapeDtypeStruct((m, n), out_dtype),
        scratch_shapes=_panel_scratch(tm, k),
        compiler_params=_cparams(2, vmem_mib),
        name="mm",
    )(a, w)


def _glu_kernel(a_hbm, wa_ref, wb_ref, o_ref, a_ref, sem):
    _fetch_row_panel(a_hbm, a_ref, sem)
    a = a_ref[...]
    ya = jnp.dot(a, wa_ref[...].astype(BF16), preferred_element_type=F32)
    yb = jnp.dot(a, wb_ref[...].astype(BF16), preferred_element_type=F32)
    o_ref[...] = (ya * jax.nn.sigmoid(yb)).astype(o_ref.dtype)


def _glu(a, w, tm, tn):
    m, k = a.shape
    n = w.shape[1] // 2
    nj = n // tn
    return pl.pallas_call(
        _glu_kernel,
        grid=(m // tm, nj),
        in_specs=[pl.BlockSpec(memory_space=pl.ANY),
                  pl.BlockSpec((k, tn), lambda i, j: (0, j)),
                  pl.BlockSpec((k, tn), lambda i, j: (0, j + nj))],
        out_specs=pl.BlockSpec((tm, tn), lambda i, j: (i, j)),
        out_shape=jax.ShapeDtypeStruct((m, n), F32),
        scratch_shapes=_panel_scratch(tm, k),
        compiler_params=_cparams(2, 56),
        name="glu",
    )(a, w, w)


def _ffn_in_kernel(a_hbm, wg_ref, wu_ref, cw_ref, cb_ref, o_ref, a_ref, sem):
    _fetch_row_panel(a_hbm, a_ref, sem)
    a = a_ref[...]
    g = jnp.dot(a, wg_ref[...].astype(BF16), preferred_element_type=F32)
    u = jnp.dot(a, wu_ref[...].astype(BF16), preferred_element_type=F32)
    rows = g.shape[0]
    row = lax.broadcasted_iota(jnp.int32, (rows, 1), 0)
    g_prev = jnp.where(row == 0, 0.0, pltpu.roll(g, 1, 0))
    g_next = jnp.where(row == rows - 1, 0.0, pltpu.roll(g, rows - 1, 0))
    gc = cw_ref[0:1, :] * g_prev + cw_ref[1:2, :] * g + cw_ref[2:3, :] * g_next + cb_ref[...]
    o_ref[...] = (gc * jax.nn.sigmoid(gc) * u).astype(o_ref.dtype)


def _ffn_in(h, w_gate, w_up, conv_w, conv_b, seq, tn):
    m, k = h.shape
    f = w_gate.shape[1]
    return pl.pallas_call(
        _ffn_in_kernel,
        grid=(m // seq, f // tn),
        in_specs=[pl.BlockSpec(memory_space=pl.ANY),
                  pl.BlockSpec((k, tn), lambda i, j: (0, j)),
                  pl.BlockSpec((k, tn), lambda i, j: (0, j)),
                  pl.BlockSpec((3, tn), lambda i, j: (0, j)),
                  pl.BlockSpec((1, tn), lambda i, j: (0, j))],
        out_specs=pl.BlockSpec((seq, tn), lambda i, j: (i, j)),
        out_shape=jax.ShapeDtypeStruct((m, f), BF16),
        scratch_shapes=_panel_scratch(seq, k),
        compiler_params=_cparams(2, 56),
        name="ffn_in",
    )(h, w_gate, w_up, conv_w, conv_b.reshape(1, f))


def _t5_bucket(rel):
    half = NUM_BUCKETS // 2
    max_exact = half // 2
    base = jnp.where(rel > 0, half, 0)
    n = jnp.abs(rel)
    nf = jnp.maximum(n, 1).astype(F32)
    large = max_exact + (jnp.log(nf / max_exact) / math.log(ATTN_BLOCK / max_exact)
                         * (half - max_exact)).astype(jnp.int32)
    large = jnp.minimum(large, half - 1)
    return base + jnp.where(n < max_exact, n, large)


def _bias_kernel(bucket_ref, inwin_ref, rbt_ref, o_ref):
    nb = rbt_ref.shape[1]
    lanes = bucket_ref.shape[1]
    onehot = (lax.broadcasted_iota(jnp.int32, (nb, lanes), 0) == bucket_ref[...]).astype(F32)
    bias = jnp.dot(rbt_ref[...], onehot, preferred_element_type=F32,
                   precision=lax.Precision.HIGHEST)
    o_ref[...] = jnp.where(inwin_ref[...] > 0, bias, NEG_INF)


def _attn_bias(rel_bias):
    nb, heads = rel_bias.shape
    blk = ATTN_BLOCK
    q_idx = jnp.arange(blk)[:, None]
    k_idx = jnp.arange(3 * blk)[None, :]
    rel = k_idx - blk - q_idx
    bucket = _t5_bucket(rel).reshape(1, 3 * blk * blk).astype(jnp.int32)
    inwin = (jnp.abs(rel) <= blk).astype(jnp.int32).reshape(1, 3 * blk * blk)
    tl = 4096
    out = pl.pallas_call(
        _bias_kernel,
        grid=(3 * blk * blk // tl,),
        in_specs=[pl.BlockSpec((1, tl), lambda i: (0, i)),
                  pl.BlockSpec((1, tl), lambda i: (0, i)),
                  pl.BlockSpec((heads, nb), lambda i: (0, 0))],
        out_specs=pl.BlockSpec((heads, tl), lambda i: (0, i)),
        out_shape=jax.ShapeDtypeStruct((heads, 3 * blk * blk), F32),
        compiler_params=_cparams(1, 32),
        name="attn_bias",
    )(bucket, inwin, rel_bias.T)
    return out.reshape(heads, blk, 3 * blk)


def _attn_kernel(sink_ref, q_ref, kp_ref, ko_ref, kn_ref, vp_ref, vo_ref, vn_ref, bias_ref,
                 o_ref, *, nblk, kvh, grp):
    blk, hd = ATTN_BLOCK, HEAD_DIM
    n = pl.program_id(0) % nblk
    col = lax.broadcasted_iota(jnp.int32, (1, 3 * blk), 1)
    key_pos = (n - 1) * blk + col
    edge = jnp.where((key_pos >= 0) & (key_pos < nblk * blk), 0.0, NEG_INF)
    scale = hd ** -0.5
    heads = [[kh * grp + g for g in range(grp)] for kh in range(kvh)]
    scores = []
    for kh in range(kvh):
        ks = slice(kh * hd, (kh + 1) * hd)
        k = jnp.concatenate([kp_ref[:, ks], ko_ref[:, ks], kn_ref[:, ks]], axis=0)
        q = jnp.concatenate([q_ref[:, h * hd:(h + 1) * hd] for h in heads[kh]], axis=0)
        s = lax.dot_general(q, k, (((1,), (1,)), ((), ())), preferred_element_type=F32) * scale
        scores.append(s + bias_ref[kh * grp:(kh + 1) * grp].reshape(grp * blk, 3 * blk) + edge)
    probs, denoms = [], []
    for kh in range(kvh):
        s = scores[kh]
        sink = jnp.concatenate([jnp.full((blk, 1), sink_ref[h], F32) for h in heads[kh]], axis=0)
        mx = jnp.maximum(jnp.max(s, axis=-1, keepdims=True), sink)
        p = jnp.exp(s - mx)
        denoms.append(jnp.sum(p, axis=-1, keepdims=True) + jnp.exp(sink - mx))
        probs.append(p.astype(BF16))
    for kh in range(kvh):
        ks = slice(kh * hd, (kh + 1) * hd)
        v = jnp.concatenate([vp_ref[:, ks], vo_ref[:, ks], vn_ref[:, ks]], axis=0)
        o = jnp.dot(probs[kh], v, preferred_element_type=F32) / denoms[kh]
        for g, h in enumerate(heads[kh]):
            o_ref[:, h * hd:(h + 1) * hd] = o[g * blk:(g + 1) * blk].astype(o_ref.dtype)


def _attention(qkv, bias, sink, seq, heads, kvh):
    m = qkv.shape[0]
    blk, hd = ATTN_BLOCK, HEAD_DIM
    nblk = seq // blk
    grp = heads // kvh
    qw, kw = heads * hd, kvh * hd
    kcol, vcol = qw // kw, qw // kw + 1

    def prev(i):
        return jnp.where(i % nblk == 0, i, i - 1)

    def nxt(i):
        return jnp.where(i % nblk == nblk - 1, i, i + 1)

    kv = lambda rowf, colb: pl.BlockSpec((blk, kw), lambda i: (rowf(i), colb))
    same = lambda i: i
    return pl.pallas_call(
        functools.partial(_attn_kernel, nblk=nblk, kvh=kvh, grp=grp),
        grid=(m // blk,),
        in_specs=[pl.BlockSpec(memory_space=pltpu.SMEM),
                  pl.BlockSpec((blk, qw), lambda i: (i, 0)),
                  kv(prev, kcol), kv(same, kcol), kv(nxt, kcol),
                  kv(prev, vcol), kv(same, vcol), kv(nxt, vcol),
                  pl.BlockSpec((heads, blk, 3 * blk), lambda i: (0, 0, 0))],
        out_specs=pl.BlockSpec((blk, qw), lambda i: (i, 0)),
        out_shape=jax.ShapeDtypeStruct((m, qw), BF16),
        compiler_params=_cparams(1, 48),
        name="attention",
    )(sink, qkv, qkv, qkv, qkv, qkv, qkv, qkv, bias)


def _cexp(zr, zi):
    mag = jnp.exp(zr)
    return mag * jnp.cos(zi), mag * jnp.sin(zi)


def _cpow_int(br, bi, e, nbits):
    res_r = res_i = None
    for bit in range(nbits):
        on = ((e >> bit) & 1) == 1
        fr = jnp.where(on, br, 1.0)
        fi = jnp.where(on, bi, 0.0)
        if res_r is None:
            res_r, res_i = fr, fi
        else:
            res_r, res_i = res_r * fr - res_i * fi, res_r * fi + res_i * fr
        if bit + 1 < nbits:
            br, bi = br * br - bi * bi, 2.0 * br * bi
    return res_r, res_i


def _ssm_prep_group(g, lre_r, lim_r, ldt_r, lre_c, lim_c, ldt_c, btr, bti, ctr, cti, dcol):
    t, hch, p = SSM_CHUNK, SSM_GROUP_CH, SSM_STATE
    th, p2 = t * hch, 2 * p
    hi_prec = lax.Precision.HIGHEST
    ar, ai = lre_r[g], lim_r[g]
    dt = jnp.exp(ldt_r[g])
    zr, zi = ar * dt, ai * dt
    lbr, lbi = _cexp(zr, zi)
    nr = lbr - 1.0
    den = ar * ar + ai * ai
    cr = (nr * ar + lbi * ai) / den
    ci = (lbi * ar - nr * ai) / den
    b_r, b_i = btr[g], bti[g]
    bbr = cr * b_r - ci * b_i
    bbi = cr * b_i + ci * b_r
    row = lax.broadcasted_iota(jnp.int32, (th, p2), 0)
    lane = lax.broadcasted_iota(jnp.int32, (th, p2), 1)
    j = row // hch
    nbits = (t - 1).bit_length()
    pr, pi = _cpow_int(lbr, lbi, jnp.where(lane < p, t - 1 - j, j), nbits)
    bt_r = jnp.concatenate([bbr] * t, axis=0)
    bt_i = jnp.concatenate([bbi] * t, axis=0)
    w_mat = jnp.concatenate([pr * bt_r - pi * bt_i, pr * bt_i + pi * bt_r], axis=1)
    l_r, l_i = lbr, lbi
    for _ in range(nbits):
        l_r, l_i = l_r * l_r - l_i * l_i, 2.0 * l_r * l_i
    lam = jnp.concatenate([l_r, l_i], axis=1)
    arc, aic = lre_c[g], lim_c[g]
    dtc = jnp.exp(ldt_c[g])
    zrc, zic = arc * dtc, aic * dtc
    rowc = lax.broadcasted_iota(jnp.int32, (p2, th), 0)
    lanec = lax.broadcasted_iota(jnp.int32, (p2, th), 1)
    nn = lanec // hch
    lbrc, lbic = _cexp(zrc, zic)
    qr, qi = _cpow_int(lbrc, lbic, jnp.where(rowc < p, nn, t - 1 - nn), nbits)
    tile = (lax.broadcasted_iota(jnp.int32, (hch, th), 1) % hch
            == lax.broadcasted_iota(jnp.int32, (hch, th), 0)).astype(F32)
    c_r = jnp.dot(ctr[g], tile, preferred_element_type=F32, precision=hi_prec)
    c_i = jnp.dot(cti[g], tile, preferred_element_type=F32, precision=hi_prec)
    e_r = c_r * qr - c_i * qi
    e_i = c_r * qi + c_i * qr
    v_mat = jnp.concatenate([e_r * lbrc - e_i * lbic, -(e_r * lbic + e_i * lbrc)], axis=0)
    fwd_lane = lax.broadcasted_iota(jnp.int32, (hch, p2), 1) < p
    rhs = jnp.concatenate([e_r, e_i], axis=0)
    lhs0 = jnp.concatenate([jnp.where(fwd_lane, bbr, 0.0), -jnp.where(fwd_lane, bbi, 0.0)], axis=1)
    lhs1 = jnp.concatenate([jnp.where(fwd_lane, 0.0, bbr), -jnp.where(fwd_lane, 0.0, bbi)], axis=1)
    k0 = jnp.dot(lhs0, rhs, preferred_element_type=F32, precision=hi_prec)
    k1 = jnp.dot(lhs1, rhs, preferred_element_type=F32, precision=hi_prec)
    lane_m = lax.broadcasted_iota(jnp.int32, (hch, th), 1)
    row_m = lax.broadcasted_iota(jnp.int32, (hch, th), 0)
    d_g = dcol[g]
    blocks = []
    for jj in range(t):
        a = pltpu.roll(k0, hch * jj, 1) if jj else k0
        a = jnp.where(lane_m >= hch * jj, a, 0.0)
        sh = (hch * (jj + 1)) % th
        b = pltpu.roll(k1, sh, 1) if sh else k1
        b = jnp.where(lane_m < hch * (jj + 1), b, 0.0)
        dd = jnp.where(lane_m == hch * jj + row_m, d_g, 0.0)
        blocks.append(a + b + dd)
    m_mat = jnp.concatenate(blocks, axis=0)
    return m_mat, w_mat, v_mat, lam


def _ssm_prep_kernel(lre_r, lim_r, ldt_r, lre_c, lim_c, ldt_c, btr, bti, ctr, cti, dcol,
                     m_ref, w_ref, v_ref, lam_ref, *, pairs, nb):
    ins = (lre_r, lim_r, ldt_r, lre_c, lim_c, ldt_c, btr, bti, ctr, cti, dcol)
    first = lax.broadcasted_iota(jnp.int32, (2 * nb, 4 * SSM_STATE), 0) < nb

    def body(q, carry):
        lams = []
        for s in range(2):
            g = 2 * q + s
            m_mat, w_mat, v_mat, lam = _ssm_prep_group(g, *ins)
            m_ref[g] = m_mat.astype(m_ref.dtype)
            w_ref[g] = w_mat.astype(w_ref.dtype)
            v_ref[g] = v_mat.astype(v_ref.dtype)
            lams.append(jnp.broadcast_to(lam, (2 * nb, 4 * SSM_STATE)))
        lam_ref[q] = jnp.where(first, lams[0], lams[1])
        return carry

    lax.fori_loop(0, pairs, body, 0)


def _ssm_prep(lre, lim, ldt, bre, bim, cre, cim, d, nb):
    _, g, p = lre.shape
    hch, t = SSM_GROUP_CH, SSM_CHUNK
    th = t * hch
    cat = lambda a: jnp.concatenate([a[0], a[1]], axis=-1)
    lre2, lim2 = cat(lre), cat(lim)
    ldt2 = jnp.repeat(ldt.T, p, axis=1)
    bt = lambda a: jnp.transpose(a, (1, 3, 0, 2)).reshape(g, hch, 2 * p)
    ct = lambda a: jnp.transpose(a, (1, 0, 3, 2)).reshape(g, 2 * p, hch)
    gp = _pick(g, (8, 4, 2))
    rowv = pl.BlockSpec((gp, 1, 2 * p), lambda i: (i, 0, 0))
    colv = pl.BlockSpec((gp, 2 * p, 1), lambda i: (i, 0, 0))
    mat = pl.BlockSpec((gp, th, th), lambda i: (i, 0, 0))
    return pl.pallas_call(
        functools.partial(_ssm_prep_kernel, pairs=gp // 2, nb=nb),
        grid=(g // gp,),
        in_specs=[rowv, rowv, rowv, colv, colv, colv,
                  pl.BlockSpec((gp, hch, 2 * p), lambda i: (i, 0, 0)),
                  pl.BlockSpec((gp, hch, 2 * p), lambda i: (i, 0, 0)),
                  pl.BlockSpec((gp, 2 * p, hch), lambda i: (i, 0, 0)),
                  pl.BlockSpec((gp, 2 * p, hch), lambda i: (i, 0, 0)),
                  pl.BlockSpec((gp, hch, 1), lambda i: (i, 0, 0))],
        out_specs=[mat, mat, mat,
                   pl.BlockSpec((gp // 2, 2 * nb, 4 * p), lambda i: (i, 0, 0))],
        out_shape=[jax.ShapeDtypeStruct((g, th, th), BF16)] * 3
        + [jax.ShapeDtypeStruct((g // 2, 2 * nb, 4 * p), F32)],
        compiler_params=_cparams(1, 32),
        name="ssm_prep",
    )(lre2.reshape(g, 1, 2 * p), lim2.reshape(g, 1, 2 * p), ldt2.reshape(g, 1, 2 * p),
      lre2.reshape(g, 2 * p, 1), lim2.reshape(g, 2 * p, 1), ldt2.reshape(g, 2 * p, 1),
      bt(bre), bt(bim), ct(cre), ct(cim), d.reshape(g, hch, 1))


def _ssm_kernel(u_ref, m_ref, w_ref, v_ref, lam_ref, yext_ref, y_ref, yin_s, s_s, xf_s, xb_s,
                tok_s, u2_s, *, pb, nchunk, nb):
    slab = 2 * nb
    rows = nchunk * slab
    p2 = 2 * SSM_STATE
    hch, tch = SSM_GROUP_CH, SSM_CHUNK
    half = tch * hch // 2
    lanes = 2 * pb * hch
    slots = lanes // hch
    first = (lax.broadcasted_iota(jnp.int32, (rows, 1), 0) // nb) % 2 == 0
    lane_grp = lax.broadcasted_iota(jnp.int32, (nchunk, lanes), 1) // hch

    def pair_dot(lhs, mats, q):
        ya = jnp.dot(lhs, mats[2 * q], preferred_element_type=F32)
        yb = jnp.dot(lhs, mats[2 * q + 1], preferred_element_type=F32)
        return jnp.where(first, ya, yb)

    for b in range(nb):
        for tq in range(tch):
            tok_s[pl.ds((b * tch + tq) * nchunk, nchunk), :] = (
                u_ref[pl.ds(b * nchunk * tch + tq, nchunk, stride=tch), :])
    for q in range(pb):
        for g2 in range(2):
            gl = 2 * q + g2
            for b in range(nb):
                for col in range(2):
                    acc = None
                    for j in range(slots):
                        tq = col * slots + j
                        piece = tok_s[pl.ds((b * tch + tq) * nchunk, nchunk), :]
                        shift = (hch * j - hch * gl) % lanes
                        if shift:
                            piece = pltpu.roll(piece, shift, 1)
                        acc = piece if acc is None else jnp.where(lane_grp == j, piece, acc)
                    u2_s[q, col, pl.ds(g2 * nb + b, nchunk, stride=slab), :] = acc

    for q in range(pb):
        u = jnp.concatenate([u2_s[q, 0], u2_s[q, 1]], axis=1).astype(BF16)
        y0 = pair_dot(u, m_ref, q)
        yin_s[q, 0] = y0[:, :half]
        yin_s[q, 1] = y0[:, half:]
        s_s[q] = pair_dot(u, w_ref, q)

    fwd = lax.broadcasted_iota(jnp.int32, (slab, p2), 1) < SSM_STATE
    lam_r = [lam_ref[q, :, 0:p2] for q in range(pb)]
    lam_i = [lam_ref[q, :, p2:2 * p2] for q in range(pb)]

    def step(k, carry):
        kf = pl.multiple_of(k * slab, slab)
        kb = pl.multiple_of((nchunk - 1 - k) * slab, slab)
        out = []
        for q in range(pb):
            xr, xi = carry[2 * q], carry[2 * q + 1]
            xf_s[q, pl.ds(kf, slab), 0:p2] = xr
            xf_s[q, pl.ds(kf, slab), p2:2 * p2] = xi
            xb_s[q, pl.ds(kb, slab), 0:p2] = xr
            xb_s[q, pl.ds(kb, slab), p2:2 * p2] = xi
            sr = jnp.where(fwd, s_s[q, pl.ds(kf, slab), 0:p2], s_s[q, pl.ds(kb, slab), 0:p2])
            si = jnp.where(fwd, s_s[q, pl.ds(kf, slab), p2:2 * p2],
                           s_s[q, pl.ds(kb, slab), p2:2 * p2])
            out.append(lam_r[q] * xr - lam_i[q] * xi + sr)
            out.append(lam_r[q] * xi + lam_i[q] * xr + si)
        return tuple(out)

    zero = jnp.zeros((slab, p2), F32)
    lax.fori_loop(0, nchunk, step, (zero,) * (2 * pb))

    fwd_all = lax.broadcasted_iota(jnp.int32, (rows, 2 * p2), 1) % p2 < SSM_STATE
    for q in range(pb):
        x = jnp.where(fwd_all, xf_s[q], xb_s[q]).astype(BF16)
        yv = pair_dot(x, v_ref, q)
        yin_s[q, 0] = yin_s[q, 0] + yv[:, :half]
        yin_s[q, 1] = yin_s[q, 1] + yv[:, half:]

    for q in range(pb):
        for col in range(2):
            for r in range(slab):
                u2_s[q, col, pl.ds(r * nchunk, nchunk), :] = (
                    yin_s[q, col, pl.ds(r, nchunk, stride=slab), :])
    for b in range(nb):
        for tq in range(tch):
            col = (tq * hch) // lanes
            acc = None
            for q in range(pb):
                for g2 in range(2):
                    gl = 2 * q + g2
                    piece = u2_s[q, col, pl.ds((g2 * nb + b) * nchunk, nchunk), :]
                    shift = (hch * gl - (tq * hch) % lanes) % lanes
                    if shift:
                        piece = pltpu.roll(piece, shift, 1)
                    acc = piece if acc is None else jnp.where(lane_grp == gl, piece, acc)
            tok_s[pl.ds(b * nchunk * tch + tq, nchunk, stride=tch), :] = acc
    y_ref[...] = jax.nn.gelu(tok_s[...] + yext_ref[...]).astype(y_ref.dtype)


def _ssm_core(u, m_mat, w_mat, v_mat, lam, yext, nchunk, nb):
    m, dm = u.shape
    th = SSM_CHUNK * SSM_GROUP_CH
    npair = dm // (2 * SSM_GROUP_CH)
    rows = nchunk * 2 * nb
    pb = 4
    assert npair % pb == 0 and 2 * pb * SSM_GROUP_CH == 128 and th == 256
    p4 = 4 * SSM_STATE
    mat = pl.BlockSpec((2 * pb, th, th), lambda i: (i, 0, 0))
    tok = pl.BlockSpec((m, 128), lambda i: (0, i))
    return pl.pallas_call(
        functools.partial(_ssm_kernel, pb=pb, nchunk=nchunk, nb=nb),
        grid=(npair // pb,),
        in_specs=[tok, mat, mat, mat, pl.BlockSpec((pb, 2 * nb, p4), lambda i: (i, 0, 0)), tok],
        out_specs=tok,
        out_shape=jax.ShapeDtypeStruct((m, dm), BF16),
        scratch_shapes=[pltpu.VMEM((pb, 2, rows, th // 2), F32), pltpu.VMEM((pb, rows, p4), F32),
                        pltpu.VMEM((pb, rows, p4), F32), pltpu.VMEM((pb, rows, p4), F32),
                        pltpu.VMEM((m, 128), F32), pltpu.VMEM((pb, 2, rows, th // 2), F32)],
        compiler_params=_cparams(1, 60),
        name="ssm_core",
    )(u, m_mat, w_mat, v_mat, lam, yext)


def _s5_mixer(h, w_in, lre, lim, ldt, bre, bim, cre, cim, d, w_glu, batch, seq):
    m, dm = h.shape
    u = _mm(h, w_in, F32, tm=_pick(m, (2048, 1024, 512, 256, 128)), tn=_pick(dm, (256, 128)))
    yb = _s5_direction(u.reshape(batch, seq, dm), lre[1], lim[1], ldt[1], bre[1], bim[1],
                       cre[1], cim[1], reverse=True)
    m_mat, w_mat, v_mat, lam = _ssm_prep(lre, lim, ldt, bre.at[1].set(0.0), bim.at[1].set(0.0),
                                         cre, cim, d, batch)
    yg = _ssm_core(u, m_mat, w_mat, v_mat, lam, yb.reshape(m, dm), seq // SSM_CHUNK, batch)
    return _glu(yg, w_glu, tm=_pick(m, (2048, 1024, 512, 256, 128)), tn=_pick(dm, (256, 128)))


def _gelu_kernel(y_ref, o_ref):
    o_ref[...] = jax.nn.gelu(y_ref[...]).astype(o_ref.dtype)


def _gelu_cast(y):
    m, d = y.shape
    tm = _pick(m, (256, 128, 64, 32, 16, 8))
    row = pl.BlockSpec((tm, d), lambda i: (i, 0))
    return pl.pallas_call(
        _gelu_kernel, grid=(m // tm,), in_specs=[row], out_specs=row,
        out_shape=jax.ShapeDtypeStruct((m, d), BF16),
        compiler_params=_cparams(1, 32), name="gelu_cast",
    )(y)


def _recurrence_combine(left, right):
    a_l, b_l = left
    a_r, b_r = right
    return a_r * a_l, a_r * b_l + b_r


def _s5_direction(u, lam_re, lam_im, log_dt, b_re, b_im, c_re, c_im, reverse):
    bsz, seq, dm = u.shape
    g, p = lam_re.shape
    ug = u.reshape(bsz, seq, g, dm // g)
    lam = lax.complex(lam_re, lam_im)
    dt = jnp.exp(log_dt)[:, None]
    lam_bar = jnp.exp(lam * dt)
    b_bar = ((lam_bar - 1.0) / lam)[..., None] * lax.complex(b_re, b_im)
    c_mat = lax.complex(c_re, c_im)
    bu = jnp.einsum('blgh,gph->blgp', ug, b_bar)
    a = jnp.broadcast_to(lam_bar, (1, seq, g, p))
    _, states = lax.associative_scan(_recurrence_combine, (a, bu), axis=1, reverse=reverse)
    return jnp.real(jnp.einsum('blgp,ghp->blgh', states, c_mat)).reshape(bsz, seq, dm)


def kernel(x, rel_bias, pre_mix_norm, post_mix_norm, pre_ffn_norm, post_ffn_norm, attn_wqkv, attn_sink, attn_wo, ssm_w_in, ssm_lambda_re, ssm_lambda_im, ssm_log_dt, ssm_b_re, ssm_b_im, ssm_c_re, ssm_c_im, ssm_d, ssm_w_glu, ffn_w_gate, ffn_w_up, ffn_conv_w, ffn_conv_b, ffn_w_down):
    batch, seq, dm = x.shape
    depth = pre_mix_norm.shape[0]
    m = batch * seq
    heads = dm // HEAD_DIM
    kvh = (attn_wqkv.shape[2] // HEAD_DIM - heads) // 2
    dff = ffn_w_gate.shape[2]
    assert seq % ATTN_BLOCK == 0 and seq % SSM_CHUNK == 0 and (2 * batch) % 8 == 0
    tm_big = _pick(m, (2048, 1024, 512, 256, 128))
    tm_mid = _pick(m, (1024, 512, 256, 128))

    xf = x.reshape(m, dm)
    h = _norm_cast(xf, pre_mix_norm[0])
    bias = _attn_bias(rel_bias)
    for i in range(depth):
        j = i // 2
        if i % 2 == 0:
            qkv = _mm(h, attn_wqkv[j], BF16, tm=tm_big, tn=_pick(attn_wqkv.shape[2], (512, 256, 128)))
            o = _attention(qkv, bias, attn_sink[j], seq, heads, kvh)
            mix = _mm(o, attn_wo[j], F32, tm=tm_big, tn=_pick(dm, (512, 256, 128)))
        else:
            mix = _s5_mixer(h, ssm_w_in[j], ssm_lambda_re[j], ssm_lambda_im[j], ssm_log_dt[j],
                            ssm_b_re[j], ssm_b_im[j], ssm_c_re[j], ssm_c_im[j], ssm_d[j],
                            ssm_w_glu[j], batch, seq)
        xf, h = _resid_norm(xf, mix, post_mix_norm[i], pre_ffn_norm[i])
        hid = _ffn_in(h, ffn_w_gate[i], ffn_w_up[i], ffn_conv_w[i], ffn_conv_b[i], seq,
                      tn=_pick(dff, (256, 128)))
        kc = dff // 2 if (dff // 2) % 128 == 0 else dff
        f = _mm(hid, ffn_w_down[i], F32, tm=tm_mid, tn=_pick(dm, (256, 128)), kc=kc, vmem_mib=60)
        g_next = pre_mix_norm[i + 1] if i + 1 < depth else None
        xf, h = _resid_norm(xf, f, post_ffn_norm[i], g_next)
    return xf.reshape(batch, seq, dm)
```

```python
import functools
import math

import jax
import jax.numpy as jnp
from jax import lax
from jax.experimental import pallas as pl
from jax.experimental.pallas import tpu as pltpu

F32 = jnp.float32
BF16 = jnp.bfloat16

HEAD_DIM = 128
ATTN_BLOCK = 128
NUM_BUCKETS = 32
SSM_GROUP_CH = 16
SSM_STATE = 64
SSM_CHUNK = 16
RMS_EPS = 1e-6
NEG_INF = -1e30

MIB = 1024 * 1024
VMEM_STREAM_MIB = 32
VMEM_PANEL_MIB = 56
VMEM_MAX_MIB = 60


def _cparams(n_grid_dims, vmem_mib):
    return pltpu.CompilerParams(
        dimension_semantics=("arbitrary",) * n_grid_dims,
        vmem_limit_bytes=vmem_mib * MIB,
    )


def _pick(n, prefs):
    for p in prefs:
        if p <= n and n % p == 0:
            return p
    return n


def _rms(x, g):
    return x * lax.rsqrt(jnp.mean(x * x, axis=-1, keepdims=True) + RMS_EPS) * g


def _norm_kernel(x_ref, g_ref, h_ref):
    h_ref[...] = _rms(x_ref[...], g_ref[...]).astype(h_ref.dtype)


def _norm_cast(x, g):
    m, d = x.shape
    tm = _pick(m, (256, 128, 64, 32, 16, 8))
    return pl.pallas_call(
        _norm_kernel,
        grid=(m // tm,),
        in_specs=[pl.BlockSpec((tm, d), lambda i: (i, 0)),
                  pl.BlockSpec((1, d), lambda i: (0, 0))],
        out_specs=pl.BlockSpec((tm, d), lambda i: (i, 0)),
        out_shape=jax.ShapeDtypeStruct((m, d), BF16),
        compiler_params=_cparams(1, VMEM_STREAM_MIB),
        name="norm_cast",
    )(x, g.reshape(1, d))


def _resid_norm_kernel(x_ref, m_ref, g1_ref, g2_ref, xo_ref, ho_ref):
    xn = x_ref[...] + _rms(m_ref[...], g1_ref[...])
    xo_ref[...] = xn
    ho_ref[...] = _rms(xn, g2_ref[...]).astype(ho_ref.dtype)


def _resid_kernel(x_ref, m_ref, g1_ref, xo_ref):
    xo_ref[...] = x_ref[...] + _rms(m_ref[...], g1_ref[...])


def _resid_norm(x, mix, g_post, g_next):
    m, d = x.shape
    tm = _pick(m, (128, 64, 32, 16, 8))
    row = pl.BlockSpec((tm, d), lambda i: (i, 0))
    vec = pl.BlockSpec((1, d), lambda i: (0, 0))
    if g_next is None:
        return pl.pallas_call(
            _resid_kernel, grid=(m // tm,),
            in_specs=[row, row, vec], out_specs=row,
            out_shape=jax.ShapeDtypeStruct((m, d), F32),
            compiler_params=_cparams(1, VMEM_STREAM_MIB), name="resid",
        )(x, mix, g_post.reshape(1, d)), None
    return pl.pallas_call(
        _resid_norm_kernel, grid=(m // tm,),
        in_specs=[row, row, vec, vec], out_specs=[row, row],
        out_shape=[jax.ShapeDtypeStruct((m, d), F32), jax.ShapeDtypeStruct((m, d), BF16)],
        compiler_params=_cparams(1, VMEM_STREAM_MIB), name="resid_norm",
    )(x, mix, g_post.reshape(1, d), g_next.reshape(1, d))


def _fetch_row_panel(a_hbm, a_ref, sem):
    @pl.when(pl.program_id(1) == 0)
    def _():
        tm = a_ref.shape[0]
        cp = pltpu.make_async_copy(
            a_hbm.at[pl.ds(pl.multiple_of(pl.program_id(0) * tm, tm), tm), :], a_ref, sem)
        cp.start()
        cp.wait()


def _panel_scratch(tm, k):
    return [pltpu.VMEM((tm, k), BF16), pltpu.SemaphoreType.DMA(())]


def _mm_kernel(a_hbm, w_ref, o_ref, a_ref, sem, *, kc):
    _fetch_row_panel(a_hbm, a_ref, sem)
    k = a_ref.shape[1]
    acc = None
    for k0 in range(0, k, kc):
        part = jnp.dot(a_ref[:, k0:k0 + kc], w_ref[k0:k0 + kc, :].astype(BF16),
                       preferred_element_type=F32)
        acc = part if acc is None else acc + part
    o_ref[...] = acc.astype(o_ref.dtype)


def _mm(a, w, out_dtype, tm, tn, kc=None, vmem_mib=VMEM_PANEL_MIB):
    m, k = a.shape
    n = w.shape[1]
    kc = k if kc is None else kc
    return pl.pallas_call(
        functools.partial(_mm_kernel, kc=kc),
        grid=(m // tm, n // tn),
        in_specs=[pl.BlockSpec(memory_space=pl.ANY),
                  pl.BlockSpec((k, tn), lambda i, j: (0, j))],
        out_specs=pl.BlockSpec((tm, tn), lambda i, j: (i, j)),
        out_shape=jax.ShapeDtypeStruct((m, n), out_dtype),
        scratch_shapes=_panel_scratch(tm, k),
        compiler_params=_cparams(2, vmem_mib),
        name="mm",
    )(a, w)


def _glu_kernel(a_hbm, wa_ref, wb_ref, o_ref, a_ref, sem):
    _fetch_row_panel(a_hbm, a_ref, sem)
    a = a_ref[...]
    ya = jnp.dot(a, wa_ref[...].astype(BF16), preferred_element_type=F32)
    yb = jnp.dot(a, wb_ref[...].astype(BF16), preferred_element_type=F32)
    o_ref[...] = (ya * jax.nn.sigmoid(yb)).astype(o_ref.dtype)


def _glu(a, w, tm, tn):
    m, k = a.shape
    n = w.shape[1] // 2
    nj = n // tn
    return pl.pallas_call(
        _glu_kernel,
        grid=(m // tm, nj),
        in_specs=[pl.BlockSpec(memory_space=pl.ANY),
                  pl.BlockSpec((k, tn), lambda i, j: (0, j)),
                  pl.BlockSpec((k, tn), lambda i, j: (0, j + nj))],
        out_specs=pl.BlockSpec((tm, tn), lambda i, j: (i, j)),
        out_shape=jax.ShapeDtypeStruct((m, n), F32),
        scratch_shapes=_panel_scratch(tm, k),
        compiler_params=_cparams(2, VMEM_PANEL_MIB),
        name="glu",
    )(a, w, w)


def _ffn_in_kernel(a_hbm, wg_ref, wu_ref, cw_ref, cb_ref, o_ref, a_ref, sem):
    _fetch_row_panel(a_hbm, a_ref, sem)
    a = a_ref[...]
    g = jnp.dot(a, wg_ref[...].astype(BF16), preferred_element_type=F32)
    u = jnp.dot(a, wu_ref[...].astype(BF16), preferred_element_type=F32)
    rows = g.shape[0]
    row = lax.broadcasted_iota(jnp.int32, (rows, 1), 0)
    g_prev = jnp.where(row == 0, 0.0, pltpu.roll(g, 1, 0))
    g_next = jnp.where(row == rows - 1, 0.0, pltpu.roll(g, rows - 1, 0))
    gc = cw_ref[0:1, :] * g_prev + cw_ref[1:2, :] * g + cw_ref[2:3, :] * g_next + cb_ref[...]
    o_ref[...] = (gc * jax.nn.sigmoid(gc) * u).astype(o_ref.dtype)


def _ffn_in(h, w_gate, w_up, conv_w, conv_b, seq, tn):
    m, k = h.shape
    f = w_gate.shape[1]
    return pl.pallas_call(
        _ffn_in_kernel,
        grid=(m // seq, f // tn),
        in_specs=[pl.BlockSpec(memory_space=pl.ANY),
                  pl.BlockSpec((k, tn), lambda i, j: (0, j)),
                  pl.BlockSpec((k, tn), lambda i, j: (0, j)),
                  pl.BlockSpec((3, tn), lambda i, j: (0, j)),
                  pl.BlockSpec((1, tn), lambda i, j: (0, j))],
        out_specs=pl.BlockSpec((seq, tn), lambda i, j: (i, j)),
        out_shape=jax.ShapeDtypeStruct((m, f), BF16),
        scratch_shapes=_panel_scratch(seq, k),
        compiler_params=_cparams(2, VMEM_PANEL_MIB),
        name="ffn_in",
    )(h, w_gate, w_up, conv_w, conv_b.reshape(1, f))


def _t5_bucket(rel):
    half = NUM_BUCKETS // 2
    max_exact = half // 2
    base = jnp.where(rel > 0, half, 0)
    n = jnp.abs(rel)
    nf = jnp.maximum(n, 1).astype(F32)
    large = max_exact + (jnp.log(nf / max_exact) / math.log(ATTN_BLOCK / max_exact)
                         * (half - max_exact)).astype(jnp.int32)
    large = jnp.minimum(large, half - 1)
    return base + jnp.where(n < max_exact, n, large)


def _bias_kernel(bucket_ref, inwin_ref, rbt_ref, o_ref):
    nb = rbt_ref.shape[1]
    lanes = bucket_ref.shape[1]
    onehot = (lax.broadcasted_iota(jnp.int32, (nb, lanes), 0) == bucket_ref[...]).astype(F32)
    bias = jnp.dot(rbt_ref[...], onehot, preferred_element_type=F32,
                   precision=lax.Precision.HIGHEST)
    o_ref[...] = jnp.where(inwin_ref[...] > 0, bias, NEG_INF)


def _attn_bias(rel_bias):
    nb, heads = rel_bias.shape
    blk = ATTN_BLOCK
    q_idx = jnp.arange(blk)[:, None]
    k_idx = jnp.arange(3 * blk)[None, :]
    rel = k_idx - blk - q_idx
    bucket = _t5_bucket(rel).reshape(1, 3 * blk * blk).astype(jnp.int32)
    inwin = (jnp.abs(rel) <= blk).astype(jnp.int32).reshape(1, 3 * blk * blk)
    tl = 4096
    out = pl.pallas_call(
        _bias_kernel,
        grid=(3 * blk * blk // tl,),
        in_specs=[pl.BlockSpec((1, tl), lambda i: (0, i)),
                  pl.BlockSpec((1, tl), lambda i: (0, i)),
                  pl.BlockSpec((heads, nb), lambda i: (0, 0))],
        out_specs=pl.BlockSpec((heads, tl), lambda i: (0, i)),
        out_shape=jax.ShapeDtypeStruct((heads, 3 * blk * blk), F32),
        compiler_params=_cparams(1, VMEM_STREAM_MIB),
        name="attn_bias",
    )(bucket, inwin, rel_bias.T)
    return out.reshape(heads, blk, 3 * blk)


def _attn_kernel(sink_ref, q_ref, kp_ref, ko_ref, kn_ref, vp_ref, vo_ref, vn_ref, bias_ref,
                 o_ref, *, nblk, kvh, grp):
    blk, hd = ATTN_BLOCK, HEAD_DIM
    n = pl.program_id(0) % nblk
    col = lax.broadcasted_iota(jnp.int32, (1, 3 * blk), 1)
    key_pos = (n - 1) * blk + col
    edge = jnp.where((key_pos >= 0) & (key_pos < nblk * blk), 0.0, NEG_INF)
    scale = hd ** -0.5
    heads = [[kh * grp + g for g in range(grp)] for kh in range(kvh)]
    scores = []
    for kh in range(kvh):
        ks = slice(kh * hd, (kh + 1) * hd)
        k = jnp.concatenate([kp_ref[:, ks], ko_ref[:, ks], kn_ref[:, ks]], axis=0)
        q = jnp.concatenate([q_ref[:, h * hd:(h + 1) * hd] for h in heads[kh]], axis=0)
        s = lax.dot_general(q, k, (((1,), (1,)), ((), ())), preferred_element_type=F32) * scale
        scores.append(s + bias_ref[kh * grp:(kh + 1) * grp].reshape(grp * blk, 3 * blk) + edge)
    probs, denoms = [], []
    for kh in range(kvh):
        s = scores[kh]
        sink = jnp.concatenate([jnp.full((blk, 1), sink_ref[h], F32) for h in heads[kh]], axis=0)
        mx = jnp.maximum(jnp.max(s, axis=-1, keepdims=True), sink)
        p = jnp.exp(s - mx)
        denoms.append(jnp.sum(p, axis=-1, keepdims=True) + jnp.exp(sink - mx))
        probs.append(p.astype(BF16))
    for kh in range(kvh):
        ks = slice(kh * hd, (kh + 1) * hd)
        v = jnp.concatenate([vp_ref[:, ks], vo_ref[:, ks], vn_ref[:, ks]], axis=0)
        o = jnp.dot(probs[kh], v, preferred_element_type=F32) / denoms[kh]
        for g, h in enumerate(heads[kh]):
            o_ref[:, h * hd:(h + 1) * hd] = o[g * blk:(g + 1) * blk].astype(o_ref.dtype)


def _attention(qkv, bias, sink, seq, heads, kvh):
    m = qkv.shape[0]
    blk, hd = ATTN_BLOCK, HEAD_DIM
    nblk = seq // blk
    grp = heads // kvh
    qw, kw = heads * hd, kvh * hd
    kcol, vcol = qw // kw, qw // kw + 1

    def prev(i):
        return jnp.where(i % nblk == 0, i, i - 1)

    def nxt(i):
        return jnp.where(i % nblk == nblk - 1, i, i + 1)

    kv = lambda rowf, colb: pl.BlockSpec((blk, kw), lambda i: (rowf(i), colb))
    same = lambda i: i
    return pl.pallas_call(
        functools.partial(_attn_kernel, nblk=nblk, kvh=kvh, grp=grp),
        grid=(m // blk,),
        in_specs=[pl.BlockSpec(memory_space=pltpu.SMEM),
                  pl.BlockSpec((blk, qw), lambda i: (i, 0)),
                  kv(prev, kcol), kv(same, kcol), kv(nxt, kcol),
                  kv(prev, vcol), kv(same, vcol), kv(nxt, vcol),
                  pl.BlockSpec((heads, blk, 3 * blk), lambda i: (0, 0, 0))],
        out_specs=pl.BlockSpec((blk, qw), lambda i: (i, 0)),
        out_shape=jax.ShapeDtypeStruct((m, qw), BF16),
        compiler_params=_cparams(1, VMEM_PANEL_MIB),
        name="attention",
    )(sink, qkv, qkv, qkv, qkv, qkv, qkv, qkv, bias)


def _cexp(zr, zi):
    mag = jnp.exp(zr)
    return mag * jnp.cos(zi), mag * jnp.sin(zi)


def _cpow_int(br, bi, e, nbits):
    res_r = res_i = None
    for bit in range(nbits):
        on = ((e >> bit) & 1) == 1
        fr = jnp.where(on, br, 1.0)
        fi = jnp.where(on, bi, 0.0)
        if res_r is None:
            res_r, res_i = fr, fi
        else:
            res_r, res_i = res_r * fr - res_i * fi, res_r * fi + res_i * fr
        if bit + 1 < nbits:
            br, bi = br * br - bi * bi, 2.0 * br * bi
    return res_r, res_i


def _ssm_prep_group(g, lre_r, lim_r, ldt_r, lre_c, lim_c, ldt_c, btr, bti, ctr, cti, dcol):
    t, hch, p = SSM_CHUNK, SSM_GROUP_CH, SSM_STATE
    th, p2 = t * hch, 2 * p
    hi_prec = lax.Precision.HIGHEST
    ar, ai = lre_r[g], lim_r[g]
    dt = jnp.exp(ldt_r[g])
    zr, zi = ar * dt, ai * dt
    lbr, lbi = _cexp(zr, zi)
    nr = lbr - 1.0
    den = ar * ar + ai * ai
    cr = (nr * ar + lbi * ai) / den
    ci = (lbi * ar - nr * ai) / den
    b_r, b_i = btr[g], bti[g]
    bbr = cr * b_r - ci * b_i
    bbi = cr * b_i + ci * b_r
    row = lax.broadcasted_iota(jnp.int32, (th, p2), 0)
    lane = lax.broadcasted_iota(jnp.int32, (th, p2), 1)
    j = row // hch
    nbits = (t - 1).bit_length()
    pr, pi = _cpow_int(lbr, lbi, jnp.where(lane < p, t - 1 - j, j), nbits)
    bt_r = jnp.concatenate([bbr] * t, axis=0)
    bt_i = jnp.concatenate([bbi] * t, axis=0)
    w_mat = jnp.concatenate([pr * bt_r - pi * bt_i, pr * bt_i + pi * bt_r], axis=1)
    l_r, l_i = lbr, lbi
    for _ in range(nbits):
        l_r, l_i = l_r * l_r - l_i * l_i, 2.0 * l_r * l_i
    lam = jnp.concatenate([l_r, l_i], axis=1)
    arc, aic = lre_c[g], lim_c[g]
    dtc = jnp.exp(ldt_c[g])
    zrc, zic = arc * dtc, aic * dtc
    rowc = lax.broadcasted_iota(jnp.int32, (p2, th), 0)
    lanec = lax.broadcasted_iota(jnp.int32, (p2, th), 1)
    nn = lanec // hch
    lbrc, lbic = _cexp(zrc, zic)
    qr, qi = _cpow_int(lbrc, lbic, jnp.where(rowc < p, nn, t - 1 - nn), nbits)
    tile = (lax.broadcasted_iota(jnp.int32, (hch, th), 1) % hch
            == lax.broadcasted_iota(jnp.int32, (hch, th), 0)).astype(F32)
    c_r = jnp.dot(ctr[g], tile, preferred_element_type=F32, precision=hi_prec)
    c_i = jnp.dot(cti[g], tile, preferred_element_type=F32, precision=hi_prec)
    e_r = c_r * qr - c_i * qi
    e_i = c_r * qi + c_i * qr
    v_mat = jnp.concatenate([e_r * lbrc - e_i * lbic, -(e_r * lbic + e_i * lbrc)], axis=0)
    fwd_lane = lax.broadcasted_iota(jnp.int32, (hch, p2), 1) < p
    rhs = jnp.concatenate([e_r, e_i], axis=0)
    lhs0 = jnp.concatenate([jnp.where(fwd_lane, bbr, 0.0), -jnp.where(fwd_lane, bbi, 0.0)], axis=1)
    lhs1 = jnp.concatenate([jnp.where(fwd_lane, 0.0, bbr), -jnp.where(fwd_lane, 0.0, bbi)], axis=1)
    k0 = jnp.dot(lhs0, rhs, preferred_element_type=F32, precision=hi_prec)
    k1 = jnp.dot(lhs1, rhs, preferred_element_type=F32, precision=hi_prec)
    lane_m = lax.broadcasted_iota(jnp.int32, (hch, th), 1)
    row_m = lax.broadcasted_iota(jnp.int32, (hch, th), 0)
    d_g = dcol[g]
    blocks = []
    for jj in range(t):
        a = pltpu.roll(k0, hch * jj, 1) if jj else k0
        a = jnp.where(lane_m >= hch * jj, a, 0.0)
        sh = (hch * (jj + 1)) % th
        b = pltpu.roll(k1, sh, 1) if sh else k1
        b = jnp.where(lane_m < hch * (jj + 1), b, 0.0)
        dd = jnp.where(lane_m == hch * jj + row_m, d_g, 0.0)
        blocks.append(a + b + dd)
    m_mat = jnp.concatenate(blocks, axis=0)
    return m_mat, w_mat, v_mat, lam


def _ssm_prep_kernel(lre_r, lim_r, ldt_r, lre_c, lim_c, ldt_c, btr, bti, ctr, cti, dcol,
                     m_ref, w_ref, v_ref, lam_ref, *, pairs, nb):
    ins = (lre_r, lim_r, ldt_r, lre_c, lim_c, ldt_c, btr, bti, ctr, cti, dcol)
    first = lax.broadcasted_iota(jnp.int32, (2 * nb, 4 * SSM_STATE), 0) < nb

    def body(q, carry):
        lams = []
        for s in range(2):
            g = 2 * q + s
            m_mat, w_mat, v_mat, lam = _ssm_prep_group(g, *ins)
            m_ref[g] = m_mat.astype(m_ref.dtype)
            w_ref[g] = w_mat.astype(w_ref.dtype)
            v_ref[g] = v_mat.astype(v_ref.dtype)
            lams.append(jnp.broadcast_to(lam, (2 * nb, 4 * SSM_STATE)))
        lam_ref[q] = jnp.where(first, lams[0], lams[1])
        return carry

    lax.fori_loop(0, pairs, body, 0)


def _ssm_prep(lre, lim, ldt, bre, bim, cre, cim, d, nb):
    _, g, p = lre.shape
    hch, t = SSM_GROUP_CH, SSM_CHUNK
    th = t * hch
    cat = lambda a: jnp.concatenate([a[0], a[1]], axis=-1)
    lre2, lim2 = cat(lre), cat(lim)
    ldt2 = jnp.repeat(ldt.T, p, axis=1)
    bt = lambda a: jnp.transpose(a, (1, 3, 0, 2)).reshape(g, hch, 2 * p)
    ct = lambda a: jnp.transpose(a, (1, 0, 3, 2)).reshape(g, 2 * p, hch)
    gp = _pick(g, (8, 4, 2))
    rowv = pl.BlockSpec((gp, 1, 2 * p), lambda i: (i, 0, 0))
    colv = pl.BlockSpec((gp, 2 * p, 1), lambda i: (i, 0, 0))
    mat = pl.BlockSpec((gp, th, th), lambda i: (i, 0, 0))
    return pl.pallas_call(
        functools.partial(_ssm_prep_kernel, pairs=gp // 2, nb=nb),
        grid=(g // gp,),
        in_specs=[rowv, rowv, rowv, colv, colv, colv,
                  pl.BlockSpec((gp, hch, 2 * p), lambda i: (i, 0, 0)),
                  pl.BlockSpec((gp, hch, 2 * p), lambda i: (i, 0, 0)),
                  pl.BlockSpec((gp, 2 * p, hch), lambda i: (i, 0, 0)),
                  pl.BlockSpec((gp, 2 * p, hch), lambda i: (i, 0, 0)),
                  pl.BlockSpec((gp, hch, 1), lambda i: (i, 0, 0))],
        out_specs=[mat, mat, mat,
                   pl.BlockSpec((gp // 2, 2 * nb, 4 * p), lambda i: (i, 0, 0))],
        out_shape=[jax.ShapeDtypeStruct((g, th, th), BF16)] * 3
        + [jax.ShapeDtypeStruct((g // 2, 2 * nb, 4 * p), F32)],
        compiler_params=_cparams(1, VMEM_STREAM_MIB),
        name="ssm_prep",
    )(lre2.reshape(g, 1, 2 * p), lim2.reshape(g, 1, 2 * p), ldt2.reshape(g, 1, 2 * p),
      lre2.reshape(g, 2 * p, 1), lim2.reshape(g, 2 * p, 1), ldt2.reshape(g, 2 * p, 1),
      bt(bre), bt(bim), ct(cre), ct(cim), d.reshape(g, hch, 1))


def _ssm_kernel(u_ref, m_ref, w_ref, v_ref, lam_ref, yext_ref, y_ref, yin_s, s_s, xf_s, xb_s,
                tok_s, u2_s, *, pb, nchunk, nb):
    slab = 2 * nb
    rows = nchunk * slab
    p2 = 2 * SSM_STATE
    hch, tch = SSM_GROUP_CH, SSM_CHUNK
    half = tch * hch // 2
    lanes = 2 * pb * hch
    slots = lanes // hch
    first = (lax.broadcasted_iota(jnp.int32, (rows, 1), 0) // nb) % 2 == 0
    lane_grp = lax.broadcasted_iota(jnp.int32, (nchunk, lanes), 1) // hch

    def pair_dot(lhs, mats, q):
        ya = jnp.dot(lhs, mats[2 * q], preferred_element_type=F32)
        yb = jnp.dot(lhs, mats[2 * q + 1], preferred_element_type=F32)
        return jnp.where(first, ya, yb)

    for b in range(nb):
        for tq in range(tch):
            tok_s[pl.ds((b * tch + tq) * nchunk, nchunk), :] = (
                u_ref[pl.ds(b * nchunk * tch + tq, nchunk, stride=tch), :])
    for q in range(pb):
        for g2 in range(2):
            gl = 2 * q + g2
            for b in range(nb):
                for col in range(2):
                    acc = None
                    for j in range(slots):
                        tq = col * slots + j
                        piece = tok_s[pl.ds((b * tch + tq) * nchunk, nchunk), :]
                        shift = (hch * j - hch * gl) % lanes
                        if shift:
                            piece = pltpu.roll(piece, shift, 1)
                        acc = piece if acc is None else jnp.where(lane_grp == j, piece, acc)
                    u2_s[q, col, pl.ds(g2 * nb + b, nchunk, stride=slab), :] = acc

    for q in range(pb):
        u = jnp.concatenate([u2_s[q, 0], u2_s[q, 1]], axis=1).astype(BF16)
        y0 = pair_dot(u, m_ref, q)
        yin_s[q, 0] = y0[:, :half]
        yin_s[q, 1] = y0[:, half:]
        s_s[q] = pair_dot(u, w_ref, q)

    fwd = lax.broadcasted_iota(jnp.int32, (slab, p2), 1) < SSM_STATE
    lam_r = [lam_ref[q, :, 0:p2] for q in range(pb)]
    lam_i = [lam_ref[q, :, p2:2 * p2] for q in range(pb)]

    def step(k, carry):
        kf = pl.multiple_of(k * slab, slab)
        kb = pl.multiple_of((nchunk - 1 - k) * slab, slab)
        out = []
        for q in range(pb):
            xr, xi = carry[2 * q], carry[2 * q + 1]
            xf_s[q, pl.ds(kf, slab), 0:p2] = xr
            xf_s[q, pl.ds(kf, slab), p2:2 * p2] = xi
            xb_s[q, pl.ds(kb, slab), 0:p2] = xr
            xb_s[q, pl.ds(kb, slab), p2:2 * p2] = xi
            sr = jnp.where(fwd, s_s[q, pl.ds(kf, slab), 0:p2], s_s[q, pl.ds(kb, slab), 0:p2])
            si = jnp.where(fwd, s_s[q, pl.ds(kf, slab), p2:2 * p2],
                           s_s[q, pl.ds(kb, slab), p2:2 * p2])
            out.append(lam_r[q] * xr - lam_i[q] * xi + sr)
            out.append(lam_r[q] * xi + lam_i[q] * xr + si)
        return tuple(out)

    zero = jnp.zeros((slab, p2), F32)
    lax.fori_loop(0, nchunk, step, (zero,) * (2 * pb))

    fwd_all = lax.broadcasted_iota(jnp.int32, (rows, 2 * p2), 1) % p2 < SSM_STATE
    for q in range(pb):
        x = jnp.where(fwd_all, xf_s[q], xb_s[q]).astype(BF16)
        yv = pair_dot(x, v_ref, q)
        yin_s[q, 0] = yin_s[q, 0] + yv[:, :half]
        yin_s[q, 1] = yin_s[q, 1] + yv[:, half:]

    for q in range(pb):
        for col in range(2):
            for r in range(slab):
                u2_s[q, col, pl.ds(r * nchunk, nchunk), :] = (
                    yin_s[q, col, pl.ds(r, nchunk, stride=slab), :])
    for b in range(nb):
        for tq in range(tch):
            col = (tq * hch) // lanes
            acc = None
            for q in range(pb):
                for g2 in range(2):
                    gl = 2 * q + g2
                    piece = u2_s[q, col, pl.ds((g2 * nb + b) * nchunk, nchunk), :]
                    shift = (hch * gl - (tq * hch) % lanes) % lanes
                    if shift:
                        piece = pltpu.roll(piece, shift, 1)
                    acc = piece if acc is None else jnp.where(lane_grp == gl, piece, acc)
            tok_s[pl.ds(b * nchunk * tch + tq, nchunk, stride=tch), :] = acc
    y_ref[...] = jax.nn.gelu(tok_s[...] + yext_ref[...]).astype(y_ref.dtype)


def _ssm_core(u, m_mat, w_mat, v_mat, lam, yext, nchunk, nb):
    m, dm = u.shape
    th = SSM_CHUNK * SSM_GROUP_CH
    npair = dm // (2 * SSM_GROUP_CH)
    rows = nchunk * 2 * nb
    pb = 4
    assert npair % pb == 0 and 2 * pb * SSM_GROUP_CH == 128 and th == 256
    p4 = 4 * SSM_STATE
    mat = pl.BlockSpec((2 * pb, th, th), lambda i: (i, 0, 0))
    tok = pl.BlockSpec((m, 128), lambda i: (0, i))
    return pl.pallas_call(
        functools.partial(_ssm_kernel, pb=pb, nchunk=nchunk, nb=nb),
        grid=(npair // pb,),
        in_specs=[tok, mat, mat, mat, pl.BlockSpec((pb, 2 * nb, p4), lambda i: (i, 0, 0)), tok],
        out_specs=tok,
        out_shape=jax.ShapeDtypeStruct((m, dm), BF16),
        scratch_shapes=[pltpu.VMEM((pb, 2, rows, th // 2), F32), pltpu.VMEM((pb, rows, p4), F32),
                        pltpu.VMEM((pb, rows, p4), F32), pltpu.VMEM((pb, rows, p4), F32),
                        pltpu.VMEM((m, 128), F32), pltpu.VMEM((pb, 2, rows, th // 2), F32)],
        compiler_params=_cparams(1, VMEM_MAX_MIB),
        name="ssm_core",
    )(u, m_mat, w_mat, v_mat, lam, yext)


def _s5_mixer(h, w_in, lre, lim, ldt, bre, bim, cre, cim, d, w_glu, batch, seq):
    m, dm = h.shape
    u = _mm(h, w_in, F32, tm=_pick(m, (2048, 1024, 512, 256, 128)), tn=_pick(dm, (256, 128)))
    yb = _s5_direction(u.reshape(batch, seq, dm), lre[1], lim[1], ldt[1], bre[1], bim[1],
                       cre[1], cim[1], reverse=True)
    m_mat, w_mat, v_mat, lam = _ssm_prep(lre, lim, ldt, bre.at[1].set(0.0), bim.at[1].set(0.0),
                                         cre, cim, d, batch)
    yg = _ssm_core(u, m_mat, w_mat, v_mat, lam, yb.reshape(m, dm), seq // SSM_CHUNK, batch)
    return _glu(yg, w_glu, tm=_pick(m, (2048, 1024, 512, 256, 128)), tn=_pick(dm, (256, 128)))


def _recurrence_combine(left, right):
    a_l, b_l = left
    a_r, b_r = right
    return a_r * a_l, a_r * b_l + b_r


def _s5_direction(u, lam_re, lam_im, log_dt, b_re, b_im, c_re, c_im, reverse):
    bsz, seq, dm = u.shape
    g, p = lam_re.shape
    ug = u.reshape(bsz, seq, g, dm // g)
    lam = lax.complex(lam_re, lam_im)
    dt = jnp.exp(log_dt)[:, None]
    lam_bar = jnp.exp(lam * dt)
    b_bar = ((lam_bar - 1.0) / lam)[..., None] * lax.complex(b_re, b_im)
    c_mat = lax.complex(c_re, c_im)
    bu = jnp.einsum('blgh,gph->blgp', ug, b_bar)
    a = jnp.broadcast_to(lam_bar, (1, seq, g, p))
    _, states = lax.associative_scan(_recurrence_combine, (a, bu), axis=1, reverse=reverse)
    return jnp.real(jnp.einsum('blgp,ghp->blgh', states, c_mat)).reshape(bsz, seq, dm)


def kernel(x, rel_bias, pre_mix_norm, post_mix_norm, pre_ffn_norm, post_ffn_norm, attn_wqkv, attn_sink, attn_wo, ssm_w_in, ssm_lambda_re, ssm_lambda_im, ssm_log_dt, ssm_b_re, ssm_b_im, ssm_c_re, ssm_c_im, ssm_d, ssm_w_glu, ffn_w_gate, ffn_w_up, ffn_conv_w, ffn_conv_b, ffn_w_down):
    batch, seq, dm = x.shape
    depth = pre_mix_norm.shape[0]
    m = batch * seq
    heads = dm // HEAD_DIM
    kvh = (attn_wqkv.shape[2] // HEAD_DIM - heads) // 2
    dff = ffn_w_gate.shape[2]
    assert seq % ATTN_BLOCK == 0 and seq % SSM_CHUNK == 0 and (2 * batch) % 8 == 0
    tm_big = _pick(m, (2048, 1024, 512, 256, 128))
    tm_mid = _pick(m, (1024, 512, 256, 128))

    xf = x.reshape(m, dm)
    h = _norm_cast(xf, pre_mix_norm[0])
    bias = _attn_bias(rel_bias)
    for i in range(depth):
        j = i // 2
        if i % 2 == 0:
            qkv = _mm(h, attn_wqkv[j], BF16, tm=tm_big, tn=_pick(attn_wqkv.shape[2], (512, 256, 128)))
            o = _attention(qkv, bias, attn_sink[j], seq, heads, kvh)
            mix = _mm(o, attn_wo[j], F32, tm=tm_big, tn=_pick(dm, (512, 256, 128)))
        else:
            mix = _s5_mixer(h, ssm_w_in[j], ssm_lambda_re[j], ssm_lambda_im[j], ssm_log_dt[j],
                            ssm_b_re[j], ssm_b_im[j], ssm_c_re[j], ssm_c_im[j], ssm_d[j],
                            ssm_w_glu[j], batch, seq)
        xf, h = _resid_norm(xf, mix, post_mix_norm[i], pre_ffn_norm[i])
        hid = _ffn_in(h, ffn_w_gate[i], ffn_w_up[i], ffn_conv_w[i], ffn_conv_b[i], seq,
                      tn=_pick(dff, (256, 128)))
        kc = dff // 2 if (dff // 2) % 128 == 0 else dff
        f = _mm(hid, ffn_w_down[i], F32, tm=tm_mid, tn=_pick(dm, (256, 128)), kc=kc,
                vmem_mib=VMEM_MAX_MIB)
        g_next = pre_mix_norm[i + 1] if i + 1 < depth else None
        xf, h = _resid_norm(xf, f, post_ffn_norm[i], g_next)
    return xf.reshape(batch, seq, dm)
```

```python
import functools
import math

import jax
import jax.numpy as jnp
from jax import lax
from jax.experimental import pallas as pl
from jax.experimental.pallas import tpu as pltpu

F32 = jnp.float32
BF16 = jnp.bfloat16

HEAD_DIM = 128
ATTN_BLOCK = 128
NUM_BUCKETS = 32
SSM_GROUP_CH = 16
SSM_STATE = 64
SSM_CHUNK = 16
SSM_TILE_GROUPS = 8
RMS_EPS = 1e-6
NEG_INF = -1e30

MIB = 1024 * 1024
VMEM_STREAM_MIB = 32
VMEM_PANEL_MIB = 56
VMEM_MAX_MIB = 60


def _cparams(n_grid_dims, vmem_mib):
    return pltpu.CompilerParams(
        dimension_semantics=("arbitrary",) * n_grid_dims,
        vmem_limit_bytes=vmem_mib * MIB,
    )


def _pick(n, prefs):
    for p in prefs:
        if p <= n and n % p == 0:
            return p
    return n


def _rms(x, g):
    return x * lax.rsqrt(jnp.mean(x * x, axis=-1, keepdims=True) + RMS_EPS) * g


def _norm_kernel(x_ref, g_ref, h_ref):
    h_ref[...] = _rms(x_ref[...], g_ref[...]).astype(h_ref.dtype)


def _norm_cast(x, g):
    m, d = x.shape
    tm = _pick(m, (256, 128, 64, 32, 16, 8))
    return pl.pallas_call(
        _norm_kernel,
        grid=(m // tm,),
        in_specs=[pl.BlockSpec((tm, d), lambda i: (i, 0)),
                  pl.BlockSpec((1, d), lambda i: (0, 0))],
        out_specs=pl.BlockSpec((tm, d), lambda i: (i, 0)),
        out_shape=jax.ShapeDtypeStruct((m, d), BF16),
        compiler_params=_cparams(1, VMEM_STREAM_MIB),
        name="norm_cast",
    )(x, g.reshape(1, d))


def _resid_norm_kernel(x_ref, m_ref, g1_ref, g2_ref, xo_ref, ho_ref):
    xn = x_ref[...] + _rms(m_ref[...], g1_ref[...])
    xo_ref[...] = xn
    ho_ref[...] = _rms(xn, g2_ref[...]).astype(ho_ref.dtype)


def _resid_kernel(x_ref, m_ref, g1_ref, xo_ref):
    xo_ref[...] = x_ref[...] + _rms(m_ref[...], g1_ref[...])


def _resid_norm(x, mix, g_post, g_next):
    m, d = x.shape
    tm = _pick(m, (128, 64, 32, 16, 8))
    row = pl.BlockSpec((tm, d), lambda i: (i, 0))
    vec = pl.BlockSpec((1, d), lambda i: (0, 0))
    if g_next is None:
        return pl.pallas_call(
            _resid_kernel, grid=(m // tm,),
            in_specs=[row, row, vec], out_specs=row,
            out_shape=jax.ShapeDtypeStruct((m, d), F32),
            compiler_params=_cparams(1, VMEM_STREAM_MIB), name="resid",
        )(x, mix, g_post.reshape(1, d)), None
    return pl.pallas_call(
        _resid_norm_kernel, grid=(m // tm,),
        in_specs=[row, row, vec, vec], out_specs=[row, row],
        out_shape=[jax.ShapeDtypeStruct((m, d), F32), jax.ShapeDtypeStruct((m, d), BF16)],
        compiler_params=_cparams(1, VMEM_STREAM_MIB), name="resid_norm",
    )(x, mix, g_post.reshape(1, d), g_next.reshape(1, d))


def _fetch_row_panel(a_hbm, a_ref, sem):
    @pl.when(pl.program_id(1) == 0)
    def _():
        tm = a_ref.shape[0]
        cp = pltpu.make_async_copy(
            a_hbm.at[pl.ds(pl.multiple_of(pl.program_id(0) * tm, tm), tm), :], a_ref, sem)
        cp.start()
        cp.wait()


def _panel_scratch(tm, k):
    return [pltpu.VMEM((tm, k), BF16), pltpu.SemaphoreType.DMA(())]


def _mm_kernel(a_hbm, w_ref, o_ref, a_ref, sem, *, kc):
    _fetch_row_panel(a_hbm, a_ref, sem)
    k = a_ref.shape[1]
    acc = None
    for k0 in range(0, k, kc):
        part = jnp.dot(a_ref[:, k0:k0 + kc], w_ref[k0:k0 + kc, :].astype(BF16),
                       preferred_element_type=F32)
        acc = part if acc is None else acc + part
    o_ref[...] = acc.astype(o_ref.dtype)


def _mm(a, w, out_dtype, tm, tn, kc=None, vmem_mib=VMEM_PANEL_MIB):
    m, k = a.shape
    n = w.shape[1]
    kc = k if kc is None else kc
    return pl.pallas_call(
        functools.partial(_mm_kernel, kc=kc),
        grid=(m // tm, n // tn),
        in_specs=[pl.BlockSpec(memory_space=pl.ANY),
                  pl.BlockSpec((k, tn), lambda i, j: (0, j))],
        out_specs=pl.BlockSpec((tm, tn), lambda i, j: (i, j)),
        out_shape=jax.ShapeDtypeStruct((m, n), out_dtype),
        scratch_shapes=_panel_scratch(tm, k),
        compiler_params=_cparams(2, vmem_mib),
        name="mm",
    )(a, w)


def _glu_kernel(a_hbm, wa_ref, wb_ref, o_ref, a_ref, sem):
    _fetch_row_panel(a_hbm, a_ref, sem)
    a = a_ref[...]
    ya = jnp.dot(a, wa_ref[...].astype(BF16), preferred_element_type=F32)
    yb = jnp.dot(a, wb_ref[...].astype(BF16), preferred_element_type=F32)
    o_ref[...] = (ya * jax.nn.sigmoid(yb)).astype(o_ref.dtype)


def _glu(a, w, tm, tn):
    m, k = a.shape
    n = w.shape[1] // 2
    nj = n // tn
    return pl.pallas_call(
        _glu_kernel,
        grid=(m // tm, nj),
        in_specs=[pl.BlockSpec(memory_space=pl.ANY),
                  pl.BlockSpec((k, tn), lambda i, j: (0, j)),
                  pl.BlockSpec((k, tn), lambda i, j: (0, j + nj))],
        out_specs=pl.BlockSpec((tm, tn), lambda i, j: (i, j)),
        out_shape=jax.ShapeDtypeStruct((m, n), F32),
        scratch_shapes=_panel_scratch(tm, k),
        compiler_params=_cparams(2, VMEM_PANEL_MIB),
        name="glu",
    )(a, w, w)


def _ffn_in_kernel(a_hbm, wg_ref, wu_ref, cw_ref, cb_ref, o_ref, a_ref, sem):
    _fetch_row_panel(a_hbm, a_ref, sem)
    a = a_ref[...]
    g = jnp.dot(a, wg_ref[...].astype(BF16), preferred_element_type=F32)
    u = jnp.dot(a, wu_ref[...].astype(BF16), preferred_element_type=F32)
    rows = g.shape[0]
    row = lax.broadcasted_iota(jnp.int32, (rows, 1), 0)
    g_prev = jnp.where(row == 0, 0.0, pltpu.roll(g, 1, 0))
    g_next = jnp.where(row == rows - 1, 0.0, pltpu.roll(g, rows - 1, 0))
    gc = cw_ref[0:1, :] * g_prev + cw_ref[1:2, :] * g + cw_ref[2:3, :] * g_next + cb_ref[...]
    o_ref[...] = (gc * jax.nn.sigmoid(gc) * u).astype(o_ref.dtype)


def _ffn_in(h, w_gate, w_up, conv_w, conv_b, seq, tn):
    m, k = h.shape
    f = w_gate.shape[1]
    return pl.pallas_call(
        _ffn_in_kernel,
        grid=(m // seq, f // tn),
        in_specs=[pl.BlockSpec(memory_space=pl.ANY),
                  pl.BlockSpec((k, tn), lambda i, j: (0, j)),
                  pl.BlockSpec((k, tn), lambda i, j: (0, j)),
                  pl.BlockSpec((3, tn), lambda i, j: (0, j)),
                  pl.BlockSpec((1, tn), lambda i, j: (0, j))],
        out_specs=pl.BlockSpec((seq, tn), lambda i, j: (i, j)),
        out_shape=jax.ShapeDtypeStruct((m, f), BF16),
        scratch_shapes=_panel_scratch(seq, k),
        compiler_params=_cparams(2, VMEM_PANEL_MIB),
        name="ffn_in",
    )(h, w_gate, w_up, conv_w, conv_b.reshape(1, f))


def _t5_bucket(rel):
    half = NUM_BUCKETS // 2
    max_exact = half // 2
    base = jnp.where(rel > 0, half, 0)
    n = jnp.abs(rel)
    nf = jnp.maximum(n, 1).astype(F32)
    large = max_exact + (jnp.log(nf / max_exact) / math.log(ATTN_BLOCK / max_exact)
                         * (half - max_exact)).astype(jnp.int32)
    large = jnp.minimum(large, half - 1)
    return base + jnp.where(n < max_exact, n, large)


def _bias_kernel(bucket_ref, inwin_ref, rbt_ref, o_ref):
    nb = rbt_ref.shape[1]
    lanes = bucket_ref.shape[1]
    onehot = (lax.broadcasted_iota(jnp.int32, (nb, lanes), 0) == bucket_ref[...]).astype(F32)
    bias = jnp.dot(rbt_ref[...], onehot, preferred_element_type=F32,
                   precision=lax.Precision.HIGHEST)
    o_ref[...] = jnp.where(inwin_ref[...] > 0, bias, NEG_INF)


def _attn_bias(rel_bias):
    nb, heads = rel_bias.shape
    blk = ATTN_BLOCK
    q_idx = jnp.arange(blk)[:, None]
    k_idx = jnp.arange(3 * blk)[None, :]
    rel = k_idx - blk - q_idx
    bucket = _t5_bucket(rel).reshape(1, 3 * blk * blk).astype(jnp.int32)
    inwin = (jnp.abs(rel) <= blk).astype(jnp.int32).reshape(1, 3 * blk * blk)
    tl = 4096
    out = pl.pallas_call(
        _bias_kernel,
        grid=(3 * blk * blk // tl,),
        in_specs=[pl.BlockSpec((1, tl), lambda i: (0, i)),
                  pl.BlockSpec((1, tl), lambda i: (0, i)),
                  pl.BlockSpec((heads, nb), lambda i: (0, 0))],
        out_specs=pl.BlockSpec((heads, tl), lambda i: (0, i)),
        out_shape=jax.ShapeDtypeStruct((heads, 3 * blk * blk), F32),
        compiler_params=_cparams(1, VMEM_STREAM_MIB),
        name="attn_bias",
    )(bucket, inwin, rel_bias.T)
    return out.reshape(heads, blk, 3 * blk)


def _attn_kernel(sink_ref, q_ref, kp_ref, ko_ref, kn_ref, vp_ref, vo_ref, vn_ref, bias_ref,
                 o_ref, *, nblk, kvh, grp):
    blk, hd = ATTN_BLOCK, HEAD_DIM
    n = pl.program_id(0) % nblk
    col = lax.broadcasted_iota(jnp.int32, (1, 3 * blk), 1)
    key_pos = (n - 1) * blk + col
    edge = jnp.where((key_pos >= 0) & (key_pos < nblk * blk), 0.0, NEG_INF)
    scale = hd ** -0.5
    heads = [[kh * grp + g for g in range(grp)] for kh in range(kvh)]
    scores = []
    for kh in range(kvh):
        ks = slice(kh * hd, (kh + 1) * hd)
        k = jnp.concatenate([kp_ref[:, ks], ko_ref[:, ks], kn_ref[:, ks]], axis=0)
        q = jnp.concatenate([q_ref[:, h * hd:(h + 1) * hd] for h in heads[kh]], axis=0)
        s = lax.dot_general(q, k, (((1,), (1,)), ((), ())), preferred_element_type=F32) * scale
        scores.append(s + bias_ref[kh * grp:(kh + 1) * grp].reshape(grp * blk, 3 * blk) + edge)
    probs, denoms = [], []
    for kh in range(kvh):
        s = scores[kh]
        sink = jnp.concatenate([jnp.full((blk, 1), sink_ref[h], F32) for h in heads[kh]], axis=0)
        mx = jnp.maximum(jnp.max(s, axis=-1, keepdims=True), sink)
        p = jnp.exp(s - mx)
        denoms.append(jnp.sum(p, axis=-1, keepdims=True) + jnp.exp(sink - mx))
        probs.append(p.astype(BF16))
    for kh in range(kvh):
        ks = slice(kh * hd, (kh + 1) * hd)
        v = jnp.concatenate([vp_ref[:, ks], vo_ref[:, ks], vn_ref[:, ks]], axis=0)
        o = jnp.dot(probs[kh], v, preferred_element_type=F32) / denoms[kh]
        for g, h in enumerate(heads[kh]):
            o_ref[:, h * hd:(h + 1) * hd] = o[g * blk:(g + 1) * blk].astype(o_ref.dtype)


def _attention(qkv, bias, sink, seq, heads, kvh):
    m = qkv.shape[0]
    blk, hd = ATTN_BLOCK, HEAD_DIM
    nblk = seq // blk
    grp = heads // kvh
    qw, kw = heads * hd, kvh * hd
    kcol, vcol = qw // kw, qw // kw + 1

    def prev(i):
        return jnp.where(i % nblk == 0, i, i - 1)

    def nxt(i):
        return jnp.where(i % nblk == nblk - 1, i, i + 1)

    kv = lambda rowf, colb: pl.BlockSpec((blk, kw), lambda i: (rowf(i), colb))
    same = lambda i: i
    return pl.pallas_call(
        functools.partial(_attn_kernel, nblk=nblk, kvh=kvh, grp=grp),
        grid=(m // blk,),
        in_specs=[pl.BlockSpec(memory_space=pltpu.SMEM),
                  pl.BlockSpec((blk, qw), lambda i: (i, 0)),
                  kv(prev, kcol), kv(same, kcol), kv(nxt, kcol),
                  kv(prev, vcol), kv(same, vcol), kv(nxt, vcol),
                  pl.BlockSpec((heads, blk, 3 * blk), lambda i: (0, 0, 0))],
        out_specs=pl.BlockSpec((blk, qw), lambda i: (i, 0)),
        out_shape=jax.ShapeDtypeStruct((m, qw), BF16),
        compiler_params=_cparams(1, VMEM_PANEL_MIB),
        name="attention",
    )(sink, qkv, qkv, qkv, qkv, qkv, qkv, qkv, bias)


def _cexp(zr, zi):
    mag = jnp.exp(zr)
    return mag * jnp.cos(zi), mag * jnp.sin(zi)


def _cpow_int(br, bi, e, nbits):
    res_r = res_i = None
    for bit in range(nbits):
        on = ((e >> bit) & 1) == 1
        fr = jnp.where(on, br, 1.0)
        fi = jnp.where(on, bi, 0.0)
        if res_r is None:
            res_r, res_i = fr, fi
        else:
            res_r, res_i = res_r * fr - res_i * fi, res_r * fi + res_i * fr
        if bit + 1 < nbits:
            br, bi = br * br - bi * bi, 2.0 * br * bi
    return res_r, res_i


def _slot_sources(rot, t):
    half = t // 2
    return [c * half + (s - rot) % half for c in range(2) for s in range(half)]


def _rotate_slots(x, rot):
    if rot == 0:
        return x
    h = x.shape[1] // 2
    sh = SSM_GROUP_CH * rot
    return jnp.concatenate([pltpu.roll(x[:, :h], sh, 1), pltpu.roll(x[:, h:], sh, 1)], axis=1)


def _permute_row_blocks(x, rot, t):
    hch = SSM_GROUP_CH
    return jnp.concatenate([x[hch * j:hch * (j + 1)] for j in _slot_sources(rot, t)], axis=0)


def _ssm_prep_group(g, rot, lre_r, lim_r, ldt_r, lre_c, lim_c, ldt_c, btr, bti, ctr, cti, dcol):
    t, hch, p = SSM_CHUNK, SSM_GROUP_CH, SSM_STATE
    th, p2 = t * hch, 2 * p
    hi_prec = lax.Precision.HIGHEST
    ar, ai = lre_r[g], lim_r[g]
    dt = jnp.exp(ldt_r[g])
    zr, zi = ar * dt, ai * dt
    lbr, lbi = _cexp(zr, zi)
    nr = lbr - 1.0
    den = ar * ar + ai * ai
    cr = (nr * ar + lbi * ai) / den
    ci = (lbi * ar - nr * ai) / den
    b_r, b_i = btr[g], bti[g]
    bbr = cr * b_r - ci * b_i
    bbi = cr * b_i + ci * b_r
    row = lax.broadcasted_iota(jnp.int32, (th, p2), 0)
    lane = lax.broadcasted_iota(jnp.int32, (th, p2), 1)
    j = row // hch
    nbits = (t - 1).bit_length()
    pr, pi = _cpow_int(lbr, lbi, jnp.where(lane < p, t - 1 - j, j), nbits)
    bt_r = jnp.concatenate([bbr] * t, axis=0)
    bt_i = jnp.concatenate([bbi] * t, axis=0)
    w_mat = jnp.concatenate([pr * bt_r - pi * bt_i, pr * bt_i + pi * bt_r], axis=1)
    l_r, l_i = lbr, lbi
    for _ in range(nbits):
        l_r, l_i = l_r * l_r - l_i * l_i, 2.0 * l_r * l_i
    lam = jnp.concatenate([l_r, l_i], axis=1)
    arc, aic = lre_c[g], lim_c[g]
    dtc = jnp.exp(ldt_c[g])
    zrc, zic = arc * dtc, aic * dtc
    rowc = lax.broadcasted_iota(jnp.int32, (p2, th), 0)
    lanec = lax.broadcasted_iota(jnp.int32, (p2, th), 1)
    nn = lanec // hch
    lbrc, lbic = _cexp(zrc, zic)
    qr, qi = _cpow_int(lbrc, lbic, jnp.where(rowc < p, nn, t - 1 - nn), nbits)
    tile = (lax.broadcasted_iota(jnp.int32, (hch, th), 1) % hch
            == lax.broadcasted_iota(jnp.int32, (hch, th), 0)).astype(F32)
    c_r = jnp.dot(ctr[g], tile, preferred_element_type=F32, precision=hi_prec)
    c_i = jnp.dot(cti[g], tile, preferred_element_type=F32, precision=hi_prec)
    e_r = c_r * qr - c_i * qi
    e_i = c_r * qi + c_i * qr
    v_mat = jnp.concatenate([e_r * lbrc - e_i * lbic, -(e_r * lbic + e_i * lbrc)], axis=0)
    fwd_lane = lax.broadcasted_iota(jnp.int32, (hch, p2), 1) < p
    rhs = jnp.concatenate([e_r, e_i], axis=0)
    lhs0 = jnp.concatenate([jnp.where(fwd_lane, bbr, 0.0), -jnp.where(fwd_lane, bbi, 0.0)], axis=1)
    lhs1 = jnp.concatenate([jnp.where(fwd_lane, 0.0, bbr), -jnp.where(fwd_lane, 0.0, bbi)], axis=1)
    k0 = jnp.dot(lhs0, rhs, preferred_element_type=F32, precision=hi_prec)
    k1 = jnp.dot(lhs1, rhs, preferred_element_type=F32, precision=hi_prec)
    lane_m = lax.broadcasted_iota(jnp.int32, (hch, th), 1)
    row_m = lax.broadcasted_iota(jnp.int32, (hch, th), 0)
    d_g = dcol[g]
    blocks = []
    for jj in range(t):
        a = pltpu.roll(k0, hch * jj, 1) if jj else k0
        a = jnp.where(lane_m >= hch * jj, a, 0.0)
        sh = (hch * (jj + 1)) % th
        b = pltpu.roll(k1, sh, 1) if sh else k1
        b = jnp.where(lane_m < hch * (jj + 1), b, 0.0)
        dd = jnp.where(lane_m == hch * jj + row_m, d_g, 0.0)
        blocks.append(a + b + dd)
    m_mat = jnp.concatenate(blocks, axis=0)
    m_mat = _permute_row_blocks(_rotate_slots(m_mat, rot), rot, t)
    w_mat = _permute_row_blocks(w_mat, rot, t)
    v_mat = _rotate_slots(v_mat, rot)
    return m_mat, w_mat, v_mat, lam


def _ssm_prep_kernel(lre_r, lim_r, ldt_r, lre_c, lim_c, ldt_c, btr, bti, ctr, cti, dcol,
                     m_ref, w_ref, v_ref, lam_ref, *, groups, nb):
    ins = (lre_r, lim_r, ldt_r, lre_c, lim_c, ldt_c, btr, bti, ctr, cti, dcol)
    first = lax.broadcasted_iota(jnp.int32, (2 * nb, 4 * SSM_STATE), 0) < nb
    lams = []
    for g in range(groups):
        m_mat, w_mat, v_mat, lam = _ssm_prep_group(g, g % SSM_TILE_GROUPS, *ins)
        m_ref[g] = m_mat.astype(m_ref.dtype)
        w_ref[g] = w_mat.astype(w_ref.dtype)
        v_ref[g] = v_mat.astype(v_ref.dtype)
        lams.append(jnp.broadcast_to(lam, (2 * nb, 4 * SSM_STATE)))
        if g % 2 == 1:
            lam_ref[g // 2] = jnp.where(first, lams[g - 1], lams[g])


def _ssm_prep(lre, lim, ldt, bre, bim, cre, cim, d, nb):
    _, g, p = lre.shape
    hch, t = SSM_GROUP_CH, SSM_CHUNK
    th = t * hch
    cat = lambda a: jnp.concatenate([a[0], a[1]], axis=-1)
    lre2, lim2 = cat(lre), cat(lim)
    ldt2 = jnp.repeat(ldt.T, p, axis=1)
    bt = lambda a: jnp.transpose(a, (1, 3, 0, 2)).reshape(g, hch, 2 * p)
    ct = lambda a: jnp.transpose(a, (1, 0, 3, 2)).reshape(g, 2 * p, hch)
    gp = SSM_TILE_GROUPS
    assert g % gp == 0
    rowv = pl.BlockSpec((gp, 1, 2 * p), lambda i: (i, 0, 0))
    colv = pl.BlockSpec((gp, 2 * p, 1), lambda i: (i, 0, 0))
    mat = pl.BlockSpec((gp, th, th), lambda i: (i, 0, 0))
    return pl.pallas_call(
        functools.partial(_ssm_prep_kernel, groups=gp, nb=nb),
        grid=(g // gp,),
        in_specs=[rowv, rowv, rowv, colv, colv, colv,
                  pl.BlockSpec((gp, hch, 2 * p), lambda i: (i, 0, 0)),
                  pl.BlockSpec((gp, hch, 2 * p), lambda i: (i, 0, 0)),
                  pl.BlockSpec((gp, 2 * p, hch), lambda i: (i, 0, 0)),
                  pl.BlockSpec((gp, 2 * p, hch), lambda i: (i, 0, 0)),
                  pl.BlockSpec((gp, hch, 1), lambda i: (i, 0, 0))],
        out_specs=[mat, mat, mat,
                   pl.BlockSpec((gp // 2, 2 * nb, 4 * p), lambda i: (i, 0, 0))],
        out_shape=[jax.ShapeDtypeStruct((g, th, th), BF16)] * 3
        + [jax.ShapeDtypeStruct((g // 2, 2 * nb, 4 * p), F32)],
        compiler_params=_cparams(1, VMEM_STREAM_MIB),
        name="ssm_prep",
    )(lre2.reshape(g, 1, 2 * p), lim2.reshape(g, 1, 2 * p), ldt2.reshape(g, 1, 2 * p),
      lre2.reshape(g, 2 * p, 1), lim2.reshape(g, 2 * p, 1), ldt2.reshape(g, 2 * p, 1),
      bt(bre), bt(bim), ct(cre), ct(cim), d.reshape(g, hch, 1))


def _ssm_kernel(u_ref, m_ref, w_ref, v_ref, lam_ref, yext_ref, y_ref, yin_s, s_s, xf_s, xb_s,
                tok_s, u2_s, *, pb, nchunk, nb):
    slab = 2 * nb
    rows = nchunk * slab
    p2 = 2 * SSM_STATE
    hch, tch = SSM_GROUP_CH, SSM_CHUNK
    half = tch * hch // 2
    lanes = 2 * pb * hch
    slots = lanes // hch
    first = (lax.broadcasted_iota(jnp.int32, (rows, 1), 0) // nb) % 2 == 0
    lane_grp = lax.broadcasted_iota(jnp.int32, (nchunk, lanes), 1) // hch

    def pair_dot(lhs, mats, q):
        ya = jnp.dot(lhs, mats[2 * q], preferred_element_type=F32)
        yb = jnp.dot(lhs, mats[2 * q + 1], preferred_element_type=F32)
        return jnp.where(first, ya, yb)

    for b in range(nb):
        for tq in range(tch):
            tok_s[pl.ds((b * tch + tq) * nchunk, nchunk), :] = (
                u_ref[pl.ds(b * nchunk * tch + tq, nchunk, stride=tch), :])
    for b in range(nb):
        for col in range(2):
            rolled = []
            for j in range(slots):
                piece = tok_s[pl.ds((b * tch + col * slots + j) * nchunk, nchunk), :]
                rolled.append(pltpu.roll(piece, hch * j, 1) if j else piece)
            for gl in range(slots):
                acc = rolled[0]
                for j in range(1, slots):
                    acc = jnp.where(lane_grp == (j + gl) % slots, rolled[j], acc)
                q, g2 = divmod(gl, 2)
                u2_s[q, col, pl.ds(g2 * nb + b, nchunk, stride=slab), :] = acc

    for q in range(pb):
        u = jnp.concatenate([u2_s[q, 0], u2_s[q, 1]], axis=1).astype(BF16)
        y0 = pair_dot(u, m_ref, q)
        yin_s[q, 0] = y0[:, :half]
        yin_s[q, 1] = y0[:, half:]
        s_s[q] = pair_dot(u, w_ref, q)

    fwd = lax.broadcasted_iota(jnp.int32, (slab, p2), 1) < SSM_STATE
    lam_r = [lam_ref[q, :, 0:p2] for q in range(pb)]
    lam_i = [lam_ref[q, :, p2:2 * p2] for q in range(pb)]

    def step(k, carry):
        kf = pl.multiple_of(k * slab, slab)
        kb = pl.multiple_of((nchunk - 1 - k) * slab, slab)
        out = []
        for q in range(pb):
            xr, xi = carry[2 * q], carry[2 * q + 1]
            xf_s[q, pl.ds(kf, slab), 0:p2] = xr
            xf_s[q, pl.ds(kf, slab), p2:2 * p2] = xi
            xb_s[q, pl.ds(kb, slab), 0:p2] = xr
            xb_s[q, pl.ds(kb, slab), p2:2 * p2] = xi
            sr = jnp.where(fwd, s_s[q, pl.ds(kf, slab), 0:p2], s_s[q, pl.ds(kb, slab), 0:p2])
            si = jnp.where(fwd, s_s[q, pl.ds(kf, slab), p2:2 * p2],
                           s_s[q, pl.ds(kb, slab), p2:2 * p2])
            out.append(lam_r[q] * xr - lam_i[q] * xi + sr)
            out.append(lam_r[q] * xi + lam_i[q] * xr + si)
        return tuple(out)

    zero = jnp.zeros((slab, p2), F32)
    lax.fori_loop(0, nchunk, step, (zero,) * (2 * pb))

    fwd_all = lax.broadcasted_iota(jnp.int32, (rows, 2 * p2), 1) % p2 < SSM_STATE
    for q in range(pb):
        x = jnp.where(fwd_all, xf_s[q], xb_s[q]).astype(BF16)
        yv = pair_dot(x, v_ref, q)
        yin_s[q, 0] = yin_s[q, 0] + yv[:, :half]
        yin_s[q, 1] = yin_s[q, 1] + yv[:, half:]

    for q in range(pb):
        for col in range(2):
            for r in range(slab):
                u2_s[q, col, pl.ds(r * nchunk, nchunk), :] = (
                    yin_s[q, col, pl.ds(r, nchunk, stride=slab), :])
    for b in range(nb):
        for tq in range(tch):
            col, i = divmod(tq, slots)
            acc = None
            for gl in range(slots):
                q, g2 = divmod(gl, 2)
                piece = u2_s[q, col, pl.ds((g2 * nb + b) * nchunk, nchunk), :]
                acc = piece if acc is None else jnp.where(lane_grp == (i + gl) % slots, piece, acc)
            if i:
                acc = pltpu.roll(acc, (-hch * i) % lanes, 1)
            tok_s[pl.ds(b * nchunk * tch + tq, nchunk, stride=tch), :] = acc
    y_ref[...] = jax.nn.gelu(tok_s[...] + yext_ref[...]).astype(y_ref.dtype)


def _ssm_core(u, m_mat, w_mat, v_mat, lam, yext, nchunk, nb):
    m, dm = u.shape
    th = SSM_CHUNK * SSM_GROUP_CH
    npair = dm // (2 * SSM_GROUP_CH)
    rows = nchunk * 2 * nb
    pb = 4
    assert npair % pb == 0 and 2 * pb * SSM_GROUP_CH == 128 and th == 256
    p4 = 4 * SSM_STATE
    mat = pl.BlockSpec((2 * pb, th, th), lambda i: (i, 0, 0))
    tok = pl.BlockSpec((m, 128), lambda i: (0, i))
    return pl.pallas_call(
        functools.partial(_ssm_kernel, pb=pb, nchunk=nchunk, nb=nb),
        grid=(npair // pb,),
        in_specs=[tok, mat, mat, mat, pl.BlockSpec((pb, 2 * nb, p4), lambda i: (i, 0, 0)), tok],
        out_specs=tok,
        out_shape=jax.ShapeDtypeStruct((m, dm), BF16),
        scratch_shapes=[pltpu.VMEM((pb, 2, rows, th // 2), F32), pltpu.VMEM((pb, rows, p4), F32),
                        pltpu.VMEM((pb, rows, p4), F32), pltpu.VMEM((pb, rows, p4), F32),
                        pltpu.VMEM((m, 128), F32), pltpu.VMEM((pb, 2, rows, th // 2), F32)],
        compiler_params=_cparams(1, VMEM_MAX_MIB),
        name="ssm_core",
    )(u, m_mat, w_mat, v_mat, lam, yext)


def _s5_mixer(h, w_in, lre, lim, ldt, bre, bim, cre, cim, d, w_glu, batch, seq):
    m, dm = h.shape
    u = _mm(h, w_in, F32, tm=_pick(m, (2048, 1024, 512, 256, 128)), tn=_pick(dm, (256, 128)))
    yb = _s5_direction(u.reshape(batch, seq, dm), lre[1], lim[1], ldt[1], bre[1], bim[1],
                       cre[1], cim[1], reverse=True)
    m_mat, w_mat, v_mat, lam = _ssm_prep(lre, lim, ldt, bre.at[1].set(0.0), bim.at[1].set(0.0),
                                         cre, cim, d, batch)
    yg = _ssm_core(u, m_mat, w_mat, v_mat, lam, yb.reshape(m, dm), seq // SSM_CHUNK, batch)
    return _glu(yg, w_glu, tm=_pick(m, (2048, 1024, 512, 256, 128)), tn=_pick(dm, (256, 128)))


def _recurrence_combine(left, right):
    a_l, b_l = left
    a_r, b_r = right
    return a_r * a_l, a_r * b_l + b_r


def _s5_direction(u, lam_re, lam_im, log_dt, b_re, b_im, c_re, c_im, reverse):
    bsz, seq, dm = u.shape
    g, p = lam_re.shape
    ug = u.reshape(bsz, seq, g, dm // g)
    lam = lax.complex(lam_re, lam_im)
    dt = jnp.exp(log_dt)[:, None]
    lam_bar = jnp.exp(lam * dt)
    b_bar = ((lam_bar - 1.0) / lam)[..., None] * lax.complex(b_re, b_im)
    c_mat = lax.complex(c_re, c_im)
    bu = jnp.einsum('blgh,gph->blgp', ug, b_bar)
    a = jnp.broadcast_to(lam_bar, (1, seq, g, p))
    _, states = lax.associative_scan(_recurrence_combine, (a, bu), axis=1, reverse=reverse)
    return jnp.real(jnp.einsum('blgp,ghp->blgh', states, c_mat)).reshape(bsz, seq, dm)


def kernel(x, rel_bias, pre_mix_norm, post_mix_norm, pre_ffn_norm, post_ffn_norm, attn_wqkv, attn_sink, attn_wo, ssm_w_in, ssm_lambda_re, ssm_lambda_im, ssm_log_dt, ssm_b_re, ssm_b_im, ssm_c_re, ssm_c_im, ssm_d, ssm_w_glu, ffn_w_gate, ffn_w_up, ffn_conv_w, ffn_conv_b, ffn_w_down):
    batch, seq, dm = x.shape
    depth = pre_mix_norm.shape[0]
    m = batch * seq
    heads = dm // HEAD_DIM
    kvh = (attn_wqkv.shape[2] // HEAD_DIM - heads) // 2
    dff = ffn_w_gate.shape[2]
    assert seq % ATTN_BLOCK == 0 and seq % SSM_CHUNK == 0 and (2 * batch) % 8 == 0
    tm_big = _pick(m, (2048, 1024, 512, 256, 128))
    tm_mid = _pick(m, (1024, 512, 256, 128))

    xf = x.reshape(m, dm)
    h = _norm_cast(xf, pre_mix_norm[0])
    bias = _attn_bias(rel_bias)
    for i in range(depth):
        j = i // 2
        if i % 2 == 0:
            qkv = _mm(h, attn_wqkv[j], BF16, tm=tm_big, tn=_pick(attn_wqkv.shape[2], (512, 256, 128)))
            o = _attention(qkv, bias, attn_sink[j], seq, heads, kvh)
            mix = _mm(o, attn_wo[j], F32, tm=tm_big, tn=_pick(dm, (512, 256, 128)))
        else:
            mix = _s5_mixer(h, ssm_w_in[j], ssm_lambda_re[j], ssm_lambda_im[j], ssm_log_dt[j],
                            ssm_b_re[j], ssm_b_im[j], ssm_c_re[j], ssm_c_im[j], ssm_d[j],
                            ssm_w_glu[j], batch, seq)
        xf, h = _resid_norm(xf, mix, post_mix_norm[i], pre_ffn_norm[i])
        hid = _ffn_in(h, ffn_w_gate[i], ffn_w_up[i], ffn_conv_w[i], ffn_conv_b[i], seq,
                      tn=_pick(dff, (256, 128)))
        kc = dff // 2 if (dff // 2) % 128 == 0 else dff
        f = _mm(hid, ffn_w_down[i], F32, tm=tm_mid, tn=_pick(dm, (256, 128)), kc=kc,
                vmem_mib=VMEM_MAX_MIB)
        g_next = pre_mix_norm[i + 1] if i + 1 < depth else None
        xf, h = _resid_norm(xf, f, post_ffn_norm[i], g_next)
    return xf.reshape(batch, seq, dm)
```

```python
import functools
import math

import jax
import jax.numpy as jnp
from jax import lax
from jax.experimental import pallas as pl
from jax.experimental.pallas import tpu as pltpu

F32 = jnp.float32
BF16 = jnp.bfloat16

HEAD_DIM = 128
ATTN_BLOCK = 128
NUM_BUCKETS = 32
SSM_GROUP_CH = 16
SSM_STATE = 64
SSM_CHUNK = 16
SSM_TILE_GROUPS = 8
RMS_EPS = 1e-6
NEG_INF = -1e30

MIB = 1024 * 1024
VMEM_STREAM_MIB = 32
VMEM_PANEL_MIB = 56
VMEM_MAX_MIB = 60


def _cparams(n_grid_dims, vmem_mib):
    return pltpu.CompilerParams(
        dimension_semantics=("arbitrary",) * n_grid_dims,
        vmem_limit_bytes=vmem_mib * MIB,
    )


def _pick(n, prefs):
    for p in prefs:
        if p <= n and n % p == 0:
            return p
    return n


def _rms(x, g):
    return x * lax.rsqrt(jnp.mean(x * x, axis=-1, keepdims=True) + RMS_EPS) * g


def _norm_kernel(x_ref, g_ref, h_ref):
    h_ref[...] = _rms(x_ref[...], g_ref[...]).astype(h_ref.dtype)


def _norm_cast(x, g):
    m, d = x.shape
    tm = _pick(m, (256, 128, 64, 32, 16, 8))
    return pl.pallas_call(
        _norm_kernel,
        grid=(m // tm,),
        in_specs=[pl.BlockSpec((tm, d), lambda i: (i, 0)),
                  pl.BlockSpec((1, d), lambda i: (0, 0))],
        out_specs=pl.BlockSpec((tm, d), lambda i: (i, 0)),
        out_shape=jax.ShapeDtypeStruct((m, d), BF16),
        compiler_params=_cparams(1, VMEM_STREAM_MIB),
        name="norm_cast",
    )(x, g.reshape(1, d))


def _resid_norm_kernel(x_ref, m_ref, g1_ref, g2_ref, xo_ref, ho_ref):
    xn = x_ref[...] + _rms(m_ref[...], g1_ref[...])
    xo_ref[...] = xn
    ho_ref[...] = _rms(xn, g2_ref[...]).astype(ho_ref.dtype)


def _resid_kernel(x_ref, m_ref, g1_ref, xo_ref):
    xo_ref[...] = x_ref[...] + _rms(m_ref[...], g1_ref[...])


def _resid_norm(x, mix, g_post, g_next):
    m, d = x.shape
    tm = _pick(m, (128, 64, 32, 16, 8))
    row = pl.BlockSpec((tm, d), lambda i: (i, 0))
    vec = pl.BlockSpec((1, d), lambda i: (0, 0))
    if g_next is None:
        return pl.pallas_call(
            _resid_kernel, grid=(m // tm,),
            in_specs=[row, row, vec], out_specs=row,
            out_shape=jax.ShapeDtypeStruct((m, d), F32),
            compiler_params=_cparams(1, VMEM_STREAM_MIB), name="resid",
        )(x, mix, g_post.reshape(1, d)), None
    return pl.pallas_call(
        _resid_norm_kernel, grid=(m // tm,),
        in_specs=[row, row, vec, vec], out_specs=[row, row],
        out_shape=[jax.ShapeDtypeStruct((m, d), F32), jax.ShapeDtypeStruct((m, d), BF16)],
        compiler_params=_cparams(1, VMEM_STREAM_MIB), name="resid_norm",
    )(x, mix, g_post.reshape(1, d), g_next.reshape(1, d))


def _fetch_row_panel(a_hbm, a_ref, sem):
    @pl.when(pl.program_id(1) == 0)
    def _():
        tm = a_ref.shape[0]
        cp = pltpu.make_async_copy(
            a_hbm.at[pl.ds(pl.multiple_of(pl.program_id(0) * tm, tm), tm), :], a_ref, sem)
        cp.start()
        cp.wait()


def _panel_scratch(tm, k):
    return [pltpu.VMEM((tm, k), BF16), pltpu.SemaphoreType.DMA(())]


def _mm_kernel(a_hbm, w_ref, o_ref, a_ref, sem, *, kc):
    _fetch_row_panel(a_hbm, a_ref, sem)
    k = a_ref.shape[1]
    acc = None
    for k0 in range(0, k, kc):
        part = jnp.dot(a_ref[:, k0:k0 + kc], w_ref[k0:k0 + kc, :].astype(BF16),
                       preferred_element_type=F32)
        acc = part if acc is None else acc + part
    o_ref[...] = acc.astype(o_ref.dtype)


def _mm(a, w, out_dtype, tm, tn, kc=None, vmem_mib=VMEM_PANEL_MIB):
    m, k = a.shape
    n = w.shape[1]
    kc = k if kc is None else kc
    return pl.pallas_call(
        functools.partial(_mm_kernel, kc=kc),
        grid=(m // tm, n // tn),
        in_specs=[pl.BlockSpec(memory_space=pl.ANY),
                  pl.BlockSpec((k, tn), lambda i, j: (0, j))],
        out_specs=pl.BlockSpec((tm, tn), lambda i, j: (i, j)),
        out_shape=jax.ShapeDtypeStruct((m, n), out_dtype),
        scratch_shapes=_panel_scratch(tm, k),
        compiler_params=_cparams(2, vmem_mib),
        name="mm",
    )(a, w)


def _glu_kernel(a_hbm, wa_ref, wb_ref, o_ref, a_ref, sem):
    _fetch_row_panel(a_hbm, a_ref, sem)
    a = a_ref[...]
    ya = jnp.dot(a, wa_ref[...].astype(BF16), preferred_element_type=F32)
    yb = jnp.dot(a, wb_ref[...].astype(BF16), preferred_element_type=F32)
    o_ref[...] = (ya * jax.nn.sigmoid(yb)).astype(o_ref.dtype)


def _glu(a, w, tm, tn):
    m, k = a.shape
    n = w.shape[1] // 2
    nj = n // tn
    return pl.pallas_call(
        _glu_kernel,
        grid=(m // tm, nj),
        in_specs=[pl.BlockSpec(memory_space=pl.ANY),
                  pl.BlockSpec((k, tn), lambda i, j: (0, j)),
                  pl.BlockSpec((k, tn), lambda i, j: (0, j + nj))],
        out_specs=pl.BlockSpec((tm, tn), lambda i, j: (i, j)),
        out_shape=jax.ShapeDtypeStruct((m, n), F32),
        scratch_shapes=_panel_scratch(tm, k),
        compiler_params=_cparams(2, VMEM_PANEL_MIB),
        name="glu",
    )(a, w, w)


def _ffn_in_kernel(a_hbm, wg_ref, wu_ref, cw_ref, cb_ref, o_ref, a_ref, sem):
    _fetch_row_panel(a_hbm, a_ref, sem)
    a = a_ref[...]
    g = jnp.dot(a, wg_ref[...].astype(BF16), preferred_element_type=F32)
    u = jnp.dot(a, wu_ref[...].astype(BF16), preferred_element_type=F32)
    rows = g.shape[0]
    row = lax.broadcasted_iota(jnp.int32, (rows, 1), 0)
    g_prev = jnp.where(row == 0, 0.0, pltpu.roll(g, 1, 0))
    g_next = jnp.where(row == rows - 1, 0.0, pltpu.roll(g, rows - 1, 0))
    gc = cw_ref[0:1, :] * g_prev + cw_ref[1:2, :] * g + cw_ref[2:3, :] * g_next + cb_ref[...]
    o_ref[...] = (gc * jax.nn.sigmoid(gc) * u).astype(o_ref.dtype)


def _ffn_in(h, w_gate, w_up, conv_w, conv_b, seq, tn):
    m, k = h.shape
    f = w_gate.shape[1]
    return pl.pallas_call(
        _ffn_in_kernel,
        grid=(m // seq, f // tn),
        in_specs=[pl.BlockSpec(memory_space=pl.ANY),
                  pl.BlockSpec((k, tn), lambda i, j: (0, j)),
                  pl.BlockSpec((k, tn), lambda i, j: (0, j)),
                  pl.BlockSpec((3, tn), lambda i, j: (0, j)),
                  pl.BlockSpec((1, tn), lambda i, j: (0, j))],
        out_specs=pl.BlockSpec((seq, tn), lambda i, j: (i, j)),
        out_shape=jax.ShapeDtypeStruct((m, f), BF16),
        scratch_shapes=_panel_scratch(seq, k),
        compiler_params=_cparams(2, VMEM_PANEL_MIB),
        name="ffn_in",
    )(h, w_gate, w_up, conv_w, conv_b.reshape(1, f))


def _t5_bucket(rel):
    half = NUM_BUCKETS // 2
    max_exact = half // 2
    base = jnp.where(rel > 0, half, 0)
    n = jnp.abs(rel)
    nf = jnp.maximum(n, 1).astype(F32)
    large = max_exact + (jnp.log(nf / max_exact) / math.log(ATTN_BLOCK / max_exact)
                         * (half - max_exact)).astype(jnp.int32)
    large = jnp.minimum(large, half - 1)
    return base + jnp.where(n < max_exact, n, large)


def _bias_kernel(bucket_ref, inwin_ref, rbt_ref, o_ref):
    nb = rbt_ref.shape[1]
    lanes = bucket_ref.shape[1]
    onehot = (lax.broadcasted_iota(jnp.int32, (nb, lanes), 0) == bucket_ref[...]).astype(F32)
    bias = jnp.dot(rbt_ref[...], onehot, preferred_element_type=F32,
                   precision=lax.Precision.HIGHEST)
    o_ref[...] = jnp.where(inwin_ref[...] > 0, bias, NEG_INF)


def _attn_bias(rel_bias):
    nb, heads = rel_bias.shape
    blk = ATTN_BLOCK
    q_idx = jnp.arange(blk)[:, None]
    k_idx = jnp.arange(3 * blk)[None, :]
    rel = k_idx - blk - q_idx
    bucket = _t5_bucket(rel).reshape(1, 3 * blk * blk).astype(jnp.int32)
    inwin = (jnp.abs(rel) <= blk).astype(jnp.int32).reshape(1, 3 * blk * blk)
    tl = 4096
    out = pl.pallas_call(
        _bias_kernel,
        grid=(3 * blk * blk // tl,),
        in_specs=[pl.BlockSpec((1, tl), lambda i: (0, i)),
                  pl.BlockSpec((1, tl), lambda i: (0, i)),
                  pl.BlockSpec((heads, nb), lambda i: (0, 0))],
        out_specs=pl.BlockSpec((heads, tl), lambda i: (0, i)),
        out_shape=jax.ShapeDtypeStruct((heads, 3 * blk * blk), F32),
        compiler_params=_cparams(1, VMEM_STREAM_MIB),
        name="attn_bias",
    )(bucket, inwin, rel_bias.T)
    return out.reshape(heads, blk, 3 * blk)


def _attn_kernel(sink_ref, q_ref, kp_ref, ko_ref, kn_ref, vp_ref, vo_ref, vn_ref, bias_ref,
                 o_ref, *, nblk, kvh, grp):
    blk, hd = ATTN_BLOCK, HEAD_DIM
    n = pl.program_id(0) % nblk
    col = lax.broadcasted_iota(jnp.int32, (1, 3 * blk), 1)
    key_pos = (n - 1) * blk + col
    edge = jnp.where((key_pos >= 0) & (key_pos < nblk * blk), 0.0, NEG_INF)
    scale = hd ** -0.5
    heads = [[kh * grp + g for g in range(grp)] for kh in range(kvh)]
    scores = []
    for kh in range(kvh):
        ks = slice(kh * hd, (kh + 1) * hd)
        k = jnp.concatenate([kp_ref[:, ks], ko_ref[:, ks], kn_ref[:, ks]], axis=0)
        q = jnp.concatenate([q_ref[:, h * hd:(h + 1) * hd] for h in heads[kh]], axis=0)
        s = lax.dot_general(q, k, (((1,), (1,)), ((), ())), preferred_element_type=F32) * scale
        scores.append(s + bias_ref[kh * grp:(kh + 1) * grp].reshape(grp * blk, 3 * blk) + edge)
    probs, denoms = [], []
    for kh in range(kvh):
        s = scores[kh]
        sink = jnp.concatenate([jnp.full((blk, 1), sink_ref[h], F32) for h in heads[kh]], axis=0)
        mx = jnp.maximum(jnp.max(s, axis=-1, keepdims=True), sink)
        p = jnp.exp(s - mx)
        denoms.append(jnp.sum(p, axis=-1, keepdims=True) + jnp.exp(sink - mx))
        probs.append(p.astype(BF16))
    for kh in range(kvh):
        ks = slice(kh * hd, (kh + 1) * hd)
        v = jnp.concatenate([vp_ref[:, ks], vo_ref[:, ks], vn_ref[:, ks]], axis=0)
        o = jnp.dot(probs[kh], v, preferred_element_type=F32) / denoms[kh]
        for g, h in enumerate(heads[kh]):
            o_ref[:, h * hd:(h + 1) * hd] = o[g * blk:(g + 1) * blk].astype(o_ref.dtype)


def _attention(qkv, bias, sink, seq, heads, kvh):
    m = qkv.shape[0]
    blk, hd = ATTN_BLOCK, HEAD_DIM
    nblk = seq // blk
    grp = heads // kvh
    qw, kw = heads * hd, kvh * hd
    kcol, vcol = qw // kw, qw // kw + 1

    def prev(i):
        return jnp.where(i % nblk == 0, i, i - 1)

    def nxt(i):
        return jnp.where(i % nblk == nblk - 1, i, i + 1)

    kv = lambda rowf, colb: pl.BlockSpec((blk, kw), lambda i: (rowf(i), colb))
    same = lambda i: i
    return pl.pallas_call(
        functools.partial(_attn_kernel, nblk=nblk, kvh=kvh, grp=grp),
        grid=(m // blk,),
        in_specs=[pl.BlockSpec(memory_space=pltpu.SMEM),
                  pl.BlockSpec((blk, qw), lambda i: (i, 0)),
                  kv(prev, kcol), kv(same, kcol), kv(nxt, kcol),
                  kv(prev, vcol), kv(same, vcol), kv(nxt, vcol),
                  pl.BlockSpec((heads, blk, 3 * blk), lambda i: (0, 0, 0))],
        out_specs=pl.BlockSpec((blk, qw), lambda i: (i, 0)),
        out_shape=jax.ShapeDtypeStruct((m, qw), BF16),
        compiler_params=_cparams(1, VMEM_PANEL_MIB),
        name="attention",
    )(sink, qkv, qkv, qkv, qkv, qkv, qkv, qkv, bias)


def _cexp(zr, zi):
    mag = jnp.exp(zr)
    return mag * jnp.cos(zi), mag * jnp.sin(zi)


def _cpow_int(br, bi, e, nbits):
    res_r = res_i = None
    for bit in range(nbits):
        on = ((e >> bit) & 1) == 1
        fr = jnp.where(on, br, 1.0)
        fi = jnp.where(on, bi, 0.0)
        if res_r is None:
            res_r, res_i = fr, fi
        else:
            res_r, res_i = res_r * fr - res_i * fi, res_r * fi + res_i * fr
        if bit + 1 < nbits:
            br, bi = br * br - bi * bi, 2.0 * br * bi
    return res_r, res_i


def _slot_sources(rot, t):
    half = t // 2
    return [c * half + (s - rot) % half for c in range(2) for s in range(half)]


def _rotate_slots(x, rot):
    if rot == 0:
        return x
    h = x.shape[1] // 2
    sh = SSM_GROUP_CH * rot
    return jnp.concatenate([pltpu.roll(x[:, :h], sh, 1), pltpu.roll(x[:, h:], sh, 1)], axis=1)


def _permute_row_blocks(x, rot, t):
    hch = SSM_GROUP_CH
    return jnp.concatenate([x[hch * j:hch * (j + 1)] for j in _slot_sources(rot, t)], axis=0)


def _ssm_prep_group(g, rot, lre_r, lim_r, ldt_r, btr, bti, ctr, cti, dcol):
    t, hch, p = SSM_CHUNK, SSM_GROUP_CH, SSM_STATE
    th, p2 = t * hch, 2 * p
    hi_prec = lax.Precision.HIGHEST
    ar, ai = lre_r[g], lim_r[g]
    dt = jnp.exp(ldt_r[g])
    zr, zi = ar * dt, ai * dt
    lbr, lbi = _cexp(zr, zi)
    nr = lbr - 1.0
    den = ar * ar + ai * ai
    cr = (nr * ar + lbi * ai) / den
    ci = (lbi * ar - nr * ai) / den
    b_r, b_i = btr[g], bti[g]
    bbr = cr * b_r - ci * b_i
    bbi = cr * b_i + ci * b_r
    row = lax.broadcasted_iota(jnp.int32, (th, p2), 0)
    lane = lax.broadcasted_iota(jnp.int32, (th, p2), 1)
    j = row // hch
    nbits = (t - 1).bit_length()
    pr, pi = _cpow_int(lbr, lbi, jnp.where(lane < p, t - 1 - j, j), nbits)
    bt_r = jnp.concatenate([bbr] * t, axis=0)
    bt_i = jnp.concatenate([bbi] * t, axis=0)
    w_mat = jnp.concatenate([pr * bt_r - pi * bt_i, pr * bt_i + pi * bt_r], axis=1)
    l_r, l_i = lbr, lbi
    for _ in range(nbits):
        l_r, l_i = l_r * l_r - l_i * l_i, 2.0 * l_r * l_i
    lam = jnp.concatenate([l_r, l_i], axis=1)
    rowc = lax.broadcasted_iota(jnp.int32, (p2, th), 0)
    lanec = lax.broadcasted_iota(jnp.int32, (p2, th), 1)
    nn = lanec // hch
    eye = (lax.broadcasted_iota(jnp.int32, (p2, p2), 0)
           == lax.broadcasted_iota(jnp.int32, (p2, p2), 1)).astype(F32)
    to_col = lambda r: lax.dot_general(eye, r, (((1,), (1,)), ((), ())),
                                       preferred_element_type=F32, precision=hi_prec)
    lbrc, lbic = to_col(lbr), to_col(lbi)
    qr, qi = _cpow_int(lbrc, lbic, jnp.where(rowc < p, nn, t - 1 - nn), nbits)
    tile = (lax.broadcasted_iota(jnp.int32, (hch, th), 1) % hch
            == lax.broadcasted_iota(jnp.int32, (hch, th), 0)).astype(F32)
    c_r = jnp.dot(ctr[g], tile, preferred_element_type=F32, precision=hi_prec)
    c_i = jnp.dot(cti[g], tile, preferred_element_type=F32, precision=hi_prec)
    e_r = c_r * qr - c_i * qi
    e_i = c_r * qi + c_i * qr
    v_mat = jnp.concatenate([e_r * lbrc - e_i * lbic, -(e_r * lbic + e_i * lbrc)], axis=0)
    fwd_lane = lax.broadcasted_iota(jnp.int32, (hch, p2), 1) < p
    rhs = jnp.concatenate([e_r, e_i], axis=0)
    lhs0 = jnp.concatenate([jnp.where(fwd_lane, bbr, 0.0), -jnp.where(fwd_lane, bbi, 0.0)], axis=1)
    lhs1 = jnp.concatenate([jnp.where(fwd_lane, 0.0, bbr), -jnp.where(fwd_lane, 0.0, bbi)], axis=1)
    k0 = jnp.dot(lhs0, rhs, preferred_element_type=F32, precision=hi_prec)
    k1 = jnp.dot(lhs1, rhs, preferred_element_type=F32, precision=hi_prec)
    lane_m = lax.broadcasted_iota(jnp.int32, (hch, th), 1)
    row_m = lax.broadcasted_iota(jnp.int32, (hch, th), 0)
    d_g = dcol[g]
    blocks = []
    for jj in range(t):
        a = pltpu.roll(k0, hch * jj, 1) if jj else k0
        a = jnp.where(lane_m >= hch * jj, a, 0.0)
        sh = (hch * (jj + 1)) % th
        b = pltpu.roll(k1, sh, 1) if sh else k1
        b = jnp.where(lane_m < hch * (jj + 1), b, 0.0)
        dd = jnp.where(lane_m == hch * jj + row_m, d_g, 0.0)
        blocks.append(a + b + dd)
    m_mat = jnp.concatenate(blocks, axis=0)
    m_mat = _permute_row_blocks(_rotate_slots(m_mat, rot), rot, t)
    w_mat = _permute_row_blocks(w_mat, rot, t)
    v_mat = _rotate_slots(v_mat, rot)
    return m_mat, w_mat, v_mat, lam


def _ssm_prep_kernel(lre_r, lim_r, ldt_r, btr, bti, ctr, cti, dcol,
                     m_ref, w_ref, v_ref, lam_ref, *, groups, nb):
    ins = (lre_r, lim_r, ldt_r, btr, bti, ctr, cti, dcol)
    first = lax.broadcasted_iota(jnp.int32, (2 * nb, 4 * SSM_STATE), 0) < nb
    lams = []
    for g in range(groups):
        m_mat, w_mat, v_mat, lam = _ssm_prep_group(g, g % SSM_TILE_GROUPS, *ins)
        m_ref[g] = m_mat.astype(m_ref.dtype)
        w_ref[g] = w_mat.astype(w_ref.dtype)
        v_ref[g] = v_mat.astype(v_ref.dtype)
        lams.append(jnp.broadcast_to(lam, (2 * nb, 4 * SSM_STATE)))
        if g % 2 == 1:
            lam_ref[g // 2] = jnp.where(first, lams[g - 1], lams[g])


def _ssm_prep(lre, lim, ldt, bre, bim, cre, cim, d, nb):
    _, g, p = lre.shape
    hch, t = SSM_GROUP_CH, SSM_CHUNK
    th = t * hch
    cat = lambda a: jnp.concatenate([a[0], a[1]], axis=-1)
    lre2, lim2 = cat(lre), cat(lim)
    ldt2 = jnp.repeat(ldt.T, p, axis=1)
    bt = lambda a: jnp.transpose(a, (1, 3, 0, 2)).reshape(g, hch, 2 * p)
    ct = lambda a: jnp.transpose(a, (1, 0, 3, 2)).reshape(g, 2 * p, hch)
    gp = SSM_TILE_GROUPS
    assert g % gp == 0
    rowv = pl.BlockSpec((gp, 1, 2 * p), lambda i: (i, 0, 0))
    mat = pl.BlockSpec((gp, th, th), lambda i: (i, 0, 0))
    return pl.pallas_call(
        functools.partial(_ssm_prep_kernel, groups=gp, nb=nb),
        grid=(g // gp,),
        in_specs=[rowv, rowv, rowv,
                  pl.BlockSpec((gp, hch, 2 * p), lambda i: (i, 0, 0)),
                  pl.BlockSpec((gp, hch, 2 * p), lambda i: (i, 0, 0)),
                  pl.BlockSpec((gp, 2 * p, hch), lambda i: (i, 0, 0)),
                  pl.BlockSpec((gp, 2 * p, hch), lambda i: (i, 0, 0)),
                  pl.BlockSpec((gp, hch, 1), lambda i: (i, 0, 0))],
        out_specs=[mat, mat, mat,
                   pl.BlockSpec((gp // 2, 2 * nb, 4 * p), lambda i: (i, 0, 0))],
        out_shape=[jax.ShapeDtypeStruct((g, th, th), BF16)] * 3
        + [jax.ShapeDtypeStruct((g // 2, 2 * nb, 4 * p), F32)],
        compiler_params=_cparams(1, VMEM_STREAM_MIB),
        name="ssm_prep",
    )(lre2.reshape(g, 1, 2 * p), lim2.reshape(g, 1, 2 * p), ldt2.reshape(g, 1, 2 * p),
      bt(bre), bt(bim), ct(cre), ct(cim), d.reshape(g, hch, 1))


def _ssm_kernel(u_ref, m_ref, w_ref, v_ref, lam_ref, yext_ref, y_ref, yin_s, s_s, xf_s, xb_s,
                tok_s, u2_s, *, pb, nchunk, nb):
    slab = 2 * nb
    rows = nchunk * slab
    p2 = 2 * SSM_STATE
    hch, tch = SSM_GROUP_CH, SSM_CHUNK
    half = tch * hch // 2
    lanes = 2 * pb * hch
    slots = lanes // hch
    first = (lax.broadcasted_iota(jnp.int32, (rows, 1), 0) // nb) % 2 == 0
    lane_grp = lax.broadcasted_iota(jnp.int32, (nchunk, lanes), 1) // hch

    def pair_dot(lhs, mats, q):
        ya = jnp.dot(lhs, mats[2 * q], preferred_element_type=F32)
        yb = jnp.dot(lhs, mats[2 * q + 1], preferred_element_type=F32)
        return jnp.where(first, ya, yb)

    for b in range(nb):
        for tq in range(tch):
            tok_s[pl.ds((b * tch + tq) * nchunk, nchunk), :] = (
                u_ref[pl.ds(b * nchunk * tch + tq, nchunk, stride=tch), :])
    for b in range(nb):
        for col in range(2):
            rolled = []
            for j in range(slots):
                piece = tok_s[pl.ds((b * tch + col * slots + j) * nchunk, nchunk), :]
                rolled.append(pltpu.roll(piece, hch * j, 1) if j else piece)
            for gl in range(slots):
                acc = rolled[0]
                for j in range(1, slots):
                    acc = jnp.where(lane_grp == (j + gl) % slots, rolled[j], acc)
                q, g2 = divmod(gl, 2)
                u2_s[q, col, pl.ds(g2 * nb + b, nchunk, stride=slab), :] = acc

    for q in range(pb):
        u = jnp.concatenate([u2_s[q, 0], u2_s[q, 1]], axis=1).astype(BF16)
        y0 = pair_dot(u, m_ref, q)
        yin_s[q, 0] = y0[:, :half]
        yin_s[q, 1] = y0[:, half:]
        s_s[q] = pair_dot(u, w_ref, q)

    fwd = lax.broadcasted_iota(jnp.int32, (slab, p2), 1) < SSM_STATE
    lam_r = [lam_ref[q, :, 0:p2] for q in range(pb)]
    lam_i = [lam_ref[q, :, p2:2 * p2] for q in range(pb)]

    def step(k, carry):
        kf = pl.multiple_of(k * slab, slab)
        kb = pl.multiple_of((nchunk - 1 - k) * slab, slab)
        out = []
        for q in range(pb):
            xr, xi = carry[2 * q], carry[2 * q + 1]
            xf_s[q, pl.ds(kf, slab), 0:p2] = xr
            xf_s[q, pl.ds(kf, slab), p2:2 * p2] = xi
            xb_s[q, pl.ds(kb, slab), 0:p2] = xr
            xb_s[q, pl.ds(kb, slab), p2:2 * p2] = xi
            sr = jnp.where(fwd, s_s[q, pl.ds(kf, slab), 0:p2], s_s[q, pl.ds(kb, slab), 0:p2])
            si = jnp.where(fwd, s_s[q, pl.ds(kf, slab), p2:2 * p2],
                           s_s[q, pl.ds(kb, slab), p2:2 * p2])
            out.append(lam_r[q] * xr - lam_i[q] * xi + sr)
            out.append(lam_r[q] * xi + lam_i[q] * xr + si)
        return tuple(out)

    zero = jnp.zeros((slab, p2), F32)
    lax.fori_loop(0, nchunk, step, (zero,) * (2 * pb))

    fwd_all = lax.broadcasted_iota(jnp.int32, (rows, 2 * p2), 1) % p2 < SSM_STATE
    for q in range(pb):
        x = jnp.where(fwd_all, xf_s[q], xb_s[q]).astype(BF16)
        yv = pair_dot(x, v_ref, q)
        yin_s[q, 0] = yin_s[q, 0] + yv[:, :half]
        yin_s[q, 1] = yin_s[q, 1] + yv[:, half:]

    for q in range(pb):
        for col in range(2):
            for r in range(slab):
                u2_s[q, col, pl.ds(r * nchunk, nchunk), :] = (
                    yin_s[q, col, pl.ds(r, nchunk, stride=slab), :])
    for b in range(nb):
        for tq in range(tch):
            col, i = divmod(tq, slots)
            acc = None
            for gl in range(slots):
                q, g2 = divmod(gl, 2)
                piece = u2_s[q, col, pl.ds((g2 * nb + b) * nchunk, nchunk), :]
                acc = piece if acc is None else jnp.where(lane_grp == (i + gl) % slots, piece, acc)
            if i:
                acc = pltpu.roll(acc, (-hch * i) % lanes, 1)
            tok_s[pl.ds(b * nchunk * tch + tq, nchunk, stride=tch), :] = acc
    y_ref[...] = jax.nn.gelu(tok_s[...] + yext_ref[...]).astype(y_ref.dtype)


def _ssm_core(u, m_mat, w_mat, v_mat, lam, yext, nchunk, nb):
    m, dm = u.shape
    th = SSM_CHUNK * SSM_GROUP_CH
    npair = dm // (2 * SSM_GROUP_CH)
    rows = nchunk * 2 * nb
    pb = 4
    assert npair % pb == 0 and 2 * pb * SSM_GROUP_CH == 128 and th == 256
    p4 = 4 * SSM_STATE
    mat = pl.BlockSpec((2 * pb, th, th), lambda i: (i, 0, 0))
    tok = pl.BlockSpec((m, 128), lambda i: (0, i))
    return pl.pallas_call(
        functools.partial(_ssm_kernel, pb=pb, nchunk=nchunk, nb=nb),
        grid=(npair // pb,),
        in_specs=[tok, mat, mat, mat, pl.BlockSpec((pb, 2 * nb, p4), lambda i: (i, 0, 0)), tok],
        out_specs=tok,
        out_shape=jax.ShapeDtypeStruct((m, dm), BF16),
        scratch_shapes=[pltpu.VMEM((pb, 2, rows, th // 2), F32), pltpu.VMEM((pb, rows, p4), F32),
                        pltpu.VMEM((pb, rows, p4), F32), pltpu.VMEM((pb, rows, p4), F32),
                        pltpu.VMEM((m, 128), F32), pltpu.VMEM((pb, 2, rows, th // 2), F32)],
        compiler_params=_cparams(1, VMEM_MAX_MIB),
        name="ssm_core",
    )(u, m_mat, w_mat, v_mat, lam, yext)


def _s5_mixer(h, w_in, lre, lim, ldt, bre, bim, cre, cim, d, w_glu, batch, seq):
    m, dm = h.shape
    u = _mm(h, w_in, F32, tm=_pick(m, (2048, 1024, 512, 256, 128)), tn=_pick(dm, (256, 128)))
    yb = _s5_direction(u.reshape(batch, seq, dm), lre[1], lim[1], ldt[1], bre[1], bim[1],
                       cre[1], cim[1], reverse=True)
    m_mat, w_mat, v_mat, lam = _ssm_prep(lre, lim, ldt, bre.at[1].set(0.0), bim.at[1].set(0.0),
                                         cre, cim, d, batch)
    yg = _ssm_core(u, m_mat, w_mat, v_mat, lam, yb.reshape(m, dm), seq // SSM_CHUNK, batch)
    return _glu(yg, w_glu, tm=_pick(m, (2048, 1024, 512, 256, 128)), tn=_pick(dm, (256, 128)))


def _recurrence_combine(left, right):
    a_l, b_l = left
    a_r, b_r = right
    return a_r * a_l, a_r * b_l + b_r


def _s5_direction(u, lam_re, lam_im, log_dt, b_re, b_im, c_re, c_im, reverse):
    bsz, seq, dm = u.shape
    g, p = lam_re.shape
    ug = u.reshape(bsz, seq, g, dm // g)
    lam = lax.complex(lam_re, lam_im)
    dt = jnp.exp(log_dt)[:, None]
    lam_bar = jnp.exp(lam * dt)
    b_bar = ((lam_bar - 1.0) / lam)[..., None] * lax.complex(b_re, b_im)
    c_mat = lax.complex(c_re, c_im)
    bu = jnp.einsum('blgh,gph->blgp', ug, b_bar)
    a = jnp.broadcast_to(lam_bar, (1, seq, g, p))
    _, states = lax.associative_scan(_recurrence_combine, (a, bu), axis=1, reverse=reverse)
    return jnp.real(jnp.einsum('blgp,ghp->blgh', states, c_mat)).reshape(bsz, seq, dm)


def kernel(x, rel_bias, pre_mix_norm, post_mix_norm, pre_ffn_norm, post_ffn_norm, attn_wqkv, attn_sink, attn_wo, ssm_w_in, ssm_lambda_re, ssm_lambda_im, ssm_log_dt, ssm_b_re, ssm_b_im, ssm_c_re, ssm_c_im, ssm_d, ssm_w_glu, ffn_w_gate, ffn_w_up, ffn_conv_w, ffn_conv_b, ffn_w_down):
    batch, seq, dm = x.shape
    depth = pre_mix_norm.shape[0]
    m = batch * seq
    heads = dm // HEAD_DIM
    kvh = (attn_wqkv.shape[2] // HEAD_DIM - heads) // 2
    dff = ffn_w_gate.shape[2]
    assert seq % ATTN_BLOCK == 0 and seq % SSM_CHUNK == 0 and (2 * batch) % 8 == 0
    tm_big = _pick(m, (2048, 1024, 512, 256, 128))
    tm_mid = _pick(m, (1024, 512, 256, 128))

    xf = x.reshape(m, dm)
    h = _norm_cast(xf, pre_mix_norm[0])
    bias = _attn_bias(rel_bias)
    for i in range(depth):
        j = i // 2
        if i % 2 == 0:
            qkv = _mm(h, attn_wqkv[j], BF16, tm=tm_big, tn=_pick(attn_wqkv.shape[2], (512, 256, 128)))
            o = _attention(qkv, bias, attn_sink[j], seq, heads, kvh)
            mix = _mm(o, attn_wo[j], F32, tm=tm_big, tn=_pick(dm, (512, 256, 128)))
        else:
            mix = _s5_mixer(h, ssm_w_in[j], ssm_lambda_re[j], ssm_lambda_im[j], ssm_log_dt[j],
                            ssm_b_re[j], ssm_b_im[j], ssm_c_re[j], ssm_c_im[j], ssm_d[j],
                            ssm_w_glu[j], batch, seq)
        xf, h = _resid_norm(xf, mix, post_mix_norm[i], pre_ffn_norm[i])
        hid = _ffn_in(h, ffn_w_gate[i], ffn_w_up[i], ffn_conv_w[i], ffn_conv_b[i], seq,
                      tn=_pick(dff, (256, 128)))
        kc = dff // 2 if (dff // 2) % 128 == 0 else dff
        f = _mm(hid, ffn_w_down[i], F32, tm=tm_mid, tn=_pick(dm, (256, 128)), kc=kc,
                vmem_mib=VMEM_MAX_MIB)
        g_next = pre_mix_norm[i + 1] if i + 1 < depth else None
        xf, h = _resid_norm(xf, f, post_ffn_norm[i], g_next)
    return xf.reshape(batch, seq, dm)
```

```python
import functools
import math

import jax
import jax.numpy as jnp
from jax import lax
from jax.experimental import pallas as pl
from jax.experimental.pallas import tpu as pltpu

F32 = jnp.float32
BF16 = jnp.bfloat16

HEAD_DIM = 128
ATTN_BLOCK = 128
NUM_BUCKETS = 32
SSM_GROUP_CH = 16
SSM_STATE = 64
SSM_CHUNK = 16
SSM_TILE_GROUPS = 8
RMS_EPS = 1e-6
NEG_INF = -1e30

MIB = 1024 * 1024
VMEM_STREAM_MIB = 32
VMEM_PANEL_MIB = 56
VMEM_MAX_MIB = 60


def _cparams(n_grid_dims, vmem_mib):
    return pltpu.CompilerParams(
        dimension_semantics=("arbitrary",) * n_grid_dims,
        vmem_limit_bytes=vmem_mib * MIB,
    )


def _pick(n, prefs):
    for p in prefs:
        if p <= n and n % p == 0:
            return p
    return n


def _rms(x, g):
    return x * lax.rsqrt(jnp.mean(x * x, axis=-1, keepdims=True) + RMS_EPS) * g


def _norm_kernel(x_ref, g_ref, h_ref):
    h_ref[...] = _rms(x_ref[...], g_ref[...]).astype(h_ref.dtype)


def _norm_cast(x, g):
    m, d = x.shape
    tm = _pick(m, (256, 128, 64, 32, 16, 8))
    return pl.pallas_call(
        _norm_kernel,
        grid=(m // tm,),
        in_specs=[pl.BlockSpec((tm, d), lambda i: (i, 0)),
                  pl.BlockSpec((1, d), lambda i: (0, 0))],
        out_specs=pl.BlockSpec((tm, d), lambda i: (i, 0)),
        out_shape=jax.ShapeDtypeStruct((m, d), BF16),
        compiler_params=_cparams(1, VMEM_STREAM_MIB),
        name="norm_cast",
    )(x, g.reshape(1, d))


def _resid_norm_kernel(x_ref, m_ref, g1_ref, g2_ref, xo_ref, ho_ref):
    xn = x_ref[...] + _rms(m_ref[...], g1_ref[...])
    xo_ref[...] = xn
    ho_ref[...] = _rms(xn, g2_ref[...]).astype(ho_ref.dtype)


def _resid_kernel(x_ref, m_ref, g1_ref, xo_ref):
    xo_ref[...] = x_ref[...] + _rms(m_ref[...], g1_ref[...])


def _resid_norm(x, mix, g_post, g_next):
    m, d = x.shape
    tm = _pick(m, (128, 64, 32, 16, 8))
    row = pl.BlockSpec((tm, d), lambda i: (i, 0))
    vec = pl.BlockSpec((1, d), lambda i: (0, 0))
    if g_next is None:
        return pl.pallas_call(
            _resid_kernel, grid=(m // tm,),
            in_specs=[row, row, vec], out_specs=row,
            out_shape=jax.ShapeDtypeStruct((m, d), F32),
            compiler_params=_cparams(1, VMEM_STREAM_MIB), name="resid",
        )(x, mix, g_post.reshape(1, d)), None
    return pl.pallas_call(
        _resid_norm_kernel, grid=(m // tm,),
        in_specs=[row, row, vec, vec], out_specs=[row, row],
        out_shape=[jax.ShapeDtypeStruct((m, d), F32), jax.ShapeDtypeStruct((m, d), BF16)],
        compiler_params=_cparams(1, VMEM_STREAM_MIB), name="resid_norm",
    )(x, mix, g_post.reshape(1, d), g_next.reshape(1, d))


def _fetch_row_panel(a_hbm, a_ref, sem):
    @pl.when(pl.program_id(1) == 0)
    def _():
        tm = a_ref.shape[0]
        cp = pltpu.make_async_copy(
            a_hbm.at[pl.ds(pl.multiple_of(pl.program_id(0) * tm, tm), tm), :], a_ref, sem)
        cp.start()
        cp.wait()


def _panel_scratch(tm, k):
    return [pltpu.VMEM((tm, k), BF16), pltpu.SemaphoreType.DMA(())]


def _mm_kernel(a_hbm, w_ref, o_ref, a_ref, sem, *, kc):
    _fetch_row_panel(a_hbm, a_ref, sem)
    k = a_ref.shape[1]
    acc = None
    for k0 in range(0, k, kc):
        part = jnp.dot(a_ref[:, k0:k0 + kc], w_ref[k0:k0 + kc, :].astype(BF16),
                       preferred_element_type=F32)
        acc = part if acc is None else acc + part
    o_ref[...] = acc.astype(o_ref.dtype)


def _mm(a, w, out_dtype, tm, tn, kc=None, vmem_mib=VMEM_PANEL_MIB):
    m, k = a.shape
    n = w.shape[1]
    kc = k if kc is None else kc
    return pl.pallas_call(
        functools.partial(_mm_kernel, kc=kc),
        grid=(m // tm, n // tn),
        in_specs=[pl.BlockSpec(memory_space=pl.ANY),
                  pl.BlockSpec((k, tn), lambda i, j: (0, j))],
        out_specs=pl.BlockSpec((tm, tn), lambda i, j: (i, j)),
        out_shape=jax.ShapeDtypeStruct((m, n), out_dtype),
        scratch_shapes=_panel_scratch(tm, k),
        compiler_params=_cparams(2, vmem_mib),
        name="mm",
    )(a, w)


def _glu_kernel(a_hbm, wa_ref, wb_ref, o_ref, a_ref, sem):
    _fetch_row_panel(a_hbm, a_ref, sem)
    a = a_ref[...]
    ya = jnp.dot(a, wa_ref[...].astype(BF16), preferred_element_type=F32)
    yb = jnp.dot(a, wb_ref[...].astype(BF16), preferred_element_type=F32)
    o_ref[...] = (ya * jax.nn.sigmoid(yb)).astype(o_ref.dtype)


def _glu(a, w, tm, tn):
    m, k = a.shape
    n = w.shape[1] // 2
    nj = n // tn
    return pl.pallas_call(
        _glu_kernel,
        grid=(m // tm, nj),
        in_specs=[pl.BlockSpec(memory_space=pl.ANY),
                  pl.BlockSpec((k, tn), lambda i, j: (0, j)),
                  pl.BlockSpec((k, tn), lambda i, j: (0, j + nj))],
        out_specs=pl.BlockSpec((tm, tn), lambda i, j: (i, j)),
        out_shape=jax.ShapeDtypeStruct((m, n), F32),
        scratch_shapes=_panel_scratch(tm, k),
        compiler_params=_cparams(2, VMEM_PANEL_MIB),
        name="glu",
    )(a, w, w)


def _ffn_in_kernel(a_hbm, wg_ref, wu_ref, cw_ref, cb_ref, o_ref, a_ref, sem):
    _fetch_row_panel(a_hbm, a_ref, sem)
    a = a_ref[...]
    g = jnp.dot(a, wg_ref[...].astype(BF16), preferred_element_type=F32)
    u = jnp.dot(a, wu_ref[...].astype(BF16), preferred_element_type=F32)
    rows = g.shape[0]
    row = lax.broadcasted_iota(jnp.int32, (rows, 1), 0)
    g_prev = jnp.where(row == 0, 0.0, pltpu.roll(g, 1, 0))
    g_next = jnp.where(row == rows - 1, 0.0, pltpu.roll(g, rows - 1, 0))
    gc = cw_ref[0:1, :] * g_prev + cw_ref[1:2, :] * g + cw_ref[2:3, :] * g_next + cb_ref[...]
    o_ref[...] = (gc * jax.nn.sigmoid(gc) * u).astype(o_ref.dtype)


def _ffn_in(h, w_gate, w_up, conv_w, conv_b, seq, tn):
    m, k = h.shape
    f = w_gate.shape[1]
    return pl.pallas_call(
        _ffn_in_kernel,
        grid=(m // seq, f // tn),
        in_specs=[pl.BlockSpec(memory_space=pl.ANY),
                  pl.BlockSpec((k, tn), lambda i, j: (0, j)),
                  pl.BlockSpec((k, tn), lambda i, j: (0, j)),
                  pl.BlockSpec((3, tn), lambda i, j: (0, j)),
                  pl.BlockSpec((1, tn), lambda i, j: (0, j))],
        out_specs=pl.BlockSpec((seq, tn), lambda i, j: (i, j)),
        out_shape=jax.ShapeDtypeStruct((m, f), BF16),
        scratch_shapes=_panel_scratch(seq, k),
        compiler_params=_cparams(2, VMEM_PANEL_MIB),
        name="ffn_in",
    )(h, w_gate, w_up, conv_w, conv_b.reshape(1, f))


def _t5_bucket(rel):
    half = NUM_BUCKETS // 2
    max_exact = half // 2
    base = jnp.where(rel > 0, half, 0)
    n = jnp.abs(rel)
    nf = jnp.maximum(n, 1).astype(F32)
    large = max_exact + (jnp.log(nf / max_exact) / math.log(ATTN_BLOCK / max_exact)
                         * (half - max_exact)).astype(jnp.int32)
    large = jnp.minimum(large, half - 1)
    return base + jnp.where(n < max_exact, n, large)


def _bias_kernel(bucket_ref, inwin_ref, rbt_ref, o_ref):
    nb = rbt_ref.shape[1]
    lanes = bucket_ref.shape[1]
    onehot = (lax.broadcasted_iota(jnp.int32, (nb, lanes), 0) == bucket_ref[...]).astype(F32)
    bias = jnp.dot(rbt_ref[...], onehot, preferred_element_type=F32,
                   precision=lax.Precision.HIGHEST)
    o_ref[...] = jnp.where(inwin_ref[...] > 0, bias, NEG_INF)


def _attn_bias(rel_bias):
    nb, heads = rel_bias.shape
    blk = ATTN_BLOCK
    q_idx = jnp.arange(blk)[:, None]
    k_idx = jnp.arange(3 * blk)[None, :]
    rel = k_idx - blk - q_idx
    bucket = _t5_bucket(rel).reshape(1, 3 * blk * blk).astype(jnp.int32)
    inwin = (jnp.abs(rel) <= blk).astype(jnp.int32).reshape(1, 3 * blk * blk)
    tl = 4096
    out = pl.pallas_call(
        _bias_kernel,
        grid=(3 * blk * blk // tl,),
        in_specs=[pl.BlockSpec((1, tl), lambda i: (0, i)),
                  pl.BlockSpec((1, tl), lambda i: (0, i)),
                  pl.BlockSpec((heads, nb), lambda i: (0, 0))],
        out_specs=pl.BlockSpec((heads, tl), lambda i: (0, i)),
        out_shape=jax.ShapeDtypeStruct((heads, 3 * blk * blk), F32),
        compiler_params=_cparams(1, VMEM_STREAM_MIB),
        name="attn_bias",
    )(bucket, inwin, rel_bias.T)
    return out.reshape(heads, blk, 3 * blk)


def _attn_kernel(sink_ref, q_ref, kp_ref, ko_ref, kn_ref, vp_ref, vo_ref, vn_ref, bias_ref,
                 o_ref, *, nblk, kvh, grp):
    blk, hd = ATTN_BLOCK, HEAD_DIM
    n = pl.program_id(0) % nblk
    col = lax.broadcasted_iota(jnp.int32, (1, 3 * blk), 1)
    key_pos = (n - 1) * blk + col
    edge = jnp.where((key_pos >= 0) & (key_pos < nblk * blk), 0.0, NEG_INF)
    scale = hd ** -0.5
    heads = [[kh * grp + g for g in range(grp)] for kh in range(kvh)]
    scores = []
    for kh in range(kvh):
        ks = slice(kh * hd, (kh + 1) * hd)
        k = jnp.concatenate([kp_ref[:, ks], ko_ref[:, ks], kn_ref[:, ks]], axis=0)
        q = jnp.concatenate([q_ref[:, h * hd:(h + 1) * hd] for h in heads[kh]], axis=0)
        s = lax.dot_general(q, k, (((1,), (1,)), ((), ())), preferred_element_type=F32) * scale
        scores.append(s + bias_ref[kh * grp:(kh + 1) * grp].reshape(grp * blk, 3 * blk) + edge)
    probs, denoms = [], []
    for kh in range(kvh):
        s = scores[kh]
        sink = jnp.concatenate([jnp.full((blk, 1), sink_ref[h], F32) for h in heads[kh]], axis=0)
        mx = jnp.maximum(jnp.max(s, axis=-1, keepdims=True), sink)
        p = jnp.exp(s - mx)
        denoms.append(jnp.sum(p, axis=-1, keepdims=True) + jnp.exp(sink - mx))
        probs.append(p.astype(BF16))
    for kh in range(kvh):
        ks = slice(kh * hd, (kh + 1) * hd)
        v = jnp.concatenate([vp_ref[:, ks], vo_ref[:, ks], vn_ref[:, ks]], axis=0)
        o = jnp.dot(probs[kh], v, preferred_element_type=F32) / denoms[kh]
        for g, h in enumerate(heads[kh]):
            o_ref[:, h * hd:(h + 1) * hd] = o[g * blk:(g + 1) * blk].astype(o_ref.dtype)


def _attention(qkv, bias, sink, seq, heads, kvh):
    m = qkv.shape[0]
    blk, hd = ATTN_BLOCK, HEAD_DIM
    nblk = seq // blk
    grp = heads // kvh
    qw, kw = heads * hd, kvh * hd
    kcol, vcol = qw // kw, qw // kw + 1

    def prev(i):
        return jnp.where(i % nblk == 0, i, i - 1)

    def nxt(i):
        return jnp.where(i % nblk == nblk - 1, i, i + 1)

    kv = lambda rowf, colb: pl.BlockSpec((blk, kw), lambda i: (rowf(i), colb))
    same = lambda i: i
    return pl.pallas_call(
        functools.partial(_attn_kernel, nblk=nblk, kvh=kvh, grp=grp),
        grid=(m // blk,),
        in_specs=[pl.BlockSpec(memory_space=pltpu.SMEM),
                  pl.BlockSpec((blk, qw), lambda i: (i, 0)),
                  kv(prev, kcol), kv(same, kcol), kv(nxt, kcol),
                  kv(prev, vcol), kv(same, vcol), kv(nxt, vcol),
                  pl.BlockSpec((heads, blk, 3 * blk), lambda i: (0, 0, 0))],
        out_specs=pl.BlockSpec((blk, qw), lambda i: (i, 0)),
        out_shape=jax.ShapeDtypeStruct((m, qw), BF16),
        compiler_params=_cparams(1, VMEM_PANEL_MIB),
        name="attention",
    )(sink, qkv, qkv, qkv, qkv, qkv, qkv, qkv, bias)


def _cexp(zr, zi):
    mag = jnp.exp(zr)
    return mag * jnp.cos(zi), mag * jnp.sin(zi)


def _cpow_int(br, bi, e, nbits):
    res_r = res_i = None
    for bit in range(nbits):
        on = ((e >> bit) & 1) == 1
        fr = jnp.where(on, br, 1.0)
        fi = jnp.where(on, bi, 0.0)
        if res_r is None:
            res_r, res_i = fr, fi
        else:
            res_r, res_i = res_r * fr - res_i * fi, res_r * fi + res_i * fr
        if bit + 1 < nbits:
            br, bi = br * br - bi * bi, 2.0 * br * bi
    return res_r, res_i


def _slot_sources(rot, t):
    half = t // 2
    return [c * half + (s - rot) % half for c in range(2) for s in range(half)]


def _rotate_slots(x, rot):
    if rot == 0:
        return x
    h = x.shape[1] // 2
    sh = SSM_GROUP_CH * rot
    return jnp.concatenate([pltpu.roll(x[:, :h], sh, 1), pltpu.roll(x[:, h:], sh, 1)], axis=1)


def _permute_row_blocks(x, rot, t):
    hch = SSM_GROUP_CH
    return jnp.concatenate([x[hch * j:hch * (j + 1)] for j in _slot_sources(rot, t)], axis=0)


def _ssm_prep_group(g, rot, lre_r, lim_r, ldt_r, btr, bti, ctr, cti, dcol):
    t, hch, p = SSM_CHUNK, SSM_GROUP_CH, SSM_STATE
    th, p2 = t * hch, 2 * p
    hi_prec = lax.Precision.HIGHEST
    ar, ai = lre_r[g], lim_r[g]
    dt = jnp.exp(ldt_r[g])
    zr, zi = ar * dt, ai * dt
    lbr, lbi = _cexp(zr, zi)
    nr = lbr - 1.0
    den = ar * ar + ai * ai
    cr = (nr * ar + lbi * ai) / den
    ci = (lbi * ar - nr * ai) / den
    b_r, b_i = btr[g], bti[g]
    bbr = cr * b_r - ci * b_i
    bbi = cr * b_i + ci * b_r
    row = lax.broadcasted_iota(jnp.int32, (th, p2), 0)
    lane = lax.broadcasted_iota(jnp.int32, (th, p2), 1)
    j = row // hch
    nbits = (t - 1).bit_length()
    pr, pi = _cpow_int(lbr, lbi, jnp.where(lane < p, t - 1 - j, j), nbits)
    bt_r = jnp.concatenate([bbr] * t, axis=0)
    bt_i = jnp.concatenate([bbi] * t, axis=0)
    w_mat = jnp.concatenate([pr * bt_r - pi * bt_i, pr * bt_i + pi * bt_r], axis=1)
    l_r, l_i = lbr, lbi
    for _ in range(nbits):
        l_r, l_i = l_r * l_r - l_i * l_i, 2.0 * l_r * l_i
    lam = jnp.concatenate([l_r, l_i], axis=1)
    rowc = lax.broadcasted_iota(jnp.int32, (p2, th), 0)
    lanec = lax.broadcasted_iota(jnp.int32, (p2, th), 1)
    nn = lanec // hch
    eye = (lax.broadcasted_iota(jnp.int32, (p2, p2), 0)
           == lax.broadcasted_iota(jnp.int32, (p2, p2), 1)).astype(F32)
    to_col = lambda r: lax.dot_general(eye, r, (((1,), (1,)), ((), ())),
                                       preferred_element_type=F32, precision=hi_prec)
    lbrc, lbic = to_col(lbr), to_col(lbi)
    qr, qi = _cpow_int(lbrc, lbic, jnp.where(rowc < p, nn, t - 1 - nn), nbits)
    tile = (lax.broadcasted_iota(jnp.int32, (hch, th), 1) % hch
            == lax.broadcasted_iota(jnp.int32, (hch, th), 0)).astype(F32)
    c_r = jnp.dot(ctr[g], tile, preferred_element_type=F32, precision=hi_prec)
    c_i = jnp.dot(cti[g], tile, preferred_element_type=F32, precision=hi_prec)
    e_r = c_r * qr - c_i * qi
    e_i = c_r * qi + c_i * qr
    v_mat = jnp.concatenate([e_r * lbrc - e_i * lbic, -(e_r * lbic + e_i * lbrc)], axis=0)
    fwd_lane = lax.broadcasted_iota(jnp.int32, (hch, p2), 1) < p
    rhs = jnp.concatenate([e_r, e_i], axis=0)
    lhs0 = jnp.concatenate([jnp.where(fwd_lane, bbr, 0.0), -jnp.where(fwd_lane, bbi, 0.0)], axis=1)
    lhs1 = jnp.concatenate([jnp.where(fwd_lane, 0.0, bbr), -jnp.where(fwd_lane, 0.0, bbi)], axis=1)
    k0 = jnp.dot(lhs0, rhs, preferred_element_type=F32, precision=hi_prec)
    k1 = jnp.dot(lhs1, rhs, preferred_element_type=F32, precision=hi_prec)
    lane_m = lax.broadcasted_iota(jnp.int32, (hch, th), 1)
    row_m = lax.broadcasted_iota(jnp.int32, (hch, th), 0)
    d_g = dcol[g]
    blocks = []
    for jj in range(t):
        a = pltpu.roll(k0, hch * jj, 1) if jj else k0
        a = jnp.where(lane_m >= hch * jj, a, 0.0)
        sh = (hch * (jj + 1)) % th
        b = pltpu.roll(k1, sh, 1) if sh else k1
        b = jnp.where(lane_m < hch * (jj + 1), b, 0.0)
        dd = jnp.where(lane_m == hch * jj + row_m, d_g, 0.0)
        blocks.append(a + b + dd)
    m_mat = jnp.concatenate(blocks, axis=0)
    m_mat = _permute_row_blocks(_rotate_slots(m_mat, rot), rot, t)
    w_mat = _permute_row_blocks(w_mat, rot, t)
    v_mat = _rotate_slots(v_mat, rot)
    return m_mat, w_mat, v_mat, lam


def _ssm_prep_kernel(lre_r, lim_r, ldt_r, btr, bti, ctr, cti, dcol,
                     m_ref, w_ref, v_ref, lam_ref, *, groups, nb):
    ins = (lre_r, lim_r, ldt_r, btr, bti, ctr, cti, dcol)
    first = lax.broadcasted_iota(jnp.int32, (2 * nb, 4 * SSM_STATE), 0) < nb
    lams = []
    for g in range(groups):
        m_mat, w_mat, v_mat, lam = _ssm_prep_group(g, g % SSM_TILE_GROUPS, *ins)
        m_ref[g] = m_mat.astype(m_ref.dtype)
        w_ref[g] = w_mat.astype(w_ref.dtype)
        v_ref[g] = v_mat.astype(v_ref.dtype)
        lams.append(jnp.broadcast_to(lam, (2 * nb, 4 * SSM_STATE)))
        if g % 2 == 1:
            lam_ref[g // 2] = jnp.where(first, lams[g - 1], lams[g])


def _ssm_prep(lre, lim, ldt, bre, bim, cre, cim, d, nb):
    _, g, p = lre.shape
    hch, t = SSM_GROUP_CH, SSM_CHUNK
    th = t * hch
    cat = lambda a: jnp.concatenate([a[0], a[1]], axis=-1)
    lre2, lim2 = cat(lre), cat(lim)
    ldt2 = jnp.repeat(ldt.T, p, axis=1)
    bt = lambda a: jnp.transpose(a, (1, 3, 0, 2)).reshape(g, hch, 2 * p)
    ct = lambda a: jnp.transpose(a, (1, 0, 3, 2)).reshape(g, 2 * p, hch)
    gp = SSM_TILE_GROUPS
    assert g % gp == 0
    rowv = pl.BlockSpec((gp, 1, 2 * p), lambda i: (i, 0, 0))
    mat = pl.BlockSpec((gp, th, th), lambda i: (i, 0, 0))
    return pl.pallas_call(
        functools.partial(_ssm_prep_kernel, groups=gp, nb=nb),
        grid=(g // gp,),
        in_specs=[rowv, rowv, rowv,
                  pl.BlockSpec((gp, hch, 2 * p), lambda i: (i, 0, 0)),
                  pl.BlockSpec((gp, hch, 2 * p), lambda i: (i, 0, 0)),
                  pl.BlockSpec((gp, 2 * p, hch), lambda i: (i, 0, 0)),
                  pl.BlockSpec((gp, 2 * p, hch), lambda i: (i, 0, 0)),
                  pl.BlockSpec((gp, hch, 1), lambda i: (i, 0, 0))],
        out_specs=[mat, mat, mat,
                   pl.BlockSpec((gp // 2, 2 * nb, 4 * p), lambda i: (i, 0, 0))],
        out_shape=[jax.ShapeDtypeStruct((g, th, th), BF16)] * 3
        + [jax.ShapeDtypeStruct((g // 2, 2 * nb, 4 * p), F32)],
        compiler_params=_cparams(1, VMEM_STREAM_MIB),
        name="ssm_prep",
    )(lre2.reshape(g, 1, 2 * p), lim2.reshape(g, 1, 2 * p), ldt2.reshape(g, 1, 2 * p),
      bt(bre), bt(bim), ct(cre), ct(cim), d.reshape(g, hch, 1))


def _ssm_kernel(u_ref, m_ref, w_ref, v_ref, lam_ref, yext_ref, y_ref, yin_s, s_s, xf_s, xb_s,
                tok_s, u2_s, *, pb, nchunk, nb):
    slab = 2 * nb
    rows = nchunk * slab
    p2 = 2 * SSM_STATE
    hch, tch = SSM_GROUP_CH, SSM_CHUNK
    half = tch * hch // 2
    lanes = 2 * pb * hch
    slots = lanes // hch
    first = (lax.broadcasted_iota(jnp.int32, (rows, 1), 0) // nb) % 2 == 0
    lane_grp = lax.broadcasted_iota(jnp.int32, (nchunk, lanes), 1) // hch

    def pair_dot(lhs, mats, q):
        ya = jnp.dot(lhs, mats[2 * q], preferred_element_type=F32)
        yb = jnp.dot(lhs, mats[2 * q + 1], preferred_element_type=F32)
        return jnp.where(first, ya, yb)

    for b in range(nb):
        for tq in range(tch):
            tok_s[pl.ds((b * tch + tq) * nchunk, nchunk), :] = (
                u_ref[pl.ds(b * nchunk * tch + tq, nchunk, stride=tch), :])
    for b in range(nb):
        for col in range(2):
            rolled = []
            for j in range(slots):
                piece = tok_s[pl.ds((b * tch + col * slots + j) * nchunk, nchunk), :]
                rolled.append(pltpu.roll(piece, hch * j, 1) if j else piece)
            for gl in range(slots):
                acc = rolled[0]
                for j in range(1, slots):
                    acc = jnp.where(lane_grp == (j + gl) % slots, rolled[j], acc)
                q, g2 = divmod(gl, 2)
                u2_s[q, col, pl.ds(g2 * nb + b, nchunk, stride=slab), :] = acc

    for q in range(pb):
        u = jnp.concatenate([u2_s[q, 0], u2_s[q, 1]], axis=1).astype(BF16)
        y0 = pair_dot(u, m_ref, q)
        yin_s[q, 0] = y0[:, :half]
        yin_s[q, 1] = y0[:, half:]
        s_s[q] = pair_dot(u, w_ref, q)

    fwd = lax.broadcasted_iota(jnp.int32, (slab, p2), 1) < SSM_STATE
    lam_r = [lam_ref[q, :, 0:p2] for q in range(pb)]
    lam_i = [lam_ref[q, :, p2:2 * p2] for q in range(pb)]

    def step(k, carry):
        kf = pl.multiple_of(k * slab, slab)
        kb = pl.multiple_of((nchunk - 1 - k) * slab, slab)
        out = []
        for q in range(pb):
            xr, xi = carry[2 * q], carry[2 * q + 1]
            xf_s[q, pl.ds(kf, slab), 0:p2] = xr
            xf_s[q, pl.ds(kf, slab), p2:2 * p2] = xi
            xb_s[q, pl.ds(kb, slab), 0:p2] = xr
            xb_s[q, pl.ds(kb, slab), p2:2 * p2] = xi
            sr = jnp.where(fwd, s_s[q, pl.ds(kf, slab), 0:p2], s_s[q, pl.ds(kb, slab), 0:p2])
            si = jnp.where(fwd, s_s[q, pl.ds(kf, slab), p2:2 * p2],
                           s_s[q, pl.ds(kb, slab), p2:2 * p2])
            out.append(lam_r[q] * xr - lam_i[q] * xi + sr)
            out.append(lam_r[q] * xi + lam_i[q] * xr + si)
        return tuple(out)

    zero = jnp.zeros((slab, p2), F32)
    lax.fori_loop(0, nchunk, step, (zero,) * (2 * pb))

    fwd_all = lax.broadcasted_iota(jnp.int32, (rows, 2 * p2), 1) % p2 < SSM_STATE
    for q in range(pb):
        x = jnp.where(fwd_all, xf_s[q], xb_s[q]).astype(BF16)
        yv = pair_dot(x, v_ref, q)
        yin_s[q, 0] = yin_s[q, 0] + yv[:, :half]
        yin_s[q, 1] = yin_s[q, 1] + yv[:, half:]

    for q in range(pb):
        for col in range(2):
            for r in range(slab):
                u2_s[q, col, pl.ds(r * nchunk, nchunk), :] = (
                    yin_s[q, col, pl.ds(r, nchunk, stride=slab), :])
    for b in range(nb):
        for tq in range(tch):
            col, i = divmod(tq, slots)
            acc = None
            for gl in range(slots):
                q, g2 = divmod(gl, 2)
                piece = u2_s[q, col, pl.ds((g2 * nb + b) * nchunk, nchunk), :]
                acc = piece if acc is None else jnp.where(lane_grp == (i + gl) % slots, piece, acc)
            if i:
                acc = pltpu.roll(acc, (-hch * i) % lanes, 1)
            tok_s[pl.ds(b * nchunk * tch + tq, nchunk, stride=tch), :] = acc
    y_ref[...] = jax.nn.gelu(tok_s[...] + yext_ref[...]).astype(y_ref.dtype)


def _ssm_core(u, m_mat, w_mat, v_mat, lam, yext, nchunk, nb):
    m, dm = u.shape
    th = SSM_CHUNK * SSM_GROUP_CH
    npair = dm // (2 * SSM_GROUP_CH)
    rows = nchunk * 2 * nb
    pb = 4
    assert npair % pb == 0 and 2 * pb * SSM_GROUP_CH == 128 and th == 256
    p4 = 4 * SSM_STATE
    mat = pl.BlockSpec((2 * pb, th, th), lambda i: (i, 0, 0))
    tok = pl.BlockSpec((m, 128), lambda i: (0, i))
    return pl.pallas_call(
        functools.partial(_ssm_kernel, pb=pb, nchunk=nchunk, nb=nb),
        grid=(npair // pb,),
        in_specs=[tok, mat, mat, mat, pl.BlockSpec((pb, 2 * nb, p4), lambda i: (i, 0, 0)), tok],
        out_specs=tok,
        out_shape=jax.ShapeDtypeStruct((m, dm), BF16),
        scratch_shapes=[pltpu.VMEM((pb, 2, rows, th // 2), F32), pltpu.VMEM((pb, rows, p4), F32),
                        pltpu.VMEM((pb, rows, p4), F32), pltpu.VMEM((pb, rows, p4), F32),
                        pltpu.VMEM((m, 128), F32), pltpu.VMEM((pb, 2, rows, th // 2), F32)],
        compiler_params=_cparams(1, VMEM_MAX_MIB),
        name="ssm_core",
    )(u, m_mat, w_mat, v_mat, lam, yext)


def _s5_mixer(h, w_in, lre, lim, ldt, bre, bim, cre, cim, d, w_glu, batch, seq):
    m, dm = h.shape
    u = _mm(h, w_in, F32, tm=_pick(m, (2048, 1024, 512, 256, 128)), tn=_pick(dm, (256, 128)))
    yb = _s5_direction(u, batch, lre[1], lim[1], ldt[1], bre[1], bim[1], cre[1], cim[1],
                       reverse=True)
    m_mat, w_mat, v_mat, lam = _ssm_prep(lre, lim, ldt, bre.at[1].set(0.0), bim.at[1].set(0.0),
                                         cre, cim, d, batch)
    yg = _ssm_core(u, m_mat, w_mat, v_mat, lam, yb.reshape(m, dm), seq // SSM_CHUNK, batch)
    return _glu(yg, w_glu, tm=_pick(m, (2048, 1024, 512, 256, 128)), tn=_pick(dm, (256, 128)))


def _recurrence_combine(left, right):
    a_l, b_l = left
    a_r, b_r = right
    return a_r * a_l, a_r * b_l + b_r


def _state_in_kernel(u_ref, w_ref, re_ref, im_ref):
    n = re_ref.shape[1]
    bu = jnp.dot(u_ref[...].astype(BF16), w_ref[0], preferred_element_type=F32)
    re_ref[...] = bu[:, :n]
    im_ref[...] = bu[:, n:]


def _state_out_kernel(re_ref, im_ref, w_ref, y_ref):
    s = jnp.concatenate([re_ref[...].astype(BF16), im_ref[...].astype(BF16)], axis=1)
    y_ref[...] = jnp.dot(s, w_ref[0], preferred_element_type=F32)


def _block_diag(x):
    nb, k, r, c = x.shape
    eye = jnp.eye(k, dtype=x.dtype)
    return jnp.einsum('bgrc,gk->bgrkc', x, eye).reshape(nb, k * r, k * c)


def _s5_direction(u, batch, lam_re, lam_im, log_dt, b_re, b_im, c_re, c_im, reverse):
    m, dm = u.shape
    g, p = lam_re.shape
    hch, tg = dm // g, SSM_TILE_GROUPS
    lam = lax.complex(lam_re, lam_im)
    dt = jnp.exp(log_dt)[:, None]
    lam_bar = jnp.exp(lam * dt)
    b_bar = ((lam_bar - 1.0) / lam)[..., None] * lax.complex(b_re, b_im)
    bt = jnp.transpose(b_bar, (0, 2, 1)).reshape(g // tg, tg, hch, p)
    w_in = jnp.concatenate([_block_diag(jnp.real(bt)), _block_diag(jnp.imag(bt))],
                           axis=2).astype(BF16)
    ct = jnp.transpose(lax.complex(c_re, c_im), (0, 2, 1)).reshape(g // tg, tg, p, hch)
    w_out = jnp.concatenate([_block_diag(jnp.real(ct)), -_block_diag(jnp.imag(ct))],
                            axis=1).astype(BF16)
    nblk, lanes, sw = g // tg, tg * hch, tg * p
    tm = _pick(m, (2048, 1024, 512, 256, 128))
    state = jax.ShapeDtypeStruct((m, g * p), F32)
    st_spec = pl.BlockSpec((tm, sw), lambda i, j: (i, j))
    bu_re, bu_im = pl.pallas_call(
        _state_in_kernel, grid=(m // tm, nblk),
        in_specs=[pl.BlockSpec((tm, lanes), lambda i, j: (i, j)),
                  pl.BlockSpec((1, lanes, 2 * sw), lambda i, j: (j, 0, 0))],
        out_specs=[st_spec, st_spec], out_shape=[state, state],
        compiler_params=_cparams(2, VMEM_STREAM_MIB), name="ssm_state_in",
    )(u, w_in)
    scan_shape = (batch, m // batch, g * p // 128, 128)
    a = jnp.broadcast_to(lam_bar.reshape((1, 1) + scan_shape[2:]), (1,) + scan_shape[1:])
    bu = lax.complex(bu_re, bu_im).reshape(scan_shape)
    _, states = lax.associative_scan(_recurrence_combine, (a, bu), axis=1, reverse=reverse)
    return pl.pallas_call(
        _state_out_kernel, grid=(m // tm, nblk),
        in_specs=[st_spec, st_spec,
                  pl.BlockSpec((1, 2 * sw, lanes), lambda i, j: (j, 0, 0))],
        out_specs=pl.BlockSpec((tm, lanes), lambda i, j: (i, j)),
        out_shape=jax.ShapeDtypeStruct((m, dm), F32),
        compiler_params=_cparams(2, VMEM_STREAM_MIB), name="ssm_state_out",
    )(jnp.real(states).reshape(m, g * p), jnp.imag(states).reshape(m, g * p), w_out)


def kernel(x, rel_bias, pre_mix_norm, post_mix_norm, pre_ffn_norm, post_ffn_norm, attn_wqkv, attn_sink, attn_wo, ssm_w_in, ssm_lambda_re, ssm_lambda_im, ssm_log_dt, ssm_b_re, ssm_b_im, ssm_c_re, ssm_c_im, ssm_d, ssm_w_glu, ffn_w_gate, ffn_w_up, ffn_conv_w, ffn_conv_b, ffn_w_down):
    batch, seq, dm = x.shape
    depth = pre_mix_norm.shape[0]
    m = batch * seq
    heads = dm // HEAD_DIM
    kvh = (attn_wqkv.shape[2] // HEAD_DIM - heads) // 2
    dff = ffn_w_gate.shape[2]
    assert seq % ATTN_BLOCK == 0 and seq % SSM_CHUNK == 0 and (2 * batch) % 8 == 0
    tm_big = _pick(m, (2048, 1024, 512, 256, 128))
    tm_mid = _pick(m, (1024, 512, 256, 128))

    xf = x.reshape(m, dm)
    h = _norm_cast(xf, pre_mix_norm[0])
    bias = _attn_bias(rel_bias)
    for i in range(depth):
        j = i // 2
        if i % 2 == 0:
            qkv = _mm(h, attn_wqkv[j], BF16, tm=tm_big, tn=_pick(attn_wqkv.shape[2], (512, 256, 128)))
            o = _attention(qkv, bias, attn_sink[j], seq, heads, kvh)
            mix = _mm(o, attn_wo[j], F32, tm=tm_big, tn=_pick(dm, (512, 256, 128)))
        else:
            mix = _s5_mixer(h, ssm_w_in[j], ssm_lambda_re[j], ssm_lambda_im[j], ssm_log_dt[j],
                            ssm_b_re[j], ssm_b_im[j], ssm_c_re[j], ssm_c_im[j], ssm_d[j],
                            ssm_w_glu[j], batch, seq)
        xf, h = _resid_norm(xf, mix, post_mix_norm[i], pre_ffn_norm[i])
        hid = _ffn_in(h, ffn_w_gate[i], ffn_w_up[i], ffn_conv_w[i], ffn_conv_b[i], seq,
                      tn=_pick(dff, (256, 128)))
        kc = dff // 2 if (dff // 2) % 128 == 0 else dff
        f = _mm(hid, ffn_w_down[i], F32, tm=tm_mid, tn=_pick(dm, (256, 128)), kc=kc,
                vmem_mib=VMEM_MAX_MIB)
        g_next = pre_mix_norm[i + 1] if i + 1 < depth else None
        xf, h = _resid_norm(xf, f, post_ffn_norm[i], g_next)
    return xf.reshape(batch, seq, dm)
```

```python
import functools
import math

import jax
import jax.numpy as jnp
from jax import lax
from jax.experimental import pallas as pl
from jax.experimental.pallas import tpu as pltpu

F32 = jnp.float32
BF16 = jnp.bfloat16

HEAD_DIM = 128
ATTN_BLOCK = 128
NUM_BUCKETS = 32
SSM_GROUP_CH = 16
SSM_STATE = 64
SSM_CHUNK = 16
SSM_TILE_GROUPS = 8
RMS_EPS = 1e-6
NEG_INF = -1e30

MIB = 1024 * 1024
VMEM_STREAM_MIB = 32
VMEM_PANEL_MIB = 56
VMEM_MAX_MIB = 60


def _cparams(n_grid_dims, vmem_mib):
    return pltpu.CompilerParams(
        dimension_semantics=("arbitrary",) * n_grid_dims,
        vmem_limit_bytes=vmem_mib * MIB,
    )


def _pick(n, prefs):
    for p in prefs:
        if p <= n and n % p == 0:
            return p
    return n


def _rms(x, g):
    return x * lax.rsqrt(jnp.mean(x * x, axis=-1, keepdims=True) + RMS_EPS) * g


def _norm_kernel(x_ref, g_ref, h_ref):
    h_ref[...] = _rms(x_ref[...], g_ref[...]).astype(h_ref.dtype)


def _norm_cast(x, g):
    m, d = x.shape
    tm = _pick(m, (256, 128, 64, 32, 16, 8))
    return pl.pallas_call(
        _norm_kernel,
        grid=(m // tm,),
        in_specs=[pl.BlockSpec((tm, d), lambda i: (i, 0)),
                  pl.BlockSpec((1, d), lambda i: (0, 0))],
        out_specs=pl.BlockSpec((tm, d), lambda i: (i, 0)),
        out_shape=jax.ShapeDtypeStruct((m, d), BF16),
        compiler_params=_cparams(1, VMEM_STREAM_MIB),
        name="norm_cast",
    )(x, g.reshape(1, d))


def _resid_norm_kernel(x_ref, m_ref, g1_ref, g2_ref, xo_ref, ho_ref):
    xn = x_ref[...] + _rms(m_ref[...], g1_ref[...])
    xo_ref[...] = xn
    ho_ref[...] = _rms(xn, g2_ref[...]).astype(ho_ref.dtype)


def _resid_kernel(x_ref, m_ref, g1_ref, xo_ref):
    xo_ref[...] = x_ref[...] + _rms(m_ref[...], g1_ref[...])


def _resid_norm(x, mix, g_post, g_next):
    m, d = x.shape
    tm = _pick(m, (128, 64, 32, 16, 8))
    row = pl.BlockSpec((tm, d), lambda i: (i, 0))
    vec = pl.BlockSpec((1, d), lambda i: (0, 0))
    if g_next is None:
        return pl.pallas_call(
            _resid_kernel, grid=(m // tm,),
            in_specs=[row, row, vec], out_specs=row,
            out_shape=jax.ShapeDtypeStruct((m, d), F32),
            compiler_params=_cparams(1, VMEM_STREAM_MIB), name="resid",
        )(x, mix, g_post.reshape(1, d)), None
    return pl.pallas_call(
        _resid_norm_kernel, grid=(m // tm,),
        in_specs=[row, row, vec, vec], out_specs=[row, row],
        out_shape=[jax.ShapeDtypeStruct((m, d), F32), jax.ShapeDtypeStruct((m, d), BF16)],
        compiler_params=_cparams(1, VMEM_STREAM_MIB), name="resid_norm",
    )(x, mix, g_post.reshape(1, d), g_next.reshape(1, d))


def _fetch_row_panel(a_hbm, a_ref, sem):
    @pl.when(pl.program_id(1) == 0)
    def _():
        tm = a_ref.shape[0]
        cp = pltpu.make_async_copy(
            a_hbm.at[pl.ds(pl.multiple_of(pl.program_id(0) * tm, tm), tm), :], a_ref, sem)
        cp.start()
        cp.wait()


def _panel_scratch(tm, k):
    return [pltpu.VMEM((tm, k), BF16), pltpu.SemaphoreType.DMA(())]


def _mm_kernel(a_hbm, w_ref, o_ref, a_ref, sem, *, kc):
    _fetch_row_panel(a_hbm, a_ref, sem)
    k = a_ref.shape[1]
    acc = None
    for k0 in range(0, k, kc):
        part = jnp.dot(a_ref[:, k0:k0 + kc], w_ref[k0:k0 + kc, :].astype(BF16),
                       preferred_element_type=F32)
        acc = part if acc is None else acc + part
    o_ref[...] = acc.astype(o_ref.dtype)


def _mm(a, w, out_dtype, tm, tn, kc=None, vmem_mib=VMEM_PANEL_MIB):
    m, k = a.shape
    n = w.shape[1]
    kc = k if kc is None else kc
    return pl.pallas_call(
        functools.partial(_mm_kernel, kc=kc),
        grid=(m // tm, n // tn),
        in_specs=[pl.BlockSpec(memory_space=pl.ANY),
                  pl.BlockSpec((k, tn), lambda i, j: (0, j))],
        out_specs=pl.BlockSpec((tm, tn), lambda i, j: (i, j)),
        out_shape=jax.ShapeDtypeStruct((m, n), out_dtype),
        scratch_shapes=_panel_scratch(tm, k),
        compiler_params=_cparams(2, vmem_mib),
        name="mm",
    )(a, w)


def _glu_kernel(a_hbm, wa_ref, wb_ref, o_ref, a_ref, sem):
    _fetch_row_panel(a_hbm, a_ref, sem)
    a = a_ref[...]
    ya = jnp.dot(a, wa_ref[...].astype(BF16), preferred_element_type=F32)
    yb = jnp.dot(a, wb_ref[...].astype(BF16), preferred_element_type=F32)
    o_ref[...] = (ya * jax.nn.sigmoid(yb)).astype(o_ref.dtype)


def _glu(a, w, tm, tn):
    m, k = a.shape
    n = w.shape[1] // 2
    nj = n // tn
    return pl.pallas_call(
        _glu_kernel,
        grid=(m // tm, nj),
        in_specs=[pl.BlockSpec(memory_space=pl.ANY),
                  pl.BlockSpec((k, tn), lambda i, j: (0, j)),
                  pl.BlockSpec((k, tn), lambda i, j: (0, j + nj))],
        out_specs=pl.BlockSpec((tm, tn), lambda i, j: (i, j)),
        out_shape=jax.ShapeDtypeStruct((m, n), F32),
        scratch_shapes=_panel_scratch(tm, k),
        compiler_params=_cparams(2, VMEM_PANEL_MIB),
        name="glu",
    )(a, w, w)


def _ffn_in_kernel(a_hbm, wg_ref, wu_ref, cw_ref, cb_ref, o_ref, a_ref, sem):
    _fetch_row_panel(a_hbm, a_ref, sem)
    a = a_ref[...]
    g = jnp.dot(a, wg_ref[...].astype(BF16), preferred_element_type=F32)
    u = jnp.dot(a, wu_ref[...].astype(BF16), preferred_element_type=F32)
    rows = g.shape[0]
    row = lax.broadcasted_iota(jnp.int32, (rows, 1), 0)
    g_prev = jnp.where(row == 0, 0.0, pltpu.roll(g, 1, 0))
    g_next = jnp.where(row == rows - 1, 0.0, pltpu.roll(g, rows - 1, 0))
    gc = cw_ref[0:1, :] * g_prev + cw_ref[1:2, :] * g + cw_ref[2:3, :] * g_next + cb_ref[...]
    o_ref[...] = (gc * jax.nn.sigmoid(gc) * u).astype(o_ref.dtype)


def _ffn_in(h, w_gate, w_up, conv_w, conv_b, seq, tn):
    m, k = h.shape
    f = w_gate.shape[1]
    return pl.pallas_call(
        _ffn_in_kernel,
        grid=(m // seq, f // tn),
        in_specs=[pl.BlockSpec(memory_space=pl.ANY),
                  pl.BlockSpec((k, tn), lambda i, j: (0, j)),
                  pl.BlockSpec((k, tn), lambda i, j: (0, j)),
                  pl.BlockSpec((3, tn), lambda i, j: (0, j)),
                  pl.BlockSpec((1, tn), lambda i, j: (0, j))],
        out_specs=pl.BlockSpec((seq, tn), lambda i, j: (i, j)),
        out_shape=jax.ShapeDtypeStruct((m, f), BF16),
        scratch_shapes=_panel_scratch(seq, k),
        compiler_params=_cparams(2, VMEM_PANEL_MIB),
        name="ffn_in",
    )(h, w_gate, w_up, conv_w, conv_b.reshape(1, f))


def _t5_bucket(rel):
    half = NUM_BUCKETS // 2
    max_exact = half // 2
    base = jnp.where(rel > 0, half, 0)
    n = jnp.abs(rel)
    nf = jnp.maximum(n, 1).astype(F32)
    large = max_exact + (jnp.log(nf / max_exact) / math.log(ATTN_BLOCK / max_exact)
                         * (half - max_exact)).astype(jnp.int32)
    large = jnp.minimum(large, half - 1)
    return base + jnp.where(n < max_exact, n, large)


def _bias_kernel(bucket_ref, inwin_ref, rbt_ref, o_ref):
    nb = rbt_ref.shape[1]
    lanes = bucket_ref.shape[1]
    onehot = (lax.broadcasted_iota(jnp.int32, (nb, lanes), 0) == bucket_ref[...]).astype(F32)
    bias = jnp.dot(rbt_ref[...], onehot, preferred_element_type=F32,
                   precision=lax.Precision.HIGHEST)
    o_ref[...] = jnp.where(inwin_ref[...] > 0, bias, NEG_INF)


def _attn_bias(rel_bias):
    nb, heads = rel_bias.shape
    blk = ATTN_BLOCK
    q_idx = jnp.arange(blk)[:, None]
    k_idx = jnp.arange(3 * blk)[None, :]
    rel = k_idx - blk - q_idx
    bucket = _t5_bucket(rel).reshape(1, 3 * blk * blk).astype(jnp.int32)
    inwin = (jnp.abs(rel) <= blk).astype(jnp.int32).reshape(1, 3 * blk * blk)
    tl = 4096
    out = pl.pallas_call(
        _bias_kernel,
        grid=(3 * blk * blk // tl,),
        in_specs=[pl.BlockSpec((1, tl), lambda i: (0, i)),
                  pl.BlockSpec((1, tl), lambda i: (0, i)),
                  pl.BlockSpec((heads, nb), lambda i: (0, 0))],
        out_specs=pl.BlockSpec((heads, tl), lambda i: (0, i)),
        out_shape=jax.ShapeDtypeStruct((heads, 3 * blk * blk), F32),
        compiler_params=_cparams(1, VMEM_STREAM_MIB),
        name="attn_bias",
    )(bucket, inwin, rel_bias.T)
    return out.reshape(heads, blk, 3 * blk)


def _attn_kernel(sink_ref, q_ref, kp_ref, ko_ref, kn_ref, vp_ref, vo_ref, vn_ref, bias_ref,
                 o_ref, *, nblk, kvh, grp):
    blk, hd = ATTN_BLOCK, HEAD_DIM
    n = pl.program_id(0) % nblk
    col = lax.broadcasted_iota(jnp.int32, (1, 3 * blk), 1)
    key_pos = (n - 1) * blk + col
    edge = jnp.where((key_pos >= 0) & (key_pos < nblk * blk), 0.0, NEG_INF)
    scale = hd ** -0.5
    heads = [[kh * grp + g for g in range(grp)] for kh in range(kvh)]
    scores = []
    for kh in range(kvh):
        ks = slice(kh * hd, (kh + 1) * hd)
        k = jnp.concatenate([kp_ref[:, ks], ko_ref[:, ks], kn_ref[:, ks]], axis=0)
        q = jnp.concatenate([q_ref[:, h * hd:(h + 1) * hd] for h in heads[kh]], axis=0)
        s = lax.dot_general(q, k, (((1,), (1,)), ((), ())), preferred_element_type=F32) * scale
        scores.append(s + bias_ref[kh * grp:(kh + 1) * grp].reshape(grp * blk, 3 * blk) + edge)
    probs, denoms = [], []
    for kh in range(kvh):
        s = scores[kh]
        sink = jnp.concatenate([jnp.full((blk, 1), sink_ref[h], F32) for h in heads[kh]], axis=0)
        mx = jnp.maximum(jnp.max(s, axis=-1, keepdims=True), sink)
        p = jnp.exp(s - mx)
        denoms.append(jnp.sum(p, axis=-1, keepdims=True) + jnp.exp(sink - mx))
        probs.append(p.astype(BF16))
    for kh in range(kvh):
        ks = slice(kh * hd, (kh + 1) * hd)
        v = jnp.concatenate([vp_ref[:, ks], vo_ref[:, ks], vn_ref[:, ks]], axis=0)
        o = jnp.dot(probs[kh], v, preferred_element_type=F32) / denoms[kh]
        for g, h in enumerate(heads[kh]):
            o_ref[:, h * hd:(h + 1) * hd] = o[g * blk:(g + 1) * blk].astype(o_ref.dtype)


def _attention(qkv, bias, sink, seq, heads, kvh):
    m = qkv.shape[0]
    blk, hd = ATTN_BLOCK, HEAD_DIM
    nblk = seq // blk
    grp = heads // kvh
    qw, kw = heads * hd, kvh * hd
    kcol, vcol = qw // kw, qw // kw + 1

    def prev(i):
        return jnp.where(i % nblk == 0, i, i - 1)

    def nxt(i):
        return jnp.where(i % nblk == nblk - 1, i, i + 1)

    kv = lambda rowf, colb: pl.BlockSpec((blk, kw), lambda i: (rowf(i), colb))
    same = lambda i: i
    return pl.pallas_call(
        functools.partial(_attn_kernel, nblk=nblk, kvh=kvh, grp=grp),
        grid=(m // blk,),
        in_specs=[pl.BlockSpec(memory_space=pltpu.SMEM),
                  pl.BlockSpec((blk, qw), lambda i: (i, 0)),
                  kv(prev, kcol), kv(same, kcol), kv(nxt, kcol),
                  kv(prev, vcol), kv(same, vcol), kv(nxt, vcol),
                  pl.BlockSpec((heads, blk, 3 * blk), lambda i: (0, 0, 0))],
        out_specs=pl.BlockSpec((blk, qw), lambda i: (i, 0)),
        out_shape=jax.ShapeDtypeStruct((m, qw), BF16),
        compiler_params=_cparams(1, VMEM_PANEL_MIB),
        name="attention",
    )(sink, qkv, qkv, qkv, qkv, qkv, qkv, qkv, bias)


def _cexp(zr, zi):
    mag = jnp.exp(zr)
    return mag * jnp.cos(zi), mag * jnp.sin(zi)


def _cpow_int(br, bi, e, nbits):
    res_r = res_i = None
    for bit in range(nbits):
        on = ((e >> bit) & 1) == 1
        fr = jnp.where(on, br, 1.0)
        fi = jnp.where(on, bi, 0.0)
        if res_r is None:
            res_r, res_i = fr, fi
        else:
            res_r, res_i = res_r * fr - res_i * fi, res_r * fi + res_i * fr
        if bit + 1 < nbits:
            br, bi = br * br - bi * bi, 2.0 * br * bi
    return res_r, res_i


def _slot_sources(rot, t):
    half = t // 2
    return [c * half + (s - rot) % half for c in range(2) for s in range(half)]


def _rotate_slots(x, rot):
    if rot == 0:
        return x
    h = x.shape[1] // 2
    sh = SSM_GROUP_CH * rot
    return jnp.concatenate([pltpu.roll(x[:, :h], sh, 1), pltpu.roll(x[:, h:], sh, 1)], axis=1)


def _permute_row_blocks(x, rot, t):
    hch = SSM_GROUP_CH
    return jnp.concatenate([x[hch * j:hch * (j + 1)] for j in _slot_sources(rot, t)], axis=0)


def _ssm_prep_group(g, rot, lre_r, lim_r, ldt_r, btr, bti, ctr, cti, dcol):
    t, hch, p = SSM_CHUNK, SSM_GROUP_CH, SSM_STATE
    th, p2 = t * hch, 2 * p
    hi_prec = lax.Precision.HIGHEST
    ar, ai = lre_r[g], lim_r[g]
    dt = jnp.exp(ldt_r[g])
    zr, zi = ar * dt, ai * dt
    lbr, lbi = _cexp(zr, zi)
    nr = lbr - 1.0
    den = ar * ar + ai * ai
    cr = (nr * ar + lbi * ai) / den
    ci = (lbi * ar - nr * ai) / den
    b_r, b_i = btr[g], bti[g]
    bbr = cr * b_r - ci * b_i
    bbi = cr * b_i + ci * b_r
    row = lax.broadcasted_iota(jnp.int32, (th, p2), 0)
    lane = lax.broadcasted_iota(jnp.int32, (th, p2), 1)
    j = row // hch
    nbits = (t - 1).bit_length()
    pr, pi = _cpow_int(lbr, lbi, jnp.where(lane < p, t - 1 - j, j), nbits)
    bt_r = jnp.concatenate([bbr] * t, axis=0)
    bt_i = jnp.concatenate([bbi] * t, axis=0)
    w_mat = jnp.concatenate([pr * bt_r - pi * bt_i, pr * bt_i + pi * bt_r], axis=1)
    l_r, l_i = lbr, lbi
    for _ in range(nbits):
        l_r, l_i = l_r * l_r - l_i * l_i, 2.0 * l_r * l_i
    lam = jnp.concatenate([l_r, l_i], axis=1)
    rowc = lax.broadcasted_iota(jnp.int32, (p2, th), 0)
    lanec = lax.broadcasted_iota(jnp.int32, (p2, th), 1)
    nn = lanec // hch
    eye = (lax.broadcasted_iota(jnp.int32, (p2, p2), 0)
           == lax.broadcasted_iota(jnp.int32, (p2, p2), 1)).astype(F32)
    to_col = lambda r: lax.dot_general(eye, r, (((1,), (1,)), ((), ())),
                                       preferred_element_type=F32, precision=hi_prec)
    lbrc, lbic = to_col(lbr), to_col(lbi)
    qr, qi = _cpow_int(lbrc, lbic, jnp.where(rowc < p, nn, t - 1 - nn), nbits)
    tile = (lax.broadcasted_iota(jnp.int32, (hch, th), 1) % hch
            == lax.broadcasted_iota(jnp.int32, (hch, th), 0)).astype(F32)
    c_r = jnp.dot(ctr[g], tile, preferred_element_type=F32, precision=hi_prec)
    c_i = jnp.dot(cti[g], tile, preferred_element_type=F32, precision=hi_prec)
    e_r = c_r * qr - c_i * qi
    e_i = c_r * qi + c_i * qr
    v_mat = jnp.concatenate([e_r * lbrc - e_i * lbic, -(e_r * lbic + e_i * lbrc)], axis=0)
    fwd_lane = lax.broadcasted_iota(jnp.int32, (hch, p2), 1) < p
    rhs = jnp.concatenate([e_r, e_i], axis=0)
    lhs0 = jnp.concatenate([jnp.where(fwd_lane, bbr, 0.0), -jnp.where(fwd_lane, bbi, 0.0)], axis=1)
    lhs1 = jnp.concatenate([jnp.where(fwd_lane, 0.0, bbr), -jnp.where(fwd_lane, 0.0, bbi)], axis=1)
    k0 = jnp.dot(lhs0, rhs, preferred_element_type=F32, precision=hi_prec)
    k1 = jnp.dot(lhs1, rhs, preferred_element_type=F32, precision=hi_prec)
    lane_m = lax.broadcasted_iota(jnp.int32, (hch, th), 1)
    row_m = lax.broadcasted_iota(jnp.int32, (hch, th), 0)
    d_g = dcol[g]
    blocks = []
    for jj in range(t):
        a = pltpu.roll(k0, hch * jj, 1) if jj else k0
        a = jnp.where(lane_m >= hch * jj, a, 0.0)
        sh = (hch * (jj + 1)) % th
        b = pltpu.roll(k1, sh, 1) if sh else k1
        b = jnp.where(lane_m < hch * (jj + 1), b, 0.0)
        dd = jnp.where(lane_m == hch * jj + row_m, d_g, 0.0)
        blocks.append(a + b + dd)
    m_mat = jnp.concatenate(blocks, axis=0)
    m_mat = _permute_row_blocks(_rotate_slots(m_mat, rot), rot, t)
    w_mat = _permute_row_blocks(w_mat, rot, t)
    v_mat = _rotate_slots(v_mat, rot)
    return m_mat, w_mat, v_mat, lam


def _ssm_prep_kernel(lre_r, lim_r, ldt_r, btr, bti, ctr, cti, dcol,
                     m_ref, w_ref, v_ref, lam_ref, *, groups, nb):
    ins = (lre_r, lim_r, ldt_r, btr, bti, ctr, cti, dcol)
    first = lax.broadcasted_iota(jnp.int32, (2 * nb, 4 * SSM_STATE), 0) < nb
    lams = []
    for g in range(groups):
        m_mat, w_mat, v_mat, lam = _ssm_prep_group(g, g % SSM_TILE_GROUPS, *ins)
        m_ref[g] = m_mat.astype(m_ref.dtype)
        w_ref[g] = w_mat.astype(w_ref.dtype)
        v_ref[g] = v_mat.astype(v_ref.dtype)
        lams.append(jnp.broadcast_to(lam, (2 * nb, 4 * SSM_STATE)))
        if g % 2 == 1:
            lam_ref[g // 2] = jnp.where(first, lams[g - 1], lams[g])


def _ssm_prep(lre, lim, ldt, bre, bim, cre, cim, d, nb):
    _, g, p = lre.shape
    hch, t = SSM_GROUP_CH, SSM_CHUNK
    th = t * hch
    cat = lambda a: jnp.concatenate([a[0], a[1]], axis=-1)
    lre2, lim2 = cat(lre), cat(lim)
    ldt2 = jnp.repeat(ldt.T, p, axis=1)
    bt = lambda a: jnp.transpose(a, (1, 3, 0, 2)).reshape(g, hch, 2 * p)
    ct = lambda a: jnp.transpose(a, (1, 0, 3, 2)).reshape(g, 2 * p, hch)
    gp = SSM_TILE_GROUPS
    assert g % gp == 0
    rowv = pl.BlockSpec((gp, 1, 2 * p), lambda i: (i, 0, 0))
    mat = pl.BlockSpec((gp, th, th), lambda i: (i, 0, 0))
    return pl.pallas_call(
        functools.partial(_ssm_prep_kernel, groups=gp, nb=nb),
        grid=(g // gp,),
        in_specs=[rowv, rowv, rowv,
                  pl.BlockSpec((gp, hch, 2 * p), lambda i: (i, 0, 0)),
                  pl.BlockSpec((gp, hch, 2 * p), lambda i: (i, 0, 0)),
                  pl.BlockSpec((gp, 2 * p, hch), lambda i: (i, 0, 0)),
                  pl.BlockSpec((gp, 2 * p, hch), lambda i: (i, 0, 0)),
                  pl.BlockSpec((gp, hch, 1), lambda i: (i, 0, 0))],
        out_specs=[mat, mat, mat,
                   pl.BlockSpec((gp // 2, 2 * nb, 4 * p), lambda i: (i, 0, 0))],
        out_shape=[jax.ShapeDtypeStruct((g, th, th), BF16)] * 3
        + [jax.ShapeDtypeStruct((g // 2, 2 * nb, 4 * p), F32)],
        compiler_params=_cparams(1, VMEM_STREAM_MIB),
        name="ssm_prep",
    )(lre2.reshape(g, 1, 2 * p), lim2.reshape(g, 1, 2 * p), ldt2.reshape(g, 1, 2 * p),
      bt(bre), bt(bim), ct(cre), ct(cim), d.reshape(g, hch, 1))


def _ssm_kernel(u_ref, m_ref, w_ref, v_ref, lam_ref, yext_ref, y_ref, yin_s, s_s, xf_s, xb_s,
                tok_s, u2_s, *, pb, nchunk, nb):
    slab = 2 * nb
    rows = nchunk * slab
    p2 = 2 * SSM_STATE
    hch, tch = SSM_GROUP_CH, SSM_CHUNK
    half = tch * hch // 2
    lanes = 2 * pb * hch
    slots = lanes // hch
    first = (lax.broadcasted_iota(jnp.int32, (rows, 1), 0) // nb) % 2 == 0
    lane_grp = lax.broadcasted_iota(jnp.int32, (nchunk, lanes), 1) // hch

    def pair_dot(lhs, mats, q):
        ya = jnp.dot(lhs, mats[2 * q], preferred_element_type=F32)
        yb = jnp.dot(lhs, mats[2 * q + 1], preferred_element_type=F32)
        return jnp.where(first, ya, yb)

    for b in range(nb):
        for tq in range(tch):
            tok_s[pl.ds((b * tch + tq) * nchunk, nchunk), :] = (
                u_ref[pl.ds(b * nchunk * tch + tq, nchunk, stride=tch), :])
    for b in range(nb):
        for col in range(2):
            rolled = []
            for j in range(slots):
                piece = tok_s[pl.ds((b * tch + col * slots + j) * nchunk, nchunk), :]
                rolled.append(pltpu.roll(piece, hch * j, 1) if j else piece)
            for gl in range(slots):
                acc = rolled[0]
                for j in range(1, slots):
                    acc = jnp.where(lane_grp == (j + gl) % slots, rolled[j], acc)
                q, g2 = divmod(gl, 2)
                u2_s[q, col, pl.ds(g2 * nb + b, nchunk, stride=slab), :] = acc

    for q in range(pb):
        u = jnp.concatenate([u2_s[q, 0], u2_s[q, 1]], axis=1).astype(BF16)
        y0 = pair_dot(u, m_ref, q)
        yin_s[q, 0] = y0[:, :half]
        yin_s[q, 1] = y0[:, half:]
        s_s[q] = pair_dot(u, w_ref, q)

    fwd = lax.broadcasted_iota(jnp.int32, (slab, p2), 1) < SSM_STATE
    lam_r = [lam_ref[q, :, 0:p2] for q in range(pb)]
    lam_i = [lam_ref[q, :, p2:2 * p2] for q in range(pb)]

    def step(k, carry):
        kf = pl.multiple_of(k * slab, slab)
        kb = pl.multiple_of((nchunk - 1 - k) * slab, slab)
        out = []
        for q in range(pb):
            xr, xi = carry[2 * q], carry[2 * q + 1]
            xf_s[q, pl.ds(kf, slab), 0:p2] = xr
            xf_s[q, pl.ds(kf, slab), p2:2 * p2] = xi
            xb_s[q, pl.ds(kb, slab), 0:p2] = xr
            xb_s[q, pl.ds(kb, slab), p2:2 * p2] = xi
            sr = jnp.where(fwd, s_s[q, pl.ds(kf, slab), 0:p2], s_s[q, pl.ds(kb, slab), 0:p2])
            si = jnp.where(fwd, s_s[q, pl.ds(kf, slab), p2:2 * p2],
                           s_s[q, pl.ds(kb, slab), p2:2 * p2])
            out.append(lam_r[q] * xr - lam_i[q] * xi + sr)
            out.append(lam_r[q] * xi + lam_i[q] * xr + si)
        return tuple(out)

    zero = jnp.zeros((slab, p2), F32)
    lax.fori_loop(0, nchunk, step, (zero,) * (2 * pb))

    fwd_all = lax.broadcasted_iota(jnp.int32, (rows, 2 * p2), 1) % p2 < SSM_STATE
    for q in range(pb):
        x = jnp.where(fwd_all, xf_s[q], xb_s[q]).astype(BF16)
        yv = pair_dot(x, v_ref, q)
        yin_s[q, 0] = yin_s[q, 0] + yv[:, :half]
        yin_s[q, 1] = yin_s[q, 1] + yv[:, half:]

    for q in range(pb):
        for col in range(2):
            for r in range(slab):
                u2_s[q, col, pl.ds(r * nchunk, nchunk), :] = (
                    yin_s[q, col, pl.ds(r, nchunk, stride=slab), :])
    for b in range(nb):
        for tq in range(tch):
            col, i = divmod(tq, slots)
            acc = None
            for gl in range(slots):
                q, g2 = divmod(gl, 2)
                piece = u2_s[q, col, pl.ds((g2 * nb + b) * nchunk, nchunk), :]
                acc = piece if acc is None else jnp.where(lane_grp == (i + gl) % slots, piece, acc)
            if i:
                acc = pltpu.roll(acc, (-hch * i) % lanes, 1)
            tok_s[pl.ds(b * nchunk * tch + tq, nchunk, stride=tch), :] = acc
    y_ref[...] = jax.nn.gelu(tok_s[...] + yext_ref[...]).astype(y_ref.dtype)


def _ssm_core(u, m_mat, w_mat, v_mat, lam, yext, nchunk, nb):
    m, dm = u.shape
    th = SSM_CHUNK * SSM_GROUP_CH
    npair = dm // (2 * SSM_GROUP_CH)
    rows = nchunk * 2 * nb
    pb = 4
    assert npair % pb == 0 and 2 * pb * SSM_GROUP_CH == 128 and th == 256
    p4 = 4 * SSM_STATE
    mat = pl.BlockSpec((2 * pb, th, th), lambda i: (i, 0, 0))
    tok = pl.BlockSpec((m, 128), lambda i: (0, i))
    return pl.pallas_call(
        functools.partial(_ssm_kernel, pb=pb, nchunk=nchunk, nb=nb),
        grid=(npair // pb,),
        in_specs=[tok, mat, mat, mat, pl.BlockSpec((pb, 2 * nb, p4), lambda i: (i, 0, 0)), tok],
        out_specs=tok,
        out_shape=jax.ShapeDtypeStruct((m, dm), BF16),
        scratch_shapes=[pltpu.VMEM((pb, 2, rows, th // 2), F32), pltpu.VMEM((pb, rows, p4), F32),
                        pltpu.VMEM((pb, rows, p4), F32), pltpu.VMEM((pb, rows, p4), F32),
                        pltpu.VMEM((m, 128), F32), pltpu.VMEM((pb, 2, rows, th // 2), F32)],
        compiler_params=_cparams(1, VMEM_MAX_MIB),
        name="ssm_core",
    )(u, m_mat, w_mat, v_mat, lam, yext)


def _s5_mixer(h, w_in, lre, lim, ldt, bre, bim, cre, cim, d, w_glu, batch, seq):
    m, dm = h.shape
    u = _mm(h, w_in, F32, tm=_pick(m, (2048, 1024, 512, 256, 128)), tn=_pick(dm, (256, 128)))
    yb = _s5_direction(u, batch, lre[1], lim[1], ldt[1], bre[1], bim[1], cre[1], cim[1],
                       reverse=True)
    m_mat, w_mat, v_mat, lam = _ssm_prep(lre, lim, ldt, bre.at[1].set(0.0), bim.at[1].set(0.0),
                                         cre, cim, d, batch)
    yg = _ssm_core(u, m_mat, w_mat, v_mat, lam, yb.reshape(m, dm), seq // SSM_CHUNK, batch)
    return _glu(yg, w_glu, tm=_pick(m, (2048, 1024, 512, 256, 128)), tn=_pick(dm, (256, 128)))


def _recurrence_combine(left, right):
    a_l, b_l = left
    a_r, b_r = right
    return a_r * a_l, a_r * b_l + b_r


def _state_in_kernel(u_ref, w_ref, re_ref, im_ref):
    tm, lanes = u_ref.shape
    rows = re_ref.shape[1]
    shp = (1, rows, lanes)
    own = (lax.broadcasted_iota(jnp.int32, shp, 2) // (lanes // rows)
           == lax.broadcasted_iota(jnp.int32, shp, 1))
    lhs = jnp.where(own, u_ref[...][:, None, :], 0.0).reshape(tm * rows, lanes).astype(BF16)
    bu = jnp.dot(lhs, w_ref[0], preferred_element_type=F32)
    half = bu.shape[1] // 2
    re_ref[...] = bu[:, :half].reshape(tm, rows, half)
    im_ref[...] = bu[:, half:].reshape(tm, rows, half)


def _state_out_kernel(re_ref, im_ref, w_ref, y_ref):
    tm, rows, sl = re_ref.shape
    s = jnp.concatenate([re_ref[...].reshape(tm * rows, sl), im_ref[...].reshape(tm * rows, sl)],
                        axis=1).astype(BF16)
    full = jnp.dot(s, w_ref[0], preferred_element_type=F32)
    lanes = full.shape[1]
    shp = (1, rows, lanes)
    own = (lax.broadcasted_iota(jnp.int32, shp, 2) // (lanes // rows)
           == lax.broadcasted_iota(jnp.int32, shp, 1))
    y_ref[...] = jnp.sum(jnp.where(own, full.reshape(tm, rows, lanes), 0.0), axis=1)


def _s5_direction(u, batch, lam_re, lam_im, log_dt, b_re, b_im, c_re, c_im, reverse):
    m, dm = u.shape
    g, p = lam_re.shape
    hch = dm // g
    sl, rows = 128, 8
    gpr = sl // p
    nblk = g // (gpr * rows)
    lanes = gpr * rows * hch
    lam = lax.complex(lam_re, lam_im)
    dt = jnp.exp(log_dt)[:, None]
    lam_bar = jnp.exp(lam * dt)
    b_bar = ((lam_bar - 1.0) / lam)[..., None] * lax.complex(b_re, b_im)
    eye = jnp.eye(gpr, dtype=F32)
    bt = jnp.transpose(b_bar, (0, 2, 1)).reshape(nblk, rows, gpr, hch, p)
    w_in = jnp.concatenate(
        [jnp.einsum('nrghp,gk->nrghkp', part(bt), eye).reshape(nblk, lanes, sl)
         for part in (jnp.real, jnp.imag)], axis=2).astype(BF16)
    ct = jnp.transpose(lax.complex(c_re, c_im), (0, 2, 1)).reshape(nblk, rows, gpr, p, hch)
    w_out = jnp.concatenate(
        [sign * jnp.einsum('nrgpo,gk->nkprgo', part(ct), eye).reshape(nblk, sl, lanes)
         for part, sign in ((jnp.real, 1.0), (jnp.imag, -1.0))], axis=1).astype(BF16)
    tm = _pick(m, (512, 256, 128, 64, 32, 16, 8))
    state = jax.ShapeDtypeStruct((m, g * p // sl, sl), F32)
    st_spec = pl.BlockSpec((tm, rows, sl), lambda i, j: (i, j, 0))
    tok_spec = pl.BlockSpec((tm, lanes), lambda i, j: (i, j))
    bu_re, bu_im = pl.pallas_call(
        _state_in_kernel, grid=(m // tm, nblk),
        in_specs=[tok_spec, pl.BlockSpec((1, lanes, 2 * sl), lambda i, j: (j, 0, 0))],
        out_specs=[st_spec, st_spec], out_shape=[state, state],
        compiler_params=_cparams(2, VMEM_PANEL_MIB), name="ssm_state_in",
    )(u, w_in)
    scan_shape = (batch, m // batch, g * p // sl, sl)
    a = jnp.broadcast_to(lam_bar.reshape((1, 1) + scan_shape[2:]), (1,) + scan_shape[1:])
    bu = lax.complex(bu_re, bu_im).reshape(scan_shape)
    _, states = lax.associative_scan(_recurrence_combine, (a, bu), axis=1, reverse=reverse)
    return pl.pallas_call(
        _state_out_kernel, grid=(m // tm, nblk),
        in_specs=[st_spec, st_spec, pl.BlockSpec((1, 2 * sl, lanes), lambda i, j: (j, 0, 0))],
        out_specs=tok_spec,
        out_shape=jax.ShapeDtypeStruct((m, dm), F32),
        compiler_params=_cparams(2, VMEM_PANEL_MIB), name="ssm_state_out",
    )(jnp.real(states).reshape(state.shape), jnp.imag(states).reshape(state.shape), w_out)


def kernel(x, rel_bias, pre_mix_norm, post_mix_norm, pre_ffn_norm, post_ffn_norm, attn_wqkv, attn_sink, attn_wo, ssm_w_in, ssm_lambda_re, ssm_lambda_im, ssm_log_dt, ssm_b_re, ssm_b_im, ssm_c_re, ssm_c_im, ssm_d, ssm_w_glu, ffn_w_gate, ffn_w_up, ffn_conv_w, ffn_conv_b, ffn_w_down):
    batch, seq, dm = x.shape
    depth = pre_mix_norm.shape[0]
    m = batch * seq
    heads = dm // HEAD_DIM
    kvh = (attn_wqkv.shape[2] // HEAD_DIM - heads) // 2
    dff = ffn_w_gate.shape[2]
    assert seq % ATTN_BLOCK == 0 and seq % SSM_CHUNK == 0 and (2 * batch) % 8 == 0
    tm_big = _pick(m, (2048, 1024, 512, 256, 128))
    tm_mid = _pick(m, (1024, 512, 256, 128))

    xf = x.reshape(m, dm)
    h = _norm_cast(xf, pre_mix_norm[0])
    bias = _attn_bias(rel_bias)
    for i in range(depth):
        j = i // 2
        if i % 2 == 0:
            qkv = _mm(h, attn_wqkv[j], BF16, tm=tm_big, tn=_pick(attn_wqkv.shape[2], (512, 256, 128)))
            o = _attention(qkv, bias, attn_sink[j], seq, heads, kvh)
            mix = _mm(o, attn_wo[j], F32, tm=tm_big, tn=_pick(dm, (512, 256, 128)))
        else:
            mix = _s5_mixer(h, ssm_w_in[j], ssm_lambda_re[j], ssm_lambda_im[j], ssm_log_dt[j],
                            ssm_b_re[j], ssm_b_im[j], ssm_c_re[j], ssm_c_im[j], ssm_d[j],
                            ssm_w_glu[j], batch, seq)
        xf, h = _resid_norm(xf, mix, post_mix_norm[i], pre_ffn_norm[i])
        hid = _ffn_in(h, ffn_w_gate[i], ffn_w_up[i], ffn_conv_w[i], ffn_conv_b[i], seq,
                      tn=_pick(dff, (256, 128)))
        kc = dff // 2 if (dff // 2) % 128 == 0 else dff
        f = _mm(hid, ffn_w_down[i], F32, tm=tm_mid, tn=_pick(dm, (256, 128)), kc=kc,
                vmem_mib=VMEM_MAX_MIB)
        g_next = pre_mix_norm[i + 1] if i + 1 < depth else None
        xf, h = _resid_norm(xf, f, post_ffn_norm[i], g_next)
    return xf.reshape(batch, seq, dm)
```

```python
import functools
import math

import jax
import jax.numpy as jnp
from jax import lax
from jax.experimental import pallas as pl
from jax.experimental.pallas import tpu as pltpu

F32 = jnp.float32
BF16 = jnp.bfloat16

HEAD_DIM = 128
ATTN_BLOCK = 128
NUM_BUCKETS = 32
SSM_GROUP_CH = 16
SSM_STATE = 64
SSM_CHUNK = 16
SSM_TILE_GROUPS = 8
RMS_EPS = 1e-6
NEG_INF = -1e30

MIB = 1024 * 1024
VMEM_STREAM_MIB = 32
VMEM_PANEL_MIB = 56
VMEM_MAX_MIB = 60


def _cparams(n_grid_dims, vmem_mib):
    return pltpu.CompilerParams(
        dimension_semantics=("arbitrary",) * n_grid_dims,
        vmem_limit_bytes=vmem_mib * MIB,
    )


def _pick(n, prefs):
    for p in prefs:
        if p <= n and n % p == 0:
            return p
    return n


def _rms(x, g):
    return x * lax.rsqrt(jnp.mean(x * x, axis=-1, keepdims=True) + RMS_EPS) * g


def _norm_kernel(x_ref, g_ref, h_ref):
    h_ref[...] = _rms(x_ref[...], g_ref[...]).astype(h_ref.dtype)


def _norm_cast(x, g):
    m, d = x.shape
    tm = _pick(m, (256, 128, 64, 32, 16, 8))
    return pl.pallas_call(
        _norm_kernel,
        grid=(m // tm,),
        in_specs=[pl.BlockSpec((tm, d), lambda i: (i, 0)),
                  pl.BlockSpec((1, d), lambda i: (0, 0))],
        out_specs=pl.BlockSpec((tm, d), lambda i: (i, 0)),
        out_shape=jax.ShapeDtypeStruct((m, d), BF16),
        compiler_params=_cparams(1, VMEM_STREAM_MIB),
        name="norm_cast",
    )(x, g.reshape(1, d))


def _resid_norm_kernel(x_ref, m_ref, g1_ref, g2_ref, xo_ref, ho_ref):
    xn = x_ref[...] + _rms(m_ref[...], g1_ref[...])
    xo_ref[...] = xn
    ho_ref[...] = _rms(xn, g2_ref[...]).astype(ho_ref.dtype)


def _resid_kernel(x_ref, m_ref, g1_ref, xo_ref):
    xo_ref[...] = x_ref[...] + _rms(m_ref[...], g1_ref[...])


def _resid_norm(x, mix, g_post, g_next):
    m, d = x.shape
    tm = _pick(m, (128, 64, 32, 16, 8))
    row = pl.BlockSpec((tm, d), lambda i: (i, 0))
    vec = pl.BlockSpec((1, d), lambda i: (0, 0))
    if g_next is None:
        return pl.pallas_call(
            _resid_kernel, grid=(m // tm,),
            in_specs=[row, row, vec], out_specs=row,
            out_shape=jax.ShapeDtypeStruct((m, d), F32),
            compiler_params=_cparams(1, VMEM_STREAM_MIB), name="resid",
        )(x, mix, g_post.reshape(1, d)), None
    return pl.pallas_call(
        _resid_norm_kernel, grid=(m // tm,),
        in_specs=[row, row, vec, vec], out_specs=[row, row],
        out_shape=[jax.ShapeDtypeStruct((m, d), F32), jax.ShapeDtypeStruct((m, d), BF16)],
        compiler_params=_cparams(1, VMEM_STREAM_MIB), name="resid_norm",
    )(x, mix, g_post.reshape(1, d), g_next.reshape(1, d))


def _fetch_row_panel(a_hbm, a_ref, sem):
    @pl.when(pl.program_id(1) == 0)
    def _():
        tm = a_ref.shape[0]
        cp = pltpu.make_async_copy(
            a_hbm.at[pl.ds(pl.multiple_of(pl.program_id(0) * tm, tm), tm), :], a_ref, sem)
        cp.start()
        cp.wait()


def _panel_scratch(tm, k):
    return [pltpu.VMEM((tm, k), BF16), pltpu.SemaphoreType.DMA(())]


def _mm_kernel(a_hbm, w_ref, o_ref, a_ref, sem, *, kc):
    _fetch_row_panel(a_hbm, a_ref, sem)
    k = a_ref.shape[1]
    acc = None
    for k0 in range(0, k, kc):
        part = jnp.dot(a_ref[:, k0:k0 + kc], w_ref[k0:k0 + kc, :].astype(BF16),
                       preferred_element_type=F32)
        acc = part if acc is None else acc + part
    o_ref[...] = acc.astype(o_ref.dtype)


def _mm(a, w, out_dtype, tm, tn, kc=None, vmem_mib=VMEM_PANEL_MIB):
    m, k = a.shape
    n = w.shape[1]
    kc = k if kc is None else kc
    return pl.pallas_call(
        functools.partial(_mm_kernel, kc=kc),
        grid=(m // tm, n // tn),
        in_specs=[pl.BlockSpec(memory_space=pl.ANY),
                  pl.BlockSpec((k, tn), lambda i, j: (0, j))],
        out_specs=pl.BlockSpec((tm, tn), lambda i, j: (i, j)),
        out_shape=jax.ShapeDtypeStruct((m, n), out_dtype),
        scratch_shapes=_panel_scratch(tm, k),
        compiler_params=_cparams(2, vmem_mib),
        name="mm",
    )(a, w)


def _glu_kernel(a_hbm, wa_ref, wb_ref, o_ref, a_ref, sem):
    _fetch_row_panel(a_hbm, a_ref, sem)
    a = a_ref[...]
    ya = jnp.dot(a, wa_ref[...].astype(BF16), preferred_element_type=F32)
    yb = jnp.dot(a, wb_ref[...].astype(BF16), preferred_element_type=F32)
    o_ref[...] = (ya * jax.nn.sigmoid(yb)).astype(o_ref.dtype)


def _glu(a, w, tm, tn):
    m, k = a.shape
    n = w.shape[1] // 2
    nj = n // tn
    return pl.pallas_call(
        _glu_kernel,
        grid=(m // tm, nj),
        in_specs=[pl.BlockSpec(memory_space=pl.ANY),
                  pl.BlockSpec((k, tn), lambda i, j: (0, j)),
                  pl.BlockSpec((k, tn), lambda i, j: (0, j + nj))],
        out_specs=pl.BlockSpec((tm, tn), lambda i, j: (i, j)),
        out_shape=jax.ShapeDtypeStruct((m, n), F32),
        scratch_shapes=_panel_scratch(tm, k),
        compiler_params=_cparams(2, VMEM_PANEL_MIB),
        name="glu",
    )(a, w, w)


def _ffn_in_kernel(a_hbm, wg_ref, wu_ref, cw_ref, cb_ref, o_ref, a_ref, sem):
    _fetch_row_panel(a_hbm, a_ref, sem)
    a = a_ref[...]
    g = jnp.dot(a, wg_ref[...].astype(BF16), preferred_element_type=F32)
    u = jnp.dot(a, wu_ref[...].astype(BF16), preferred_element_type=F32)
    rows = g.shape[0]
    row = lax.broadcasted_iota(jnp.int32, (rows, 1), 0)
    g_prev = jnp.where(row == 0, 0.0, pltpu.roll(g, 1, 0))
    g_next = jnp.where(row == rows - 1, 0.0, pltpu.roll(g, rows - 1, 0))
    gc = cw_ref[0:1, :] * g_prev + cw_ref[1:2, :] * g + cw_ref[2:3, :] * g_next + cb_ref[...]
    o_ref[...] = (gc * jax.nn.sigmoid(gc) * u).astype(o_ref.dtype)


def _ffn_in(h, w_gate, w_up, conv_w, conv_b, seq, tn):
    m, k = h.shape
    f = w_gate.shape[1]
    return pl.pallas_call(
        _ffn_in_kernel,
        grid=(m // seq, f // tn),
        in_specs=[pl.BlockSpec(memory_space=pl.ANY),
                  pl.BlockSpec((k, tn), lambda i, j: (0, j)),
                  pl.BlockSpec((k, tn), lambda i, j: (0, j)),
                  pl.BlockSpec((3, tn), lambda i, j: (0, j)),
                  pl.BlockSpec((1, tn), lambda i, j: (0, j))],
        out_specs=pl.BlockSpec((seq, tn), lambda i, j: (i, j)),
        out_shape=jax.ShapeDtypeStruct((m, f), BF16),
        scratch_shapes=_panel_scratch(seq, k),
        compiler_params=_cparams(2, VMEM_PANEL_MIB),
        name="ffn_in",
    )(h, w_gate, w_up, conv_w, conv_b.reshape(1, f))


def _t5_bucket(rel):
    half = NUM_BUCKETS // 2
    max_exact = half // 2
    base = jnp.where(rel > 0, half, 0)
    n = jnp.abs(rel)
    nf = jnp.maximum(n, 1).astype(F32)
    large = max_exact + (jnp.log(nf / max_exact) / math.log(ATTN_BLOCK / max_exact)
                         * (half - max_exact)).astype(jnp.int32)
    large = jnp.minimum(large, half - 1)
    return base + jnp.where(n < max_exact, n, large)


def _bias_kernel(bucket_ref, inwin_ref, rbt_ref, o_ref):
    nb = rbt_ref.shape[1]
    lanes = bucket_ref.shape[1]
    onehot = (lax.broadcasted_iota(jnp.int32, (nb, lanes), 0) == bucket_ref[...]).astype(F32)
    bias = jnp.dot(rbt_ref[...], onehot, preferred_element_type=F32,
                   precision=lax.Precision.HIGHEST)
    o_ref[...] = jnp.where(inwin_ref[...] > 0, bias, NEG_INF)


def _attn_bias(rel_bias):
    nb, heads = rel_bias.shape
    blk = ATTN_BLOCK
    q_idx = jnp.arange(blk)[:, None]
    k_idx = jnp.arange(3 * blk)[None, :]
    rel = k_idx - blk - q_idx
    bucket = _t5_bucket(rel).reshape(1, 3 * blk * blk).astype(jnp.int32)
    inwin = (jnp.abs(rel) <= blk).astype(jnp.int32).reshape(1, 3 * blk * blk)
    tl = 4096
    out = pl.pallas_call(
        _bias_kernel,
        grid=(3 * blk * blk // tl,),
        in_specs=[pl.BlockSpec((1, tl), lambda i: (0, i)),
                  pl.BlockSpec((1, tl), lambda i: (0, i)),
                  pl.BlockSpec((heads, nb), lambda i: (0, 0))],
        out_specs=pl.BlockSpec((heads, tl), lambda i: (0, i)),
        out_shape=jax.ShapeDtypeStruct((heads, 3 * blk * blk), F32),
        compiler_params=_cparams(1, VMEM_STREAM_MIB),
        name="attn_bias",
    )(bucket, inwin, rel_bias.T)
    return out.reshape(heads, blk, 3 * blk)


def _attn_kernel(sink_ref, q_ref, kp_ref, ko_ref, kn_ref, vp_ref, vo_ref, vn_ref, bias_ref,
                 o_ref, *, nblk, kvh, grp):
    blk, hd = ATTN_BLOCK, HEAD_DIM
    n = pl.program_id(0) % nblk
    col = lax.broadcasted_iota(jnp.int32, (1, 3 * blk), 1)
    key_pos = (n - 1) * blk + col
    edge = jnp.where((key_pos >= 0) & (key_pos < nblk * blk), 0.0, NEG_INF)
    scale = hd ** -0.5
    heads = [[kh * grp + g for g in range(grp)] for kh in range(kvh)]
    scores = []
    for kh in range(kvh):
        ks = slice(kh * hd, (kh + 1) * hd)
        k = jnp.concatenate([kp_ref[:, ks], ko_ref[:, ks], kn_ref[:, ks]], axis=0)
        q = jnp.concatenate([q_ref[:, h * hd:(h + 1) * hd] for h in heads[kh]], axis=0)
        s = lax.dot_general(q, k, (((1,), (1,)), ((), ())), preferred_element_type=F32) * scale
        scores.append(s + bias_ref[kh * grp:(kh + 1) * grp].reshape(grp * blk, 3 * blk) + edge)
    probs, denoms = [], []
    for kh in range(kvh):
        s = scores[kh]
        sink = jnp.concatenate([jnp.full((blk, 1), sink_ref[h], F32) for h in heads[kh]], axis=0)
        mx = jnp.maximum(jnp.max(s, axis=-1, keepdims=True), sink)
        p = jnp.exp(s - mx)
        denoms.append(jnp.sum(p, axis=-1, keepdims=True) + jnp.exp(sink - mx))
        probs.append(p.astype(BF16))
    for kh in range(kvh):
        ks = slice(kh * hd, (kh + 1) * hd)
        v = jnp.concatenate([vp_ref[:, ks], vo_ref[:, ks], vn_ref[:, ks]], axis=0)
        o = jnp.dot(probs[kh], v, preferred_element_type=F32) / denoms[kh]
        for g, h in enumerate(heads[kh]):
            o_ref[:, h * hd:(h + 1) * hd] = o[g * blk:(g + 1) * blk].astype(o_ref.dtype)


def _attention(qkv, bias, sink, seq, heads, kvh):
    m = qkv.shape[0]
    blk, hd = ATTN_BLOCK, HEAD_DIM
    nblk = seq // blk
    grp = heads // kvh
    qw, kw = heads * hd, kvh * hd
    kcol, vcol = qw // kw, qw // kw + 1

    def prev(i):
        return jnp.where(i % nblk == 0, i, i - 1)

    def nxt(i):
        return jnp.where(i % nblk == nblk - 1, i, i + 1)

    kv = lambda rowf, colb: pl.BlockSpec((blk, kw), lambda i: (rowf(i), colb))
    same = lambda i: i
    return pl.pallas_call(
        functools.partial(_attn_kernel, nblk=nblk, kvh=kvh, grp=grp),
        grid=(m // blk,),
        in_specs=[pl.BlockSpec(memory_space=pltpu.SMEM),
                  pl.BlockSpec((blk, qw), lambda i: (i, 0)),
                  kv(prev, kcol), kv(same, kcol), kv(nxt, kcol),
                  kv(prev, vcol), kv(same, vcol), kv(nxt, vcol),
                  pl.BlockSpec((heads, blk, 3 * blk), lambda i: (0, 0, 0))],
        out_specs=pl.BlockSpec((blk, qw), lambda i: (i, 0)),
        out_shape=jax.ShapeDtypeStruct((m, qw), BF16),
        compiler_params=_cparams(1, VMEM_PANEL_MIB),
        name="attention",
    )(sink, qkv, qkv, qkv, qkv, qkv, qkv, qkv, bias)


def _cexp(zr, zi):
    mag = jnp.exp(zr)
    return mag * jnp.cos(zi), mag * jnp.sin(zi)


def _cpow_int(br, bi, e, nbits):
    res_r = res_i = None
    for bit in range(nbits):
        on = ((e >> bit) & 1) == 1
        fr = jnp.where(on, br, 1.0)
        fi = jnp.where(on, bi, 0.0)
        if res_r is None:
            res_r, res_i = fr, fi
        else:
            res_r, res_i = res_r * fr - res_i * fi, res_r * fi + res_i * fr
        if bit + 1 < nbits:
            br, bi = br * br - bi * bi, 2.0 * br * bi
    return res_r, res_i


def _slot_sources(rot, t):
    half = t // 2
    return [c * half + (s - rot) % half for c in range(2) for s in range(half)]


def _rotate_slots(x, rot):
    if rot == 0:
        return x
    h = x.shape[1] // 2
    sh = SSM_GROUP_CH * rot
    return jnp.concatenate([pltpu.roll(x[:, :h], sh, 1), pltpu.roll(x[:, h:], sh, 1)], axis=1)


def _permute_row_blocks(x, rot, t):
    hch = SSM_GROUP_CH
    return jnp.concatenate([x[hch * j:hch * (j + 1)] for j in _slot_sources(rot, t)], axis=0)


def _ssm_prep_group(g, rot, lre_r, lim_r, ldt_r, btr, bti, ctr, cti, dcol):
    t, hch, p = SSM_CHUNK, SSM_GROUP_CH, SSM_STATE
    th, p2 = t * hch, 2 * p
    hi_prec = lax.Precision.HIGHEST
    ar, ai = lre_r[g], lim_r[g]
    dt = jnp.exp(ldt_r[g])
    zr, zi = ar * dt, ai * dt
    lbr, lbi = _cexp(zr, zi)
    nr = lbr - 1.0
    den = ar * ar + ai * ai
    cr = (nr * ar + lbi * ai) / den
    ci = (lbi * ar - nr * ai) / den
    b_r, b_i = btr[g], bti[g]
    bbr = cr * b_r - ci * b_i
    bbi = cr * b_i + ci * b_r
    row = lax.broadcasted_iota(jnp.int32, (th, p2), 0)
    lane = lax.broadcasted_iota(jnp.int32, (th, p2), 1)
    j = row // hch
    nbits = (t - 1).bit_length()
    pr, pi = _cpow_int(lbr, lbi, jnp.where(lane < p, t - 1 - j, j), nbits)
    bt_r = jnp.concatenate([bbr] * t, axis=0)
    bt_i = jnp.concatenate([bbi] * t, axis=0)
    w_mat = jnp.concatenate([pr * bt_r - pi * bt_i, pr * bt_i + pi * bt_r], axis=1)
    l_r, l_i = lbr, lbi
    for _ in range(nbits):
        l_r, l_i = l_r * l_r - l_i * l_i, 2.0 * l_r * l_i
    lam = jnp.concatenate([l_r, l_i], axis=1)
    rowc = lax.broadcasted_iota(jnp.int32, (p2, th), 0)
    lanec = lax.broadcasted_iota(jnp.int32, (p2, th), 1)
    nn = lanec // hch
    eye = (lax.broadcasted_iota(jnp.int32, (p2, p2), 0)
           == lax.broadcasted_iota(jnp.int32, (p2, p2), 1)).astype(F32)
    to_col = lambda r: lax.dot_general(eye, r, (((1,), (1,)), ((), ())),
                                       preferred_element_type=F32, precision=hi_prec)
    lbrc, lbic = to_col(lbr), to_col(lbi)
    qr, qi = _cpow_int(lbrc, lbic, jnp.where(rowc < p, nn, t - 1 - nn), nbits)
    tile = (lax.broadcasted_iota(jnp.int32, (hch, th), 1) % hch
            == lax.broadcasted_iota(jnp.int32, (hch, th), 0)).astype(F32)
    c_r = jnp.dot(ctr[g], tile, preferred_element_type=F32, precision=hi_prec)
    c_i = jnp.dot(cti[g], tile, preferred_element_type=F32, precision=hi_prec)
    e_r = c_r * qr - c_i * qi
    e_i = c_r * qi + c_i * qr
    v_mat = jnp.concatenate([e_r * lbrc - e_i * lbic, -(e_r * lbic + e_i * lbrc)], axis=0)
    fwd_lane = lax.broadcasted_iota(jnp.int32, (hch, p2), 1) < p
    rhs = jnp.concatenate([e_r, e_i], axis=0)
    lhs0 = jnp.concatenate([jnp.where(fwd_lane, bbr, 0.0), -jnp.where(fwd_lane, bbi, 0.0)], axis=1)
    lhs1 = jnp.concatenate([jnp.where(fwd_lane, 0.0, bbr), -jnp.where(fwd_lane, 0.0, bbi)], axis=1)
    k0 = jnp.dot(lhs0, rhs, preferred_element_type=F32, precision=hi_prec)
    k1 = jnp.dot(lhs1, rhs, preferred_element_type=F32, precision=hi_prec)
    lane_m = lax.broadcasted_iota(jnp.int32, (hch, th), 1)
    row_m = lax.broadcasted_iota(jnp.int32, (hch, th), 0)
    d_g = dcol[g]
    blocks = []
    for jj in range(t):
        a = pltpu.roll(k0, hch * jj, 1) if jj else k0
        a = jnp.where(lane_m >= hch * jj, a, 0.0)
        sh = (hch * (jj + 1)) % th
        b = pltpu.roll(k1, sh, 1) if sh else k1
        b = jnp.where(lane_m < hch * (jj + 1), b, 0.0)
        dd = jnp.where(lane_m == hch * jj + row_m, d_g, 0.0)
        blocks.append(a + b + dd)
    m_mat = jnp.concatenate(blocks, axis=0)
    m_mat = _permute_row_blocks(_rotate_slots(m_mat, rot), rot, t)
    w_mat = _permute_row_blocks(w_mat, rot, t)
    v_mat = _rotate_slots(v_mat, rot)
    return m_mat, w_mat, v_mat, lam


def _ssm_prep_kernel(lre_r, lim_r, ldt_r, btr, bti, ctr, cti, dcol,
                     m_ref, w_ref, v_ref, lam_ref, *, groups, nb):
    ins = (lre_r, lim_r, ldt_r, btr, bti, ctr, cti, dcol)
    first = lax.broadcasted_iota(jnp.int32, (2 * nb, 4 * SSM_STATE), 0) < nb
    lams = []
    for g in range(groups):
        m_mat, w_mat, v_mat, lam = _ssm_prep_group(g, g % SSM_TILE_GROUPS, *ins)
        m_ref[g] = m_mat.astype(m_ref.dtype)
        w_ref[g] = w_mat.astype(w_ref.dtype)
        v_ref[g] = v_mat.astype(v_ref.dtype)
        lams.append(jnp.broadcast_to(lam, (2 * nb, 4 * SSM_STATE)))
        if g % 2 == 1:
            lam_ref[g // 2] = jnp.where(first, lams[g - 1], lams[g])


def _ssm_prep(lre, lim, ldt, bre, bim, cre, cim, d, nb):
    _, g, p = lre.shape
    hch, t = SSM_GROUP_CH, SSM_CHUNK
    th = t * hch
    cat = lambda a: jnp.concatenate([a[0], a[1]], axis=-1)
    lre2, lim2 = cat(lre), cat(lim)
    ldt2 = jnp.repeat(ldt.T, p, axis=1)
    bt = lambda a: jnp.transpose(a, (1, 3, 0, 2)).reshape(g, hch, 2 * p)
    ct = lambda a: jnp.transpose(a, (1, 0, 3, 2)).reshape(g, 2 * p, hch)
    gp = SSM_TILE_GROUPS
    assert g % gp == 0
    rowv = pl.BlockSpec((gp, 1, 2 * p), lambda i: (i, 0, 0))
    mat = pl.BlockSpec((gp, th, th), lambda i: (i, 0, 0))
    return pl.pallas_call(
        functools.partial(_ssm_prep_kernel, groups=gp, nb=nb),
        grid=(g // gp,),
        in_specs=[rowv, rowv, rowv,
                  pl.BlockSpec((gp, hch, 2 * p), lambda i: (i, 0, 0)),
                  pl.BlockSpec((gp, hch, 2 * p), lambda i: (i, 0, 0)),
                  pl.BlockSpec((gp, 2 * p, hch), lambda i: (i, 0, 0)),
                  pl.BlockSpec((gp, 2 * p, hch), lambda i: (i, 0, 0)),
                  pl.BlockSpec((gp, hch, 1), lambda i: (i, 0, 0))],
        out_specs=[mat, mat, mat,
                   pl.BlockSpec((gp // 2, 2 * nb, 4 * p), lambda i: (i, 0, 0))],
        out_shape=[jax.ShapeDtypeStruct((g, th, th), BF16)] * 3
        + [jax.ShapeDtypeStruct((g // 2, 2 * nb, 4 * p), F32)],
        compiler_params=_cparams(1, VMEM_STREAM_MIB),
        name="ssm_prep",
    )(lre2.reshape(g, 1, 2 * p), lim2.reshape(g, 1, 2 * p), ldt2.reshape(g, 1, 2 * p),
      bt(bre), bt(bim), ct(cre), ct(cim), d.reshape(g, hch, 1))


def _ssm_kernel(u_ref, m_ref, w_ref, v_ref, lam_ref, yext_ref, y_ref, yin_s, s_s, xf_s, xb_s,
                tok_s, u2_s, *, pb, nchunk, nb):
    slab = 2 * nb
    rows = nchunk * slab
    p2 = 2 * SSM_STATE
    hch, tch = SSM_GROUP_CH, SSM_CHUNK
    half = tch * hch // 2
    lanes = 2 * pb * hch
    slots = lanes // hch
    first = (lax.broadcasted_iota(jnp.int32, (rows, 1), 0) // nb) % 2 == 0
    lane_grp = lax.broadcasted_iota(jnp.int32, (nchunk, lanes), 1) // hch

    def pair_dot(lhs, mats, q):
        ya = jnp.dot(lhs, mats[2 * q], preferred_element_type=F32)
        yb = jnp.dot(lhs, mats[2 * q + 1], preferred_element_type=F32)
        return jnp.where(first, ya, yb)

    for b in range(nb):
        for tq in range(tch):
            tok_s[pl.ds((b * tch + tq) * nchunk, nchunk), :] = (
                u_ref[pl.ds(b * nchunk * tch + tq, nchunk, stride=tch), :])
    for b in range(nb):
        for col in range(2):
            rolled = []
            for j in range(slots):
                piece = tok_s[pl.ds((b * tch + col * slots + j) * nchunk, nchunk), :]
                rolled.append(pltpu.roll(piece, hch * j, 1) if j else piece)
            for gl in range(slots):
                acc = rolled[0]
                for j in range(1, slots):
                    acc = jnp.where(lane_grp == (j + gl) % slots, rolled[j], acc)
                q, g2 = divmod(gl, 2)
                u2_s[q, col, pl.ds(g2 * nb + b, nchunk, stride=slab), :] = acc

    for q in range(pb):
        u = jnp.concatenate([u2_s[q, 0], u2_s[q, 1]], axis=1).astype(BF16)
        y0 = pair_dot(u, m_ref, q)
        yin_s[q, 0] = y0[:, :half]
        yin_s[q, 1] = y0[:, half:]
        s_s[q] = pair_dot(u, w_ref, q)

    fwd = lax.broadcasted_iota(jnp.int32, (slab, p2), 1) < SSM_STATE
    lam_r = [lam_ref[q, :, 0:p2] for q in range(pb)]
    lam_i = [lam_ref[q, :, p2:2 * p2] for q in range(pb)]

    def step(k, carry):
        kf = pl.multiple_of(k * slab, slab)
        kb = pl.multiple_of((nchunk - 1 - k) * slab, slab)
        out = []
        for q in range(pb):
            xr, xi = carry[2 * q], carry[2 * q + 1]
            xf_s[q, pl.ds(kf, slab), 0:p2] = xr
            xf_s[q, pl.ds(kf, slab), p2:2 * p2] = xi
            xb_s[q, pl.ds(kb, slab), 0:p2] = xr
            xb_s[q, pl.ds(kb, slab), p2:2 * p2] = xi
            sr = jnp.where(fwd, s_s[q, pl.ds(kf, slab), 0:p2], s_s[q, pl.ds(kb, slab), 0:p2])
            si = jnp.where(fwd, s_s[q, pl.ds(kf, slab), p2:2 * p2],
                           s_s[q, pl.ds(kb, slab), p2:2 * p2])
            out.append(lam_r[q] * xr - lam_i[q] * xi + sr)
            out.append(lam_r[q] * xi + lam_i[q] * xr + si)
        return tuple(out)

    zero = jnp.zeros((slab, p2), F32)
    lax.fori_loop(0, nchunk, step, (zero,) * (2 * pb))

    fwd_all = lax.broadcasted_iota(jnp.int32, (rows, 2 * p2), 1) % p2 < SSM_STATE
    for q in range(pb):
        x = jnp.where(fwd_all, xf_s[q], xb_s[q]).astype(BF16)
        yv = pair_dot(x, v_ref, q)
        yin_s[q, 0] = yin_s[q, 0] + yv[:, :half]
        yin_s[q, 1] = yin_s[q, 1] + yv[:, half:]

    for q in range(pb):
        for col in range(2):
            for r in range(slab):
                u2_s[q, col, pl.ds(r * nchunk, nchunk), :] = (
                    yin_s[q, col, pl.ds(r, nchunk, stride=slab), :])
    for b in range(nb):
        for tq in range(tch):
            col, i = divmod(tq, slots)
            acc = None
            for gl in range(slots):
                q, g2 = divmod(gl, 2)
                piece = u2_s[q, col, pl.ds((g2 * nb + b) * nchunk, nchunk), :]
                acc = piece if acc is None else jnp.where(lane_grp == (i + gl) % slots, piece, acc)
            if i:
                acc = pltpu.roll(acc, (-hch * i) % lanes, 1)
            tok_s[pl.ds(b * nchunk * tch + tq, nchunk, stride=tch), :] = acc
    y_ref[...] = jax.nn.gelu(tok_s[...] + yext_ref[...]).astype(y_ref.dtype)


def _ssm_core(u, m_mat, w_mat, v_mat, lam, yext, nchunk, nb):
    m, dm = u.shape
    th = SSM_CHUNK * SSM_GROUP_CH
    npair = dm // (2 * SSM_GROUP_CH)
    rows = nchunk * 2 * nb
    pb = 4
    assert npair % pb == 0 and 2 * pb * SSM_GROUP_CH == 128 and th == 256
    p4 = 4 * SSM_STATE
    mat = pl.BlockSpec((2 * pb, th, th), lambda i: (i, 0, 0))
    tok = pl.BlockSpec((m, 128), lambda i: (0, i))
    return pl.pallas_call(
        functools.partial(_ssm_kernel, pb=pb, nchunk=nchunk, nb=nb),
        grid=(npair // pb,),
        in_specs=[tok, mat, mat, mat, pl.BlockSpec((pb, 2 * nb, p4), lambda i: (i, 0, 0)), tok],
        out_specs=tok,
        out_shape=jax.ShapeDtypeStruct((m, dm), BF16),
        scratch_shapes=[pltpu.VMEM((pb, 2, rows, th // 2), F32), pltpu.VMEM((pb, rows, p4), F32),
                        pltpu.VMEM((pb, rows, p4), F32), pltpu.VMEM((pb, rows, p4), F32),
                        pltpu.VMEM((m, 128), F32), pltpu.VMEM((pb, 2, rows, th // 2), F32)],
        compiler_params=_cparams(1, VMEM_MAX_MIB),
        name="ssm_core",
    )(u, m_mat, w_mat, v_mat, lam, yext)


def _s5_mixer(h, w_in, lre, lim, ldt, bre, bim, cre, cim, d, w_glu, batch, seq):
    m, dm = h.shape
    u = _mm(h, w_in, F32, tm=_pick(m, (2048, 1024, 512, 256, 128)), tn=_pick(dm, (256, 128)))
    yb = _s5_direction(u, batch, lre[1], lim[1], ldt[1], bre[1], bim[1], cre[1], cim[1],
                       reverse=True)
    m_mat, w_mat, v_mat, lam = _ssm_prep(lre, lim, ldt, bre.at[1].set(0.0), bim.at[1].set(0.0),
                                         cre, cim, d, batch)
    yg = _ssm_core(u, m_mat, w_mat, v_mat, lam, yb.reshape(m, dm), seq // SSM_CHUNK, batch)
    return _glu(yg, w_glu, tm=_pick(m, (2048, 1024, 512, 256, 128)), tn=_pick(dm, (256, 128)))


def _recurrence_combine(left, right):
    a_l, b_l = left
    a_r, b_r = right
    return a_r * a_l, a_r * b_l + b_r


def _state_in_kernel(u_ref, w_ref, re_ref, im_ref):
    tm, lanes = u_ref.shape
    rows = re_ref.shape[1]
    shp = (1, rows, lanes)
    own = (lax.broadcasted_iota(jnp.int32, shp, 2) // (lanes // rows)
           == lax.broadcasted_iota(jnp.int32, shp, 1))
    lhs = jnp.where(own, u_ref[...][:, None, :], 0.0).reshape(tm * rows, lanes).astype(BF16)
    bu = jnp.dot(lhs, w_ref[0], preferred_element_type=F32)
    half = bu.shape[1] // 2
    re_ref[...] = bu[:, :half].reshape(tm, rows, half)
    im_ref[...] = bu[:, half:].reshape(tm, rows, half)


def _state_out_kernel(re_ref, im_ref, w_ref, y_ref):
    tm, rows, sl = re_ref.shape
    s = jnp.concatenate([re_ref[...].reshape(tm * rows, sl), im_ref[...].reshape(tm * rows, sl)],
                        axis=1).astype(BF16)
    full = jnp.dot(s, w_ref[0], preferred_element_type=F32)
    lanes = full.shape[1]
    shp = (1, rows, lanes)
    own = (lax.broadcasted_iota(jnp.int32, shp, 2) // (lanes // rows)
           == lax.broadcasted_iota(jnp.int32, shp, 1))
    y_ref[...] = jnp.sum(jnp.where(own, full.reshape(tm, rows, lanes), 0.0), axis=1)


def _s5_direction(u, batch, lam_re, lam_im, log_dt, b_re, b_im, c_re, c_im, reverse):
    m, dm = u.shape
    g, p = lam_re.shape
    hch = dm // g
    sl, rows = 128, 8
    gpr = sl // p
    nblk = g // (gpr * rows)
    lanes = gpr * rows * hch
    lam = lax.complex(lam_re, lam_im)
    dt = jnp.exp(log_dt)[:, None]
    lam_bar = jnp.exp(lam * dt)
    b_bar = ((lam_bar - 1.0) / lam)[..., None] * lax.complex(b_re, b_im)
    eye = jnp.eye(gpr, dtype=F32)
    bt = jnp.transpose(b_bar, (0, 2, 1)).reshape(nblk, rows, gpr, hch, p)
    w_in = jnp.concatenate(
        [jnp.einsum('nrghp,gk->nrghkp', part(bt), eye).reshape(nblk, lanes, sl)
         for part in (jnp.real, jnp.imag)], axis=2).astype(BF16)
    ct = jnp.transpose(lax.complex(c_re, c_im), (0, 2, 1)).reshape(nblk, rows, gpr, p, hch)
    w_out = jnp.concatenate(
        [sign * jnp.einsum('nrgpo,gk->nkprgo', part(ct), eye).reshape(nblk, sl, lanes)
         for part, sign in ((jnp.real, 1.0), (jnp.imag, -1.0))], axis=1).astype(BF16)
    tm = _pick(m, (1024, 512, 256, 128, 64, 32, 16, 8))
    state = jax.ShapeDtypeStruct((m, g * p // sl, sl), F32)
    st_spec = pl.BlockSpec((tm, rows, sl), lambda i, j: (i, j, 0))
    tok_spec = pl.BlockSpec((tm, lanes), lambda i, j: (i, j))
    bu_re, bu_im = pl.pallas_call(
        _state_in_kernel, grid=(m // tm, nblk),
        in_specs=[tok_spec, pl.BlockSpec((1, lanes, 2 * sl), lambda i, j: (j, 0, 0))],
        out_specs=[st_spec, st_spec], out_shape=[state, state],
        compiler_params=_cparams(2, VMEM_PANEL_MIB), name="ssm_state_in",
    )(u, w_in)
    scan_shape = (batch, m // batch, g * p // sl, sl)
    a = jnp.broadcast_to(lam_bar.reshape((1, 1) + scan_shape[2:]), (1,) + scan_shape[1:])
    bu = lax.complex(bu_re, bu_im).reshape(scan_shape)
    _, states = lax.associative_scan(_recurrence_combine, (a, bu), axis=1, reverse=reverse)
    return pl.pallas_call(
        _state_out_kernel, grid=(m // tm, nblk),
        in_specs=[st_spec, st_spec, pl.BlockSpec((1, 2 * sl, lanes), lambda i, j: (j, 0, 0))],
        out_specs=tok_spec,
        out_shape=jax.ShapeDtypeStruct((m, dm), F32),
        compiler_params=_cparams(2, VMEM_PANEL_MIB), name="ssm_state_out",
    )(jnp.real(states).reshape(state.shape), jnp.imag(states).reshape(state.shape), w_out)


def kernel(x, rel_bias, pre_mix_norm, post_mix_norm, pre_ffn_norm, post_ffn_norm, attn_wqkv, attn_sink, attn_wo, ssm_w_in, ssm_lambda_re, ssm_lambda_im, ssm_log_dt, ssm_b_re, ssm_b_im, ssm_c_re, ssm_c_im, ssm_d, ssm_w_glu, ffn_w_gate, ffn_w_up, ffn_conv_w, ffn_conv_b, ffn_w_down):
    batch, seq, dm = x.shape
    depth = pre_mix_norm.shape[0]
    m = batch * seq
    heads = dm // HEAD_DIM
    kvh = (attn_wqkv.shape[2] // HEAD_DIM - heads) // 2
    dff = ffn_w_gate.shape[2]
    assert seq % ATTN_BLOCK == 0 and seq % SSM_CHUNK == 0 and (2 * batch) % 8 == 0
    tm_big = _pick(m, (2048, 1024, 512, 256, 128))
    tm_mid = _pick(m, (1024, 512, 256, 128))

    xf = x.reshape(m, dm)
    h = _norm_cast(xf, pre_mix_norm[0])
    bias = _attn_bias(rel_bias)
    for i in range(depth):
        j = i // 2
        if i % 2 == 0:
            qkv = _mm(h, attn_wqkv[j], BF16, tm=tm_big, tn=_pick(attn_wqkv.shape[2], (512, 256, 128)))
            o = _attention(qkv, bias, attn_sink[j], seq, heads, kvh)
            mix = _mm(o, attn_wo[j], F32, tm=tm_big, tn=_pick(dm, (512, 256, 128)))
        else:
            mix = _s5_mixer(h, ssm_w_in[j], ssm_lambda_re[j], ssm_lambda_im[j], ssm_log_dt[j],
                            ssm_b_re[j], ssm_b_im[j], ssm_c_re[j], ssm_c_im[j], ssm_d[j],
                            ssm_w_glu[j], batch, seq)
        xf, h = _resid_norm(xf, mix, post_mix_norm[i], pre_ffn_norm[i])
        hid = _ffn_in(h, ffn_w_gate[i], ffn_w_up[i], ffn_conv_w[i], ffn_conv_b[i], seq,
                      tn=_pick(dff, (256, 128)))
        kc = dff // 2 if (dff // 2) % 128 == 0 else dff
        f = _mm(hid, ffn_w_down[i], F32, tm=tm_mid, tn=_pick(dm, (256, 128)), kc=kc,
                vmem_mib=VMEM_MAX_MIB)
        g_next = pre_mix_norm[i + 1] if i + 1 < depth else None
        xf, h = _resid_norm(xf, f, post_ffn_norm[i], g_next)
    return xf.reshape(batch, seq, dm)
```

```python
import functools
import math

import jax
import jax.numpy as jnp
from jax import lax
from jax.experimental import pallas as pl
from jax.experimental.pallas import tpu as pltpu

F32 = jnp.float32
BF16 = jnp.bfloat16

HEAD_DIM = 128
ATTN_BLOCK = 128
NUM_BUCKETS = 32
SSM_GROUP_CH = 16
SSM_STATE = 64
SSM_CHUNK = 16
SSM_TILE_GROUPS = 8
RMS_EPS = 1e-6
NEG_INF = -1e30

MIB = 1024 * 1024
VMEM_STREAM_MIB = 32
VMEM_PANEL_MIB = 56
VMEM_MAX_MIB = 60


def _cparams(n_grid_dims, vmem_mib):
    return pltpu.CompilerParams(
        dimension_semantics=("arbitrary",) * n_grid_dims,
        vmem_limit_bytes=vmem_mib * MIB,
    )


def _pick(n, prefs):
    for p in prefs:
        if p <= n and n % p == 0:
            return p
    return n


def _rms(x, g):
    return x * lax.rsqrt(jnp.mean(x * x, axis=-1, keepdims=True) + RMS_EPS) * g


def _norm_kernel(x_ref, g_ref, h_ref):
    h_ref[...] = _rms(x_ref[...], g_ref[...]).astype(h_ref.dtype)


def _norm_cast(x, g):
    m, d = x.shape
    tm = _pick(m, (256, 128, 64, 32, 16, 8))
    return pl.pallas_call(
        _norm_kernel,
        grid=(m // tm,),
        in_specs=[pl.BlockSpec((tm, d), lambda i: (i, 0)),
                  pl.BlockSpec((1, d), lambda i: (0, 0))],
        out_specs=pl.BlockSpec((tm, d), lambda i: (i, 0)),
        out_shape=jax.ShapeDtypeStruct((m, d), BF16),
        compiler_params=_cparams(1, VMEM_STREAM_MIB),
        name="norm_cast",
    )(x, g.reshape(1, d))


def _resid_norm_kernel(x_ref, m_ref, g1_ref, g2_ref, xo_ref, ho_ref):
    xn = x_ref[...] + _rms(m_ref[...], g1_ref[...])
    xo_ref[...] = xn
    ho_ref[...] = _rms(xn, g2_ref[...]).astype(ho_ref.dtype)


def _resid_kernel(x_ref, m_ref, g1_ref, xo_ref):
    xo_ref[...] = x_ref[...] + _rms(m_ref[...], g1_ref[...])


def _resid_norm(x, mix, g_post, g_next):
    m, d = x.shape
    tm = _pick(m, (128, 64, 32, 16, 8))
    row = pl.BlockSpec((tm, d), lambda i: (i, 0))
    vec = pl.BlockSpec((1, d), lambda i: (0, 0))
    if g_next is None:
        return pl.pallas_call(
            _resid_kernel, grid=(m // tm,),
            in_specs=[row, row, vec], out_specs=row,
            out_shape=jax.ShapeDtypeStruct((m, d), F32),
            compiler_params=_cparams(1, VMEM_STREAM_MIB), name="resid",
        )(x, mix, g_post.reshape(1, d)), None
    return pl.pallas_call(
        _resid_norm_kernel, grid=(m // tm,),
        in_specs=[row, row, vec, vec], out_specs=[row, row],
        out_shape=[jax.ShapeDtypeStruct((m, d), F32), jax.ShapeDtypeStruct((m, d), BF16)],
        compiler_params=_cparams(1, VMEM_STREAM_MIB), name="resid_norm",
    )(x, mix, g_post.reshape(1, d), g_next.reshape(1, d))


def _fetch_row_panel(a_hbm, a_ref, sem):
    @pl.when(pl.program_id(1) == 0)
    def _():
        tm = a_ref.shape[0]
        cp = pltpu.make_async_copy(
            a_hbm.at[pl.ds(pl.multiple_of(pl.program_id(0) * tm, tm), tm), :], a_ref, sem)
        cp.start()
        cp.wait()


def _panel_scratch(tm, k):
    return [pltpu.VMEM((tm, k), BF16), pltpu.SemaphoreType.DMA(())]


def _mm_kernel(a_hbm, w_ref, o_ref, a_ref, sem, *, kc):
    _fetch_row_panel(a_hbm, a_ref, sem)
    k = a_ref.shape[1]
    acc = None
    for k0 in range(0, k, kc):
        part = jnp.dot(a_ref[:, k0:k0 + kc], w_ref[k0:k0 + kc, :].astype(BF16),
                       preferred_element_type=F32)
        acc = part if acc is None else acc + part
    o_ref[...] = acc.astype(o_ref.dtype)


def _mm(a, w, out_dtype, tm, tn, kc=None, vmem_mib=VMEM_PANEL_MIB):
    m, k = a.shape
    n = w.shape[1]
    kc = k if kc is None else kc
    return pl.pallas_call(
        functools.partial(_mm_kernel, kc=kc),
        grid=(m // tm, n // tn),
        in_specs=[pl.BlockSpec(memory_space=pl.ANY),
                  pl.BlockSpec((k, tn), lambda i, j: (0, j))],
        out_specs=pl.BlockSpec((tm, tn), lambda i, j: (i, j)),
        out_shape=jax.ShapeDtypeStruct((m, n), out_dtype),
        scratch_shapes=_panel_scratch(tm, k),
        compiler_params=_cparams(2, vmem_mib),
        name="mm",
    )(a, w)


def _glu_kernel(a_hbm, wa_ref, wb_ref, o_ref, a_ref, sem):
    _fetch_row_panel(a_hbm, a_ref, sem)
    a = a_ref[...]
    ya = jnp.dot(a, wa_ref[...].astype(BF16), preferred_element_type=F32)
    yb = jnp.dot(a, wb_ref[...].astype(BF16), preferred_element_type=F32)
    o_ref[...] = (ya * jax.nn.sigmoid(yb)).astype(o_ref.dtype)


def _glu(a, w, tm, tn):
    m, k = a.shape
    n = w.shape[1] // 2
    nj = n // tn
    return pl.pallas_call(
        _glu_kernel,
        grid=(m // tm, nj),
        in_specs=[pl.BlockSpec(memory_space=pl.ANY),
                  pl.BlockSpec((k, tn), lambda i, j: (0, j)),
                  pl.BlockSpec((k, tn), lambda i, j: (0, j + nj))],
        out_specs=pl.BlockSpec((tm, tn), lambda i, j: (i, j)),
        out_shape=jax.ShapeDtypeStruct((m, n), F32),
        scratch_shapes=_panel_scratch(tm, k),
        compiler_params=_cparams(2, VMEM_PANEL_MIB),
        name="glu",
    )(a, w, w)


def _ffn_in_kernel(a_hbm, wg_ref, wu_ref, cw_ref, cb_ref, o_ref, a_ref, sem):
    _fetch_row_panel(a_hbm, a_ref, sem)
    a = a_ref[...]
    g = jnp.dot(a, wg_ref[...].astype(BF16), preferred_element_type=F32)
    u = jnp.dot(a, wu_ref[...].astype(BF16), preferred_element_type=F32)
    rows = g.shape[0]
    row = lax.broadcasted_iota(jnp.int32, (rows, 1), 0)
    g_prev = jnp.where(row == 0, 0.0, pltpu.roll(g, 1, 0))
    g_next = jnp.where(row == rows - 1, 0.0, pltpu.roll(g, rows - 1, 0))
    gc = cw_ref[0:1, :] * g_prev + cw_ref[1:2, :] * g + cw_ref[2:3, :] * g_next + cb_ref[...]
    o_ref[...] = (gc * jax.nn.sigmoid(gc) * u).astype(o_ref.dtype)


def _ffn_in(h, w_gate, w_up, conv_w, conv_b, seq, tn):
    m, k = h.shape
    f = w_gate.shape[1]
    return pl.pallas_call(
        _ffn_in_kernel,
        grid=(m // seq, f // tn),
        in_specs=[pl.BlockSpec(memory_space=pl.ANY),
                  pl.BlockSpec((k, tn), lambda i, j: (0, j)),
                  pl.BlockSpec((k, tn), lambda i, j: (0, j)),
                  pl.BlockSpec((3, tn), lambda i, j: (0, j)),
                  pl.BlockSpec((1, tn), lambda i, j: (0, j))],
        out_specs=pl.BlockSpec((seq, tn), lambda i, j: (i, j)),
        out_shape=jax.ShapeDtypeStruct((m, f), BF16),
        scratch_shapes=_panel_scratch(seq, k),
        compiler_params=_cparams(2, VMEM_PANEL_MIB),
        name="ffn_in",
    )(h, w_gate, w_up, conv_w, conv_b.reshape(1, f))


def _t5_bucket(rel):
    half = NUM_BUCKETS // 2
    max_exact = half // 2
    base = jnp.where(rel > 0, half, 0)
    n = jnp.abs(rel)
    nf = jnp.maximum(n, 1).astype(F32)
    large = max_exact + (jnp.log(nf / max_exact) / math.log(ATTN_BLOCK / max_exact)
                         * (half - max_exact)).astype(jnp.int32)
    large = jnp.minimum(large, half - 1)
    return base + jnp.where(n < max_exact, n, large)


def _bias_kernel(bucket_ref, inwin_ref, rbt_ref, o_ref):
    nb = rbt_ref.shape[1]
    lanes = bucket_ref.shape[1]
    onehot = (lax.broadcasted_iota(jnp.int32, (nb, lanes), 0) == bucket_ref[...]).astype(F32)
    bias = jnp.dot(rbt_ref[...], onehot, preferred_element_type=F32,
                   precision=lax.Precision.HIGHEST)
    o_ref[...] = jnp.where(inwin_ref[...] > 0, bias, NEG_INF)


def _attn_bias(rel_bias):
    nb, heads = rel_bias.shape
    blk = ATTN_BLOCK
    q_idx = jnp.arange(blk)[:, None]
    k_idx = jnp.arange(3 * blk)[None, :]
    rel = k_idx - blk - q_idx
    bucket = _t5_bucket(rel).reshape(1, 3 * blk * blk).astype(jnp.int32)
    inwin = (jnp.abs(rel) <= blk).astype(jnp.int32).reshape(1, 3 * blk * blk)
    tl = 4096
    out = pl.pallas_call(
        _bias_kernel,
        grid=(3 * blk * blk // tl,),
        in_specs=[pl.BlockSpec((1, tl), lambda i: (0, i)),
                  pl.BlockSpec((1, tl), lambda i: (0, i)),
                  pl.BlockSpec((heads, nb), lambda i: (0, 0))],
        out_specs=pl.BlockSpec((heads, tl), lambda i: (0, i)),
        out_shape=jax.ShapeDtypeStruct((heads, 3 * blk * blk), F32),
        compiler_params=_cparams(1, VMEM_STREAM_MIB),
        name="attn_bias",
    )(bucket, inwin, rel_bias.T)
    return out.reshape(heads, blk, 3 * blk)


def _attn_kernel(sink_ref, q_ref, kp_ref, ko_ref, kn_ref, vp_ref, vo_ref, vn_ref, bias_ref,
                 o_ref, *, nblk, kvh, grp):
    blk, hd = ATTN_BLOCK, HEAD_DIM
    n = pl.program_id(0) % nblk
    col = lax.broadcasted_iota(jnp.int32, (1, 3 * blk), 1)
    key_pos = (n - 1) * blk + col
    edge = jnp.where((key_pos >= 0) & (key_pos < nblk * blk), 0.0, NEG_INF)
    scale = hd ** -0.5
    heads = [[kh * grp + g for g in range(grp)] for kh in range(kvh)]
    scores = []
    for kh in range(kvh):
        ks = slice(kh * hd, (kh + 1) * hd)
        k = jnp.concatenate([kp_ref[:, ks], ko_ref[:, ks], kn_ref[:, ks]], axis=0)
        q = jnp.concatenate([q_ref[:, h * hd:(h + 1) * hd] for h in heads[kh]], axis=0)
        s = lax.dot_general(q, k, (((1,), (1,)), ((), ())), preferred_element_type=F32) * scale
        scores.append(s + bias_ref[kh * grp:(kh + 1) * grp].reshape(grp * blk, 3 * blk) + edge)
    probs, denoms = [], []
    for kh in range(kvh):
        s = scores[kh]
        sink = jnp.concatenate([jnp.full((blk, 1), sink_ref[h], F32) for h in heads[kh]], axis=0)
        mx = jnp.maximum(jnp.max(s, axis=-1, keepdims=True), sink)
        p = jnp.exp(s - mx)
        denoms.append(jnp.sum(p, axis=-1, keepdims=True) + jnp.exp(sink - mx))
        probs.append(p.astype(BF16))
    for kh in range(kvh):
        ks = slice(kh * hd, (kh + 1) * hd)
        v = jnp.concatenate([vp_ref[:, ks], vo_ref[:, ks], vn_ref[:, ks]], axis=0)
        o = jnp.dot(probs[kh], v, preferred_element_type=F32) / denoms[kh]
        for g, h in enumerate(heads[kh]):
            o_ref[:, h * hd:(h + 1) * hd] = o[g * blk:(g + 1) * blk].astype(o_ref.dtype)


def _attention(qkv, bias, sink, seq, heads, kvh):
    m = qkv.shape[0]
    blk, hd = ATTN_BLOCK, HEAD_DIM
    nblk = seq // blk
    grp = heads // kvh
    qw, kw = heads * hd, kvh * hd
    kcol, vcol = qw // kw, qw // kw + 1

    def prev(i):
        return jnp.where(i % nblk == 0, i, i - 1)

    def nxt(i):
        return jnp.where(i % nblk == nblk - 1, i, i + 1)

    kv = lambda rowf, colb: pl.BlockSpec((blk, kw), lambda i: (rowf(i), colb))
    same = lambda i: i
    return pl.pallas_call(
        functools.partial(_attn_kernel, nblk=nblk, kvh=kvh, grp=grp),
        grid=(m // blk,),
        in_specs=[pl.BlockSpec(memory_space=pltpu.SMEM),
                  pl.BlockSpec((blk, qw), lambda i: (i, 0)),
                  kv(prev, kcol), kv(same, kcol), kv(nxt, kcol),
                  kv(prev, vcol), kv(same, vcol), kv(nxt, vcol),
                  pl.BlockSpec((heads, blk, 3 * blk), lambda i: (0, 0, 0))],
        out_specs=pl.BlockSpec((blk, qw), lambda i: (i, 0)),
        out_shape=jax.ShapeDtypeStruct((m, qw), BF16),
        compiler_params=_cparams(1, VMEM_PANEL_MIB),
        name="attention",
    )(sink, qkv, qkv, qkv, qkv, qkv, qkv, qkv, bias)


def _cexp(zr, zi):
    mag = jnp.exp(zr)
    return mag * jnp.cos(zi), mag * jnp.sin(zi)


def _cpow_int(br, bi, e, nbits):
    res_r = res_i = None
    for bit in range(nbits):
        on = ((e >> bit) & 1) == 1
        fr = jnp.where(on, br, 1.0)
        fi = jnp.where(on, bi, 0.0)
        if res_r is None:
            res_r, res_i = fr, fi
        else:
            res_r, res_i = res_r * fr - res_i * fi, res_r * fi + res_i * fr
        if bit + 1 < nbits:
            br, bi = br * br - bi * bi, 2.0 * br * bi
    return res_r, res_i


def _slot_sources(rot, t):
    half = t // 2
    return [c * half + (s - rot) % half for c in range(2) for s in range(half)]


def _rotate_slots(x, rot):
    if rot == 0:
        return x
    h = x.shape[1] // 2
    sh = SSM_GROUP_CH * rot
    return jnp.concatenate([pltpu.roll(x[:, :h], sh, 1), pltpu.roll(x[:, h:], sh, 1)], axis=1)


def _permute_row_blocks(x, rot, t):
    hch = SSM_GROUP_CH
    return jnp.concatenate([x[hch * j:hch * (j + 1)] for j in _slot_sources(rot, t)], axis=0)


def _ssm_prep_group(g, rot, lre_r, lim_r, ldt_r, btr, bti, ctr, cti, dcol):
    t, hch, p = SSM_CHUNK, SSM_GROUP_CH, SSM_STATE
    th, p2 = t * hch, 2 * p
    hi_prec = lax.Precision.HIGHEST
    ar, ai = lre_r[g], lim_r[g]
    dt = jnp.exp(ldt_r[g])
    zr, zi = ar * dt, ai * dt
    lbr, lbi = _cexp(zr, zi)
    nr = lbr - 1.0
    den = ar * ar + ai * ai
    cr = (nr * ar + lbi * ai) / den
    ci = (lbi * ar - nr * ai) / den
    b_r, b_i = btr[g], bti[g]
    bbr = cr * b_r - ci * b_i
    bbi = cr * b_i + ci * b_r
    row = lax.broadcasted_iota(jnp.int32, (th, p2), 0)
    lane = lax.broadcasted_iota(jnp.int32, (th, p2), 1)
    j = row // hch
    nbits = (t - 1).bit_length()
    pr, pi = _cpow_int(lbr, lbi, jnp.where(lane < p, t - 1 - j, j), nbits)
    bt_r = jnp.concatenate([bbr] * t, axis=0)
    bt_i = jnp.concatenate([bbi] * t, axis=0)
    w_mat = jnp.concatenate([pr * bt_r - pi * bt_i, pr * bt_i + pi * bt_r], axis=1)
    l_r, l_i = lbr, lbi
    for _ in range(nbits):
        l_r, l_i = l_r * l_r - l_i * l_i, 2.0 * l_r * l_i
    lam = jnp.concatenate([l_r, l_i], axis=1)
    rowc = lax.broadcasted_iota(jnp.int32, (p2, th), 0)
    lanec = lax.broadcasted_iota(jnp.int32, (p2, th), 1)
    nn = lanec // hch
    eye = (lax.broadcasted_iota(jnp.int32, (p2, p2), 0)
           == lax.broadcasted_iota(jnp.int32, (p2, p2), 1)).astype(F32)
    to_col = lambda r: lax.dot_general(eye, r, (((1,), (1,)), ((), ())),
                                       preferred_element_type=F32, precision=hi_prec)
    lbrc, lbic = to_col(lbr), to_col(lbi)
    qr, qi = _cpow_int(lbrc, lbic, jnp.where(rowc < p, nn, t - 1 - nn), nbits)
    tile = (lax.broadcasted_iota(jnp.int32, (hch, th), 1) % hch
            == lax.broadcasted_iota(jnp.int32, (hch, th), 0)).astype(F32)
    c_r = jnp.dot(ctr[g], tile, preferred_element_type=F32, precision=hi_prec)
    c_i = jnp.dot(cti[g], tile, preferred_element_type=F32, precision=hi_prec)
    e_r = c_r * qr - c_i * qi
    e_i = c_r * qi + c_i * qr
    v_mat = jnp.concatenate([e_r * lbrc - e_i * lbic, -(e_r * lbic + e_i * lbrc)], axis=0)
    fwd_lane = lax.broadcasted_iota(jnp.int32, (hch, p2), 1) < p
    rhs = jnp.concatenate([e_r, e_i], axis=0)
    lhs0 = jnp.concatenate([jnp.where(fwd_lane, bbr, 0.0), -jnp.where(fwd_lane, bbi, 0.0)], axis=1)
    lhs1 = jnp.concatenate([jnp.where(fwd_lane, 0.0, bbr), -jnp.where(fwd_lane, 0.0, bbi)], axis=1)
    k0 = jnp.dot(lhs0, rhs, preferred_element_type=F32, precision=hi_prec)
    k1 = jnp.dot(lhs1, rhs, preferred_element_type=F32, precision=hi_prec)
    lane_m = lax.broadcasted_iota(jnp.int32, (hch, th), 1)
    row_m = lax.broadcasted_iota(jnp.int32, (hch, th), 0)
    d_g = dcol[g]
    blocks = []
    for jj in range(t):
        a = pltpu.roll(k0, hch * jj, 1) if jj else k0
        a = jnp.where(lane_m >= hch * jj, a, 0.0)
        sh = (hch * (jj + 1)) % th
        b = pltpu.roll(k1, sh, 1) if sh else k1
        b = jnp.where(lane_m < hch * (jj + 1), b, 0.0)
        dd = jnp.where(lane_m == hch * jj + row_m, d_g, 0.0)
        blocks.append(a + b + dd)
    m_mat = jnp.concatenate(blocks, axis=0)
    m_mat = _permute_row_blocks(_rotate_slots(m_mat, rot), rot, t)
    w_mat = _permute_row_blocks(w_mat, rot, t)
    v_mat = _rotate_slots(v_mat, rot)
    return m_mat, w_mat, v_mat, lam


def _ssm_prep_kernel(lre_r, lim_r, ldt_r, btr, bti, ctr, cti, dcol,
                     m_ref, w_ref, v_ref, lam_ref, *, groups, nb):
    ins = (lre_r, lim_r, ldt_r, btr, bti, ctr, cti, dcol)
    first = lax.broadcasted_iota(jnp.int32, (2 * nb, 4 * SSM_STATE), 0) < nb
    lams = []
    for g in range(groups):
        m_mat, w_mat, v_mat, lam = _ssm_prep_group(g, g % SSM_TILE_GROUPS, *ins)
        m_ref[g] = m_mat.astype(m_ref.dtype)
        w_ref[g] = w_mat.astype(w_ref.dtype)
        v_ref[g] = v_mat.astype(v_ref.dtype)
        lams.append(jnp.broadcast_to(lam, (2 * nb, 4 * SSM_STATE)))
        if g % 2 == 1:
            lam_ref[g // 2] = jnp.where(first, lams[g - 1], lams[g])


def _ssm_prep(lre, lim, ldt, bre, bim, cre, cim, d, nb):
    _, g, p = lre.shape
    hch, t = SSM_GROUP_CH, SSM_CHUNK
    th = t * hch
    cat = lambda a: jnp.concatenate([a[0], a[1]], axis=-1)
    lre2, lim2 = cat(lre), cat(lim)
    ldt2 = jnp.repeat(ldt.T, p, axis=1)
    bt = lambda a: jnp.transpose(a, (1, 3, 0, 2)).reshape(g, hch, 2 * p)
    ct = lambda a: jnp.transpose(a, (1, 0, 3, 2)).reshape(g, 2 * p, hch)
    gp = SSM_TILE_GROUPS
    assert g % gp == 0
    rowv = pl.BlockSpec((gp, 1, 2 * p), lambda i: (i, 0, 0))
    mat = pl.BlockSpec((gp, th, th), lambda i: (i, 0, 0))
    return pl.pallas_call(
        functools.partial(_ssm_prep_kernel, groups=gp, nb=nb),
        grid=(g // gp,),
        in_specs=[rowv, rowv, rowv,
                  pl.BlockSpec((gp, hch, 2 * p), lambda i: (i, 0, 0)),
                  pl.BlockSpec((gp, hch, 2 * p), lambda i: (i, 0, 0)),
                  pl.BlockSpec((gp, 2 * p, hch), lambda i: (i, 0, 0)),
                  pl.BlockSpec((gp, 2 * p, hch), lambda i: (i, 0, 0)),
                  pl.BlockSpec((gp, hch, 1), lambda i: (i, 0, 0))],
        out_specs=[mat, mat, mat,
                   pl.BlockSpec((gp // 2, 2 * nb, 4 * p), lambda i: (i, 0, 0))],
        out_shape=[jax.ShapeDtypeStruct((g, th, th), BF16)] * 3
        + [jax.ShapeDtypeStruct((g // 2, 2 * nb, 4 * p), F32)],
        compiler_params=_cparams(1, VMEM_STREAM_MIB),
        name="ssm_prep",
    )(lre2.reshape(g, 1, 2 * p), lim2.reshape(g, 1, 2 * p), ldt2.reshape(g, 1, 2 * p),
      bt(bre), bt(bim), ct(cre), ct(cim), d.reshape(g, hch, 1))


def _ssm_kernel(u_ref, m_ref, w_ref, v_ref, lam_ref, yext_ref, y_ref, yin_s, s_s, xf_s, xb_s,
                tok_s, u2_s, *, pb, nchunk, nb):
    slab = 2 * nb
    rows = nchunk * slab
    p2 = 2 * SSM_STATE
    hch, tch = SSM_GROUP_CH, SSM_CHUNK
    half = tch * hch // 2
    lanes = 2 * pb * hch
    slots = lanes // hch
    first = (lax.broadcasted_iota(jnp.int32, (rows, 1), 0) // nb) % 2 == 0
    lane_grp = lax.broadcasted_iota(jnp.int32, (nchunk, lanes), 1) // hch

    def pair_dot(lhs, mats, q):
        ya = jnp.dot(lhs, mats[2 * q], preferred_element_type=F32)
        yb = jnp.dot(lhs, mats[2 * q + 1], preferred_element_type=F32)
        return jnp.where(first, ya, yb)

    for b in range(nb):
        for tq in range(tch):
            tok_s[pl.ds((b * tch + tq) * nchunk, nchunk), :] = (
                u_ref[pl.ds(b * nchunk * tch + tq, nchunk, stride=tch), :])
    for b in range(nb):
        for col in range(2):
            rolled = []
            for j in range(slots):
                piece = tok_s[pl.ds((b * tch + col * slots + j) * nchunk, nchunk), :]
                rolled.append(pltpu.roll(piece, hch * j, 1) if j else piece)
            for gl in range(slots):
                acc = rolled[0]
                for j in range(1, slots):
                    acc = jnp.where(lane_grp == (j + gl) % slots, rolled[j], acc)
                q, g2 = divmod(gl, 2)
                u2_s[q, col, pl.ds(g2 * nb + b, nchunk, stride=slab), :] = acc

    for q in range(pb):
        u = jnp.concatenate([u2_s[q, 0], u2_s[q, 1]], axis=1).astype(BF16)
        y0 = pair_dot(u, m_ref, q)
        yin_s[q, 0] = y0[:, :half]
        yin_s[q, 1] = y0[:, half:]
        s_s[q] = pair_dot(u, w_ref, q)

    fwd = lax.broadcasted_iota(jnp.int32, (slab, p2), 1) < SSM_STATE
    lam_r = [lam_ref[q, :, 0:p2] for q in range(pb)]
    lam_i = [lam_ref[q, :, p2:2 * p2] for q in range(pb)]

    def step(k, carry):
        kf = pl.multiple_of(k * slab, slab)
        kb = pl.multiple_of((nchunk - 1 - k) * slab, slab)
        out = []
        for q in range(pb):
            xr, xi = carry[2 * q], carry[2 * q + 1]
            xf_s[q, pl.ds(kf, slab), 0:p2] = xr
            xf_s[q, pl.ds(kf, slab), p2:2 * p2] = xi
            xb_s[q, pl.ds(kb, slab), 0:p2] = xr
            xb_s[q, pl.ds(kb, slab), p2:2 * p2] = xi
            sr = jnp.where(fwd, s_s[q, pl.ds(kf, slab), 0:p2], s_s[q, pl.ds(kb, slab), 0:p2])
            si = jnp.where(fwd, s_s[q, pl.ds(kf, slab), p2:2 * p2],
                           s_s[q, pl.ds(kb, slab), p2:2 * p2])
            out.append(lam_r[q] * xr - lam_i[q] * xi + sr)
            out.append(lam_r[q] * xi + lam_i[q] * xr + si)
        return tuple(out)

    zero = jnp.zeros((slab, p2), F32)
    lax.fori_loop(0, nchunk, step, (zero,) * (2 * pb))

    fwd_all = lax.broadcasted_iota(jnp.int32, (rows, 2 * p2), 1) % p2 < SSM_STATE
    for q in range(pb):
        x = jnp.where(fwd_all, xf_s[q], xb_s[q]).astype(BF16)
        yv = pair_dot(x, v_ref, q)
        yin_s[q, 0] = yin_s[q, 0] + yv[:, :half]
        yin_s[q, 1] = yin_s[q, 1] + yv[:, half:]

    for q in range(pb):
        for col in range(2):
            for r in range(slab):
                u2_s[q, col, pl.ds(r * nchunk, nchunk), :] = (
                    yin_s[q, col, pl.ds(r, nchunk, stride=slab), :])
    for b in range(nb):
        for tq in range(tch):
            col, i = divmod(tq, slots)
            acc = None
            for gl in range(slots):
                q, g2 = divmod(gl, 2)
                piece = u2_s[q, col, pl.ds((g2 * nb + b) * nchunk, nchunk), :]
                acc = piece if acc is None else jnp.where(lane_grp == (i + gl) % slots, piece, acc)
            if i:
                acc = pltpu.roll(acc, (-hch * i) % lanes, 1)
            tok_s[pl.ds(b * nchunk * tch + tq, nchunk, stride=tch), :] = acc
    y_ref[...] = jax.nn.gelu(tok_s[...] + yext_ref[...]).astype(y_ref.dtype)


def _ssm_core(u, m_mat, w_mat, v_mat, lam, yext, nchunk, nb):
    m, dm = u.shape
    th = SSM_CHUNK * SSM_GROUP_CH
    npair = dm // (2 * SSM_GROUP_CH)
    rows = nchunk * 2 * nb
    pb = 4
    assert npair % pb == 0 and 2 * pb * SSM_GROUP_CH == 128 and th == 256
    p4 = 4 * SSM_STATE
    mat = pl.BlockSpec((2 * pb, th, th), lambda i: (i, 0, 0))
    tok = pl.BlockSpec((m, 128), lambda i: (0, i))
    return pl.pallas_call(
        functools.partial(_ssm_kernel, pb=pb, nchunk=nchunk, nb=nb),
        grid=(npair // pb,),
        in_specs=[tok, mat, mat, mat, pl.BlockSpec((pb, 2 * nb, p4), lambda i: (i, 0, 0)), tok],
        out_specs=tok,
        out_shape=jax.ShapeDtypeStruct((m, dm), BF16),
        scratch_shapes=[pltpu.VMEM((pb, 2, rows, th // 2), F32), pltpu.VMEM((pb, rows, p4), F32),
                        pltpu.VMEM((pb, rows, p4), F32), pltpu.VMEM((pb, rows, p4), F32),
                        pltpu.VMEM((m, 128), F32), pltpu.VMEM((pb, 2, rows, th // 2), F32)],
        compiler_params=_cparams(1, VMEM_MAX_MIB),
        name="ssm_core",
    )(u, m_mat, w_mat, v_mat, lam, yext)


def _s5_mixer(h, w_in, lre, lim, ldt, bre, bim, cre, cim, d, w_glu, batch, seq):
    m, dm = h.shape
    u, yb = _s5_direction(h, w_in, batch, lre[1], lim[1], ldt[1], bre[1], bim[1], cre[1], cim[1],
                          reverse=True)
    m_mat, w_mat, v_mat, lam = _ssm_prep(lre, lim, ldt, bre.at[1].set(0.0), bim.at[1].set(0.0),
                                         cre, cim, d, batch)
    yg = _ssm_core(u, m_mat, w_mat, v_mat, lam, yb.reshape(m, dm), seq // SSM_CHUNK, batch)
    return _glu(yg, w_glu, tm=_pick(m, (2048, 1024, 512, 256, 128)), tn=_pick(dm, (256, 128)))


def _recurrence_combine(left, right):
    a_l, b_l = left
    a_r, b_r = right
    return a_r * a_l, a_r * b_l + b_r


def _mm_state_kernel(a_hbm, w_ref, ws_ref, u_ref, re_ref, im_ref, a_ref, sem, *, rc):
    _fetch_row_panel(a_hbm, a_ref, sem)
    u_ref[...] = jnp.dot(a_ref[...], w_ref[...].astype(BF16), preferred_element_type=F32)
    tm, lanes = u_ref.shape
    rows = re_ref.shape[1]
    shp = (1, rows, lanes)
    own = (lax.broadcasted_iota(jnp.int32, shp, 2) // (lanes // rows)
           == lax.broadcasted_iota(jnp.int32, shp, 1))
    for r0 in range(0, tm, rc):
        lhs = jnp.where(own, u_ref[r0:r0 + rc, :][:, None, :], 0.0)
        bu = jnp.dot(lhs.reshape(rc * rows, lanes).astype(BF16), ws_ref[0],
                     preferred_element_type=F32)
        half = bu.shape[1] // 2
        re_ref[r0:r0 + rc] = bu[:, :half].reshape(rc, rows, half)
        im_ref[r0:r0 + rc] = bu[:, half:].reshape(rc, rows, half)


def _state_out_kernel(re_ref, im_ref, w_ref, y_ref):
    tm, rows, sl = re_ref.shape
    s = jnp.concatenate([re_ref[...].reshape(tm * rows, sl), im_ref[...].reshape(tm * rows, sl)],
                        axis=1).astype(BF16)
    full = jnp.dot(s, w_ref[0], preferred_element_type=F32)
    lanes = full.shape[1]
    shp = (1, rows, lanes)
    own = (lax.broadcasted_iota(jnp.int32, shp, 2) // (lanes // rows)
           == lax.broadcasted_iota(jnp.int32, shp, 1))
    y_ref[...] = jnp.sum(jnp.where(own, full.reshape(tm, rows, lanes), 0.0), axis=1)


def _s5_direction(h, w_proj, batch, lam_re, lam_im, log_dt, b_re, b_im, c_re, c_im, reverse):
    m, dm = h.shape[0], w_proj.shape[1]
    g, p = lam_re.shape
    hch = dm // g
    sl, rows = 128, 8
    gpr = sl // p
    nblk = g // (gpr * rows)
    lanes = gpr * rows * hch
    lam = lax.complex(lam_re, lam_im)
    dt = jnp.exp(log_dt)[:, None]
    lam_bar = jnp.exp(lam * dt)
    b_bar = ((lam_bar - 1.0) / lam)[..., None] * lax.complex(b_re, b_im)
    eye = jnp.eye(gpr, dtype=F32)
    bt = jnp.transpose(b_bar, (0, 2, 1)).reshape(nblk, rows, gpr, hch, p)
    w_in = jnp.concatenate(
        [jnp.einsum('nrghp,gk->nrghkp', part(bt), eye).reshape(nblk, lanes, sl)
         for part in (jnp.real, jnp.imag)], axis=2).astype(BF16)
    ct = jnp.transpose(lax.complex(c_re, c_im), (0, 2, 1)).reshape(nblk, rows, gpr, p, hch)
    w_out = jnp.concatenate(
        [sign * jnp.einsum('nrgpo,gk->nkprgo', part(ct), eye).reshape(nblk, sl, lanes)
         for part, sign in ((jnp.real, 1.0), (jnp.imag, -1.0))], axis=1).astype(BF16)
    tm = _pick(m, (1024, 512, 256, 128, 64, 32, 16, 8))
    state = jax.ShapeDtypeStruct((m, g * p // sl, sl), F32)
    st_spec = pl.BlockSpec((tm, rows, sl), lambda i, j: (i, j, 0))
    tok_spec = pl.BlockSpec((tm, lanes), lambda i, j: (i, j))
    kdim = h.shape[1]
    u, bu_re, bu_im = pl.pallas_call(
        functools.partial(_mm_state_kernel, rc=_pick(tm, (256, 128, 64, 32, 16, 8))),
        grid=(m // tm, nblk),
        in_specs=[pl.BlockSpec(memory_space=pl.ANY),
                  pl.BlockSpec((kdim, lanes), lambda i, j: (0, j)),
                  pl.BlockSpec((1, lanes, 2 * sl), lambda i, j: (j, 0, 0))],
        out_specs=[tok_spec, st_spec, st_spec],
        out_shape=[jax.ShapeDtypeStruct((m, dm), F32), state, state],
        scratch_shapes=_panel_scratch(tm, kdim),
        compiler_params=_cparams(2, VMEM_PANEL_MIB), name="mm_state_in",
    )(h, w_proj, w_in)
    scan_shape = (batch, m // batch, g * p // sl, sl)
    a = jnp.broadcast_to(lam_bar.reshape((1, 1) + scan_shape[2:]), (1,) + scan_shape[1:])
    bu = lax.complex(bu_re, bu_im).reshape(scan_shape)
    _, states = lax.associative_scan(_recurrence_combine, (a, bu), axis=1, reverse=reverse)
    return u, pl.pallas_call(
        _state_out_kernel, grid=(m // tm, nblk),
        in_specs=[st_spec, st_spec, pl.BlockSpec((1, 2 * sl, lanes), lambda i, j: (j, 0, 0))],
        out_specs=tok_spec,
        out_shape=jax.ShapeDtypeStruct((m, dm), F32),
        compiler_params=_cparams(2, VMEM_PANEL_MIB), name="ssm_state_out",
    )(jnp.real(states).reshape(state.shape), jnp.imag(states).reshape(state.shape), w_out)


def kernel(x, rel_bias, pre_mix_norm, post_mix_norm, pre_ffn_norm, post_ffn_norm, attn_wqkv, attn_sink, attn_wo, ssm_w_in, ssm_lambda_re, ssm_lambda_im, ssm_log_dt, ssm_b_re, ssm_b_im, ssm_c_re, ssm_c_im, ssm_d, ssm_w_glu, ffn_w_gate, ffn_w_up, ffn_conv_w, ffn_conv_b, ffn_w_down):
    batch, seq, dm = x.shape
    depth = pre_mix_norm.shape[0]
    m = batch * seq
    heads = dm // HEAD_DIM
    kvh = (attn_wqkv.shape[2] // HEAD_DIM - heads) // 2
    dff = ffn_w_gate.shape[2]
    assert seq % ATTN_BLOCK == 0 and seq % SSM_CHUNK == 0 and (2 * batch) % 8 == 0
    tm_big = _pick(m, (2048, 1024, 512, 256, 128))
    tm_mid = _pick(m, (1024, 512, 256, 128))

    xf = x.reshape(m, dm)
    h = _norm_cast(xf, pre_mix_norm[0])
    bias = _attn_bias(rel_bias)
    for i in range(depth):
        j = i // 2
        if i % 2 == 0:
            qkv = _mm(h, attn_wqkv[j], BF16, tm=tm_big, tn=_pick(attn_wqkv.shape[2], (512, 256, 128)))
            o = _attention(qkv, bias, attn_sink[j], seq, heads, kvh)
            mix = _mm(o, attn_wo[j], F32, tm=tm_big, tn=_pick(dm, (512, 256, 128)))
        else:
            mix = _s5_mixer(h, ssm_w_in[j], ssm_lambda_re[j], ssm_lambda_im[j], ssm_log_dt[j],
                            ssm_b_re[j], ssm_b_im[j], ssm_c_re[j], ssm_c_im[j], ssm_d[j],
                            ssm_w_glu[j], batch, seq)
        xf, h = _resid_norm(xf, mix, post_mix_norm[i], pre_ffn_norm[i])
        hid = _ffn_in(h, ffn_w_gate[i], ffn_w_up[i], ffn_conv_w[i], ffn_conv_b[i], seq,
                      tn=_pick(dff, (256, 128)))
        kc = dff // 2 if (dff // 2) % 128 == 0 else dff
        f = _mm(hid, ffn_w_down[i], F32, tm=tm_mid, tn=_pick(dm, (256, 128)), kc=kc,
                vmem_mib=VMEM_MAX_MIB)
        g_next = pre_mix_norm[i + 1] if i + 1 < depth else None
        xf, h = _resid_norm(xf, f, post_ffn_norm[i], g_next)
    return xf.reshape(batch, seq, dm)
```
